```python
import jax, jax.numpy as jnp
from jax import lax
import numpy as np

D_MODEL = 2048
BATCH = 4
SEQ = 2048
DEPTH = 2
DEC_BATCH = 128
DEC_SEQ = 8
PAST_LEN = 16384
PAGE_SIZE = 128

N_MIXERS = 2
N_CONV_LAYERS = (DEPTH + 1) // 2
N_SSD_LAYERS = DEPTH // 2
PLE_DIM = 256
D_FF = -(-8 * D_MODEL // (3 * 256)) * 256
SC_WIDTH = 3
D_INNER = 2 * D_MODEL
SSD_HEAD_DIM = 64
SSD_HEADS = D_INNER // SSD_HEAD_DIM
SSD_GROUPS = 8
SSD_STATE = 128
SSD_CONV = 4
SSD_GN = SSD_GROUPS * SSD_STATE
SSD_CONV_DIM = D_INNER + 2 * SSD_GN
SSD_IN_DIM = D_INNER + SSD_CONV_DIM + SSD_HEADS
SSD_CHUNK = 128
EPS = 1e-6

kernel_name = "hybrid_shortconv_ssd_decoder_step"


def rmsnorm(x, g):
    xf = x.astype(jnp.float32)
    xf = xf * lax.rsqrt(jnp.mean(xf * xf, axis=-1, keepdims=True) + EPS)
    return (xf * g.astype(jnp.float32)).astype(x.dtype)


def causal_dwconv(u, buf, w):
    K = w.shape[0]
    L = u.shape[1]
    up = jnp.concatenate([buf.astype(u.dtype), u], axis=1)
    out = up[:, 0:L] * w[0]
    for k in range(1, K):
        out = out + up[:, k:k + L] * w[k]
    return out, up[:, L:]


def short_conv_mixer(h, buf, w_in, w_conv, w_out):
    proj = h @ w_in
    bg, cg, v = jnp.split(proj, 3, axis=-1)
    u = cg * v
    conv, new_buf = causal_dwconv(u, buf, w_conv)
    return (bg * conv) @ w_out, new_buf


def ssd_scan(x, dt, A, Bm, Cm, h0):
    b, L, H, P = x.shape
    G, N = SSD_GROUPS, SSD_STATE
    R = H // G
    Q = SSD_CHUNK if L % SSD_CHUNK == 0 else L
    nc = L // Q
    f32 = jnp.float32
    xc = x.astype(f32).reshape(b, nc, Q, G, R, P)
    dtc = dt.reshape(b, nc, Q, G, R)
    Bc = Bm.astype(f32).reshape(b, nc, Q, G, N)
    Cc = Cm.astype(f32).reshape(b, nc, Q, G, N)
    a_cum = jnp.cumsum(dtc * A.reshape(G, R), axis=2)
    xdt = xc * dtc[..., None]
    causal = jnp.tril(jnp.ones((Q, Q), dtype=bool))
    seg = a_cum[:, :, :, None] - a_cum[:, :, None, :]
    decay = jnp.exp(jnp.where(causal[:, :, None, None], seg, -jnp.inf))
    cb = jnp.einsum('bctgn,bcsgn->bctsg', Cc, Bc)
    y_intra = jnp.einsum('bctsg,bctsgr,bcsgrp->bctgrp', cb, decay, xdt)
    decay_end = jnp.exp(a_cum[:, :, -1:] - a_cum)
    s_chunk = jnp.einsum('bcsgr,bcsgn,bcsgrp->bcgrpn', decay_end, Bc, xdt)
    chunk_decay = jnp.exp(a_cum[:, :, -1])

    def step(hs, inp):
        s_c, d_c = inp
        return d_c[..., None, None] * hs + s_c, hs

    h0g = h0.astype(f32).reshape(b, G, R, P, N)
    h_last, h_prev = lax.scan(step, h0g, (jnp.moveaxis(s_chunk, 1, 0), jnp.moveaxis(chunk_decay, 1, 0)))
    h_prev = jnp.moveaxis(h_prev, 0, 1)
    y_inter = jnp.einsum('bctgn,bcgrpn,bctgr->bctgrp', Cc, h_prev, jnp.exp(a_cum))
    y = (y_intra + y_inter).reshape(b, L, H, P).astype(x.dtype)
    return y, h_last.reshape(b, H, P, N).astype(h0.dtype)


def ssd_mixer(h, conv_buf, ssm_state, w_in, conv_w, conv_b, dt_bias, a_log, d_skip, norm_g, w_out):
    b, L, _ = h.shape
    zxbcdt = h @ w_in
    z = zxbcdt[..., :D_INNER]
    xbc = zxbcdt[..., D_INNER:D_INNER + SSD_CONV_DIM]
    dt_raw = zxbcdt[..., D_INNER + SSD_CONV_DIM:]
    xbc_c, new_conv = causal_dwconv(xbc, conv_buf, conv_w)
    xbc_c = jax.nn.silu(xbc_c + conv_b)
    xs = xbc_c[..., :D_INNER].reshape(b, L, SSD_HEADS, SSD_HEAD_DIM)
    Bm = xbc_c[..., D_INNER:D_INNER + SSD_GN].reshape(b, L, SSD_GROUPS, SSD_STATE)
    Cm = xbc_c[..., D_INNER + SSD_GN:].reshape(b, L, SSD_GROUPS, SSD_STATE)
    dt = jax.nn.softplus(dt_raw.astype(jnp.float32) + dt_bias.astype(jnp.float32))
    A = -jnp.exp(a_log.astype(jnp.float32))
    y, new_state = ssd_scan(xs, dt, A, Bm, Cm, ssm_state)
    y = y + xs * d_skip[:, None]
    gated = (y.reshape(b, L, D_INNER) * jax.nn.silu(z)).astype(jnp.float32)
    gated = gated.reshape(b, L, SSD_GROUPS, D_INNER // SSD_GROUPS)
    gated = gated * lax.rsqrt(jnp.mean(gated * gated, axis=-1, keepdims=True) + EPS)
    gated = gated * norm_g.astype(jnp.float32).reshape(SSD_GROUPS, -1)
    out = gated.reshape(b, L, D_INNER).astype(h.dtype) @ w_out
    return out, new_conv, new_state


def swiglu(h, w_gate, w_up, w_down):
    return (jax.nn.silu(h @ w_gate) * (h @ w_up)) @ w_down


def trunk(x, p, sc_bufs, ssd_bufs, ssd_states, g_mix, g_ffn, g_ple, g_final,
          sc_w_in, sc_w_conv, sc_w_out, ssd_w_in, ssd_conv_w, ssd_conv_b, ssd_dt_bias,
          ssd_a_log, ssd_d, ssd_norm_g, ssd_w_out, ffn_w_gate, ffn_w_up, ffn_w_down,
          ple_w_proj, ple_w_gate):
    h = x
    new_sc, new_ssd_conv, new_ssd = [], [], []
    for i in range(DEPTH):
        j = i // N_MIXERS
        hn = rmsnorm(h, g_mix[i])
        if i % N_MIXERS == 0:
            y, nb = short_conv_mixer(hn, sc_bufs[j], sc_w_in[j], sc_w_conv[j], sc_w_out[j])
            new_sc.append(nb)
        else:
            y, nc_, ns = ssd_mixer(hn, ssd_bufs[j], ssd_states[j], ssd_w_in[j], ssd_conv_w[j],
                                   ssd_conv_b[j], ssd_dt_bias[j], ssd_a_log[j], ssd_d[j],
                                   ssd_norm_g[j], ssd_w_out[j])
            new_ssd_conv.append(nc_)
            new_ssd.append(ns)
        h = h + y
        h = h + swiglu(rmsnorm(h, g_ffn[i]), ffn_w_gate[i], ffn_w_up[i], ffn_w_down[i])
        gate = jax.nn.sigmoid(rmsnorm(h, g_ple[i]) @ ple_w_gate[i])
        h = h + (p[i] @ ple_w_proj[i]) * gate
    return rmsnorm(h, g_final), jnp.stack(new_sc), jnp.stack(new_ssd_conv), jnp.stack(new_ssd)


def setup_inputs(seed: int = 0) -> dict:
    key = jax.random.key(seed)
    ks = jax.random.split(key, 32)
    f32 = jnp.float32
    nrm = lambda k, shape, s: jax.random.normal(k, shape, f32) * s
    dt0 = jnp.exp(jax.random.uniform(ks[17], (N_SSD_LAYERS, SSD_HEADS), f32)
                  * (np.log(0.1) - np.log(0.001)) + np.log(0.001))
    return {
        "x_prompt": nrm(ks[0], (BATCH, SEQ, D_MODEL), 1.0),
        "x_sample": nrm(ks[1], (DEC_BATCH, DEC_SEQ, D_MODEL), 1.0),
        "p_prompt": nrm(ks[2], (DEPTH, BATCH, SEQ, PLE_DIM), 1.0),
        "p_sample": nrm(ks[3], (DEPTH, DEC_BATCH, DEC_SEQ, PLE_DIM), 1.0),
        "state_sc_conv": nrm(ks[4], (N_CONV_LAYERS, DEC_BATCH, SC_WIDTH - 1, D_MODEL), 1.0),
        "state_ssd_conv": nrm(ks[5], (N_SSD_LAYERS, DEC_BATCH, SSD_CONV - 1, SSD_CONV_DIM), 1.0),
        "state_ssd": nrm(ks[6], (N_SSD_LAYERS, DEC_BATCH, SSD_HEADS, SSD_HEAD_DIM, SSD_STATE), 0.1),
        "g_mix": 1.0 + nrm(ks[7], (DEPTH, D_MODEL), 0.02),
        "g_ffn": 1.0 + nrm(ks[8], (DEPTH, D_MODEL), 0.02),
        "g_ple": 1.0 + nrm(ks[9], (DEPTH, D_MODEL), 0.02),
        "g_final": 1.0 + nrm(ks[10], (D_MODEL,), 0.02),
        "sc_w_in": nrm(ks[11], (N_CONV_LAYERS, D_MODEL, 3 * D_MODEL), D_MODEL ** -0.5),
        "sc_w_conv": nrm(ks[12], (N_CONV_LAYERS, SC_WIDTH, D_MODEL), SC_WIDTH ** -0.5),
        "sc_w_out": nrm(ks[13], (N_CONV_LAYERS, D_MODEL, D_MODEL), D_MODEL ** -0.5),
        "ssd_w_in": nrm(ks[14], (N_SSD_LAYERS, D_MODEL, SSD_IN_DIM), D_MODEL ** -0.5),
        "ssd_conv_w": nrm(ks[15], (N_SSD_LAYERS, SSD_CONV, SSD_CONV_DIM), SSD_CONV ** -0.5),
        "ssd_conv_b": nrm(ks[16], (N_SSD_LAYERS, SSD_CONV_DIM), 0.01),
        "ssd_dt_bias": dt0 + jnp.log(-jnp.expm1(-dt0)),
        "ssd_a_log": jnp.log(jax.random.uniform(ks[18], (N_SSD_LAYERS, SSD_HEADS), f32, 1.0, 16.0)),
        "ssd_d": 1.0 + nrm(ks[19], (N_SSD_LAYERS, SSD_HEADS), 0.02),
        "ssd_norm_g": 1.0 + nrm(ks[20], (N_SSD_LAYERS, D_INNER), 0.02),
        "ssd_w_out": nrm(ks[21], (N_SSD_LAYERS, D_INNER, D_MODEL), D_INNER ** -0.5),
        "ffn_w_gate": nrm(ks[22], (DEPTH, D_MODEL, D_FF), D_MODEL ** -0.5),
        "ffn_w_up": nrm(ks[23], (DEPTH, D_MODEL, D_FF), D_MODEL ** -0.5),
        "ffn_w_down": nrm(ks[24], (DEPTH, D_FF, D_MODEL), D_FF ** -0.5),
        "ple_w_proj": nrm(ks[25], (DEPTH, PLE_DIM, D_MODEL), PLE_DIM ** -0.5),
        "ple_w_gate": nrm(ks[26], (DEPTH, D_MODEL, D_MODEL), D_MODEL ** -0.5),
    }


def reference(x_prompt, x_sample, p_prompt, p_sample, state_sc_conv, state_ssd_conv, state_ssd,
              g_mix, g_ffn, g_ple, g_final, sc_w_in, sc_w_conv, sc_w_out, ssd_w_in, ssd_conv_w,
              ssd_conv_b, ssd_dt_bias, ssd_a_log, ssd_d, ssd_norm_g, ssd_w_out, ffn_w_gate,
              ffn_w_up, ffn_w_down, ple_w_proj, ple_w_gate):
    b0 = x_prompt.shape[0]
    dt_ = x_prompt.dtype
    sc0 = jnp.zeros((N_CONV_LAYERS, b0, SC_WIDTH - 1, D_MODEL), dt_)
    ssdc0 = jnp.zeros((N_SSD_LAYERS, b0, SSD_CONV - 1, SSD_CONV_DIM), dt_)
    ssd0 = jnp.zeros((N_SSD_LAYERS, b0, SSD_HEADS, SSD_HEAD_DIM, SSD_STATE), state_ssd.dtype)
    weights = (g_mix, g_ffn, g_ple, g_final, sc_w_in, sc_w_conv, sc_w_out, ssd_w_in, ssd_conv_w,
               ssd_conv_b, ssd_dt_bias, ssd_a_log, ssd_d, ssd_norm_g, ssd_w_out, ffn_w_gate,
               ffn_w_up, ffn_w_down, ple_w_proj, ple_w_gate)
    y_prompt, scp, ssdcp, ssdp = trunk(x_prompt, p_prompt, sc0, ssdc0, ssd0, *weights)
    y_sample, scs, ssdcs, ssds = trunk(x_sample, p_sample, state_sc_conv, state_ssd_conv, state_ssd, *weights)
    return (y_prompt, y_sample, scp, scs, ssdcp, ssdcs, ssdp, ssds)
```

```python
import functools

import jax
import jax.numpy as jnp
from jax import lax
from jax.experimental import pallas as pl
from jax.experimental.pallas import tpu as pltpu

F32 = jnp.float32
BF16 = jnp.bfloat16
EPS = 1e-6
MASKED = -1e30
V7X_LANES = 128
V7X_SUBLANES = 8
V7X_VMEM_LIMIT = 56 * 1024 * 1024

ROW_TILE = 1024
PLE_ROW_TILE = 512
COL_TILE = 512
SSD_CHUNK = 128


def _params(*sem):
    return pltpu.CompilerParams(dimension_semantics=sem, vmem_limit_bytes=V7X_VMEM_LIMIT)


def _dot(a, b):
    return jnp.dot(a, b, preferred_element_type=F32)


def _dot_nt(a, b):
    return lax.dot_general(a, b, (((1,), (1,)), ((), ())), preferred_element_type=F32)


def _dot_tn(a, b):
    return lax.dot_general(a, b, (((0,), (0,)), ((), ())), preferred_element_type=F32)


def _split3(a):
    a1 = a.astype(BF16)
    r1 = a - a1.astype(F32)
    a2 = r1.astype(BF16)
    a3 = (r1 - a2.astype(F32)).astype(BF16)
    return a3, a2, a1


def _dot01_rhs(a, e):
    p3, p2, p1 = _split3(a)
    return (_dot(p3, e) + _dot(p2, e)) + _dot(p1, e)


def _dot01_lhs(t, a):
    p3, p2, p1 = _split3(a)
    return (_dot(t, p3) + _dot(t, p2)) + _dot(t, p1)


def _rmsnorm(x, g):
    ms = jnp.mean(x * x, axis=-1, keepdims=True)
    return x * lax.rsqrt(ms + EPS) * g


def _softplus(x):
    return jnp.maximum(x, 0.0) + jnp.log1p(jnp.exp(-jnp.abs(x)))


def _pick_rows(i, n_prompt_tiles, prompt_ref, sample_ref):
    return jnp.where(i < n_prompt_tiles, prompt_ref[...], sample_ref[...])


def _pair_specs(block, n_prompt_tiles, col_of=lambda *_: 0):
    last = n_prompt_tiles - 1
    return [
        pl.BlockSpec(block, lambda i, *r: (jnp.minimum(i, last), col_of(i, *r))),
        pl.BlockSpec(block, lambda i, *r: (jnp.maximum(i - n_prompt_tiles, 0), col_of(i, *r))),
    ]


def _conv_prompt(u, taps, carry):
    k = taps.shape[0]
    row = lax.broadcasted_iota(jnp.int32, u.shape, 0)
    out = taps[k - 1:k, :] * u
    for d in range(1, k):
        sh = pltpu.roll(u, d, 0)
        for r in range(d):
            c = V7X_SUBLANES - d + r
            sh = jnp.where(row == r, carry[c:c + 1, :], sh)
        out = out + taps[k - 1 - d:k - d, :] * sh
    return out


def _conv_sample(u, taps, prev, seq_len):
    k = taps.shape[0]
    row = lax.broadcasted_iota(jnp.int32, u.shape, 0)
    t = row % seq_len
    out = taps[k - 1:k, :] * u
    for d in range(1, k):
        merged = jnp.where(t >= seq_len - d, prev, u)
        out = out + taps[k - 1 - d:k - d, :] * pltpu.roll(merged, d, 0)
    return out


def _shifted_state(buf, seq_len):
    n_seq, km1, c = buf.shape
    padded = jnp.pad(buf, ((0, 0), (seq_len - km1, 0), (0, 0))).reshape(n_seq * seq_len, c)
    return jnp.roll(padded, -seq_len, axis=0)


def _sc_in_kernel(h_ref, g_ref, wb_ref, wc_ref, wv_ref, taps_ref, prev_ref,
                  gated_ref, tail_ref, usamp_ref, hn_sc, carry_sc,
                  *, n_prompt_tiles, tiles_per_seq, sample_len):
    i = pl.program_id(0)
    j = pl.program_id(1)

    @pl.when(j == 0)
    def _():
        hn_sc[...] = _rmsnorm(h_ref[...], g_ref[...]).astype(BF16)

    hn = hn_sc[...]
    bg = _dot(hn, wb_ref[...])
    u = _dot(hn, wc_ref[...]) * _dot(hn, wv_ref[...])
    taps = taps_ref[...]
    tm = u.shape[0]
    tail = u[tm - V7X_SUBLANES:, :]
    tail_ref[...] = tail

    @pl.when(i < n_prompt_tiles)
    def _():
        carry = jnp.where(i % tiles_per_seq == 0, 0.0, carry_sc[j])
        gated_ref[...] = (bg * _conv_prompt(u, taps, carry)).astype(BF16)
        carry_sc[j] = tail

    @pl.when(i >= n_prompt_tiles)
    def _():
        gated_ref[...] = (bg * _conv_sample(u, taps, prev_ref[...], sample_len)).astype(BF16)
        usamp_ref[...] = u


def _short_conv_in(h, g, w_in, taps, prev, *, n_prompt_rows, seq_len_p, seq_len_s):
    m, d = h.shape
    mp, ms = n_prompt_rows, m - n_prompt_rows
    tm, tn = ROW_TILE, COL_TILE
    assert mp % tm == 0 and ms == tm and seq_len_p % tm == 0 and tm % seq_len_s == 0
    assert d % tn == 0 and taps.shape[0] - 1 <= min(seq_len_s, V7X_SUBLANES)
    npt, nj = mp // tm, d // tn
    n_tiles = npt + 1
    samp_col = lambda i, j: jnp.where(i >= npt, j, 0)
    kern = functools.partial(_sc_in_kernel, n_prompt_tiles=npt, tiles_per_seq=seq_len_p // tm,
                             sample_len=seq_len_s)
    return pl.pallas_call(
        kern,
        grid=(n_tiles, nj),
        in_specs=[
            pl.BlockSpec((tm, d), lambda i, j: (i, 0)),
            pl.BlockSpec((1, d), lambda i, j: (0, 0)),
            pl.BlockSpec((d, tn), lambda i, j: (0, j)),
            pl.BlockSpec((d, tn), lambda i, j: (0, nj + j)),
            pl.BlockSpec((d, tn), lambda i, j: (0, 2 * nj + j)),
            pl.BlockSpec((taps.shape[0], tn), lambda i, j: (0, j)),
            pl.BlockSpec((tm, tn), lambda i, j: (0, samp_col(i, j))),
        ],
        out_specs=[
            pl.BlockSpec((tm, tn), lambda i, j: (i, j)),
            pl.BlockSpec((V7X_SUBLANES, tn), lambda i, j: (i, j)),
            pl.BlockSpec((tm, tn), lambda i, j: (0, samp_col(i, j))),
        ],
        out_shape=[
            jax.ShapeDtypeStruct((mp + ms, d), BF16),
            jax.ShapeDtypeStruct((n_tiles * V7X_SUBLANES, d), F32),
            jax.ShapeDtypeStruct((ms, d), F32),
        ],
        scratch_shapes=[pltpu.VMEM((tm, d), BF16), pltpu.VMEM((nj, V7X_SUBLANES, tn), F32)],
        compiler_params=_params("arbitrary", "arbitrary"),
        name="short_conv_in",
    )(h, g, w_in, w_in, w_in, taps, prev)


def _res_kernel(*refs, n_prompt_tiles):
    res_ref, *a_refs, w_ref, out_ref = refs

    def body(a_ref):
        out_ref[...] = res_ref[...] + _dot(a_ref[...], w_ref[...])

    if len(a_refs) == 1:
        body(a_refs[0])
        return
    i = pl.program_id(0)
    pl.when(i < n_prompt_tiles)(lambda: body(a_refs[0]))
    pl.when(i >= n_prompt_tiles)(lambda: body(a_refs[1]))


def _matmul_residual(res, a, w, *, n_prompt_rows):
    a_pair = isinstance(a, tuple)
    k, n = w.shape
    m = res.shape[0]
    tm, tn = ROW_TILE, COL_TILE
    assert n_prompt_rows % tm == 0 and m % tm == 0 and n % tn == 0
    npt = n_prompt_rows // tm
    a_specs = (_pair_specs((tm, k), npt) if a_pair else [pl.BlockSpec((tm, k), lambda i, j: (i, 0))])
    return pl.pallas_call(
        functools.partial(_res_kernel, n_prompt_tiles=npt),
        grid=(m // tm, n // tn),
        in_specs=[pl.BlockSpec((tm, tn), lambda i, j: (i, j))] + a_specs
        + [pl.BlockSpec((k, tn), lambda i, j: (0, j))],
        out_specs=pl.BlockSpec((tm, tn), lambda i, j: (i, j)),
        out_shape=jax.ShapeDtypeStruct((m, n), F32),
        compiler_params=_params("arbitrary", "arbitrary"),
        name="matmul_residual",
    )(res, *(a if a_pair else (a,)), w)


def _ffn_kernel(h_ref, g_ref, wg_ref, wu_ref, wd_ref, out_ref, hn_sc):
    @pl.when(pl.program_id(1) == 0)
    def _():
        h = h_ref[...]
        hn_sc[...] = _rmsnorm(h, g_ref[...]).astype(BF16)
        out_ref[...] = h

    hn = hn_sc[...]
    gate = _dot(hn, wg_ref[...])
    act = (gate * jax.nn.sigmoid(gate) * _dot(hn, wu_ref[...])).astype(BF16)
    out_ref[...] += _dot(act, wd_ref[...])


def _ffn(h, g, w_gate, w_up, w_down):
    m, d = h.shape
    f = w_gate.shape[1]
    tm, tf = ROW_TILE, COL_TILE
    assert m % tm == 0 and f % tf == 0
    return pl.pallas_call(
        _ffn_kernel,
        grid=(m // tm, f // tf),
        in_specs=[
            pl.BlockSpec((tm, d), lambda i, j: (i, 0)),
            pl.BlockSpec((1, d), lambda i, j: (0, 0)),
            pl.BlockSpec((d, tf), lambda i, j: (0, j)),
            pl.BlockSpec((d, tf), lambda i, j: (0, j)),
            pl.BlockSpec((tf, d), lambda i, j: (j, 0)),
        ],
        out_specs=pl.BlockSpec((tm, d), lambda i, j: (i, 0)),
        out_shape=jax.ShapeDtypeStruct((m, d), F32),
        scratch_shapes=[pltpu.VMEM((tm, d), BF16)],
        compiler_params=_params("arbitrary", "arbitrary"),
        name="swiglu_ffn",
    )(h, g, w_gate, w_up, w_down)


def _ple_kernel(*refs, n_prompt_tiles, final):
    i = pl.program_id(0)
    h_ref, pp_ref, ps_ref, g_ref, wg_ref, wp_ref = refs[:6]
    h = h_ref[...]
    gate = jax.nn.sigmoid(_dot(_rmsnorm(h, g_ref[...]).astype(BF16), wg_ref[...]))
    p = _pick_rows(i, n_prompt_tiles, pp_ref, ps_ref).astype(BF16)
    out = h + _dot(p, wp_ref[...]) * gate
    if not final:
        refs[6][...] = out
        return
    gf_ref, yp_ref, ys_ref = refs[6:]
    y = _rmsnorm(out, gf_ref[...])

    @pl.when(i < n_prompt_tiles)
    def _():
        yp_ref[...] = y

    @pl.when(i >= n_prompt_tiles)
    def _():
        ys_ref[...] = y


def _ple(h, pp, ps, g, w_gate, w_proj, g_final=None):
    m, d = h.shape
    mp, pdim = pp.shape
    tm = PLE_ROW_TILE
    assert m % tm == 0 and mp % tm == 0
    npt = mp // tm
    final = g_final is not None
    const = lambda i: (0, 0)
    in_specs = [pl.BlockSpec((tm, d), lambda i: (i, 0))] + _pair_specs((tm, pdim), npt) + [
        pl.BlockSpec((1, d), const), pl.BlockSpec((d, d), const), pl.BlockSpec((pdim, d), const)]
    args = [h, pp, ps, g, w_gate, w_proj]
    if final:
        in_specs.append(pl.BlockSpec((1, d), const))
        args.append(g_final)
        out_specs = _pair_specs((tm, d), npt)
        out_shape = [jax.ShapeDtypeStruct((mp, d), F32), jax.ShapeDtypeStruct((m - mp, d), F32)]
    else:
        out_specs = pl.BlockSpec((tm, d), lambda i: (i, 0))
        out_shape = jax.ShapeDtypeStruct((m, d), F32)
    return pl.pallas_call(
        functools.partial(_ple_kernel, n_prompt_tiles=npt, final=final),
        grid=(m // tm,),
        in_specs=in_specs,
        out_specs=out_specs,
        out_shape=out_shape,
        compiler_params=_params("arbitrary"),
        name="ple_final" if final else "ple",
    )(*args)


def _ssd_in_kernel(h_ref, g_ref, w_ref, wdt_ref, wdtT_ref, dtb_ref, dtbT_ref, taps_ref, cb_ref,
                   prev_ref, zs_ref, xbc_ref, tail_ref, xsamp_ref, dt_ref, dtT_ref,
                   hn_sc, carry_sc, *, n_prompt_tiles, tiles_per_seq, sample_len, n_z_tiles):
    i = pl.program_id(0)
    j = pl.program_id(1)

    @pl.when(j == 0)
    def _():
        hn = _rmsnorm(h_ref[...], g_ref[...]).astype(BF16)
        hn_sc[...] = hn
        dt_ref[...] = _softplus(_dot(hn, wdt_ref[...]) + dtb_ref[...])
        dtT_ref[...] = _softplus(_dot_nt(wdtT_ref[...], hn) + dtbT_ref[...])

    acc = _dot(hn_sc[...], w_ref[...])

    @pl.when(j < n_z_tiles)
    def _():
        zs_ref[...] = (acc * jax.nn.sigmoid(acc)).astype(BF16)

    jc = j - n_z_tiles

    def finish(conv):
        pre = conv + cb_ref[...]
        xbc_ref[...] = (pre * jax.nn.sigmoid(pre)).astype(BF16)

    tail = acc[acc.shape[0] - V7X_SUBLANES:, :]

    @pl.when(jnp.logical_and(j >= n_z_tiles, i < n_prompt_tiles))
    def _():
        carry = jnp.where(i % tiles_per_seq == 0, 0.0, carry_sc[jc])
        finish(_conv_prompt(acc, taps_ref[...], carry))
        carry_sc[jc] = tail
        tail_ref[...] = tail

    @pl.when(jnp.logical_and(j >= n_z_tiles, i >= n_prompt_tiles))
    def _():
        finish(_conv_sample(acc, taps_ref[...], prev_ref[...], sample_len))
        tail_ref[...] = tail
        xsamp_ref[...] = acc


def _ssd_in(h, g, w_zx, w_dt, w_dtT, dt_b, dt_bT, taps, conv_b, prev,
            *, n_prompt_rows, seq_len_p, seq_len_s, d_inner):
    m, d = h.shape
    conv_dim = taps.shape[1]
    hp = w_dt.shape[1]
    tm, tn = ROW_TILE, COL_TILE
    ms = m - n_prompt_rows
    assert n_prompt_rows % tm == 0 and ms == tm and seq_len_p % tm == 0 and tm % seq_len_s == 0
    assert d_inner % tn == 0 and conv_dim % tn == 0
    assert taps.shape[0] - 1 <= min(seq_len_s, V7X_SUBLANES)
    npt = n_prompt_rows // tm
    n_tiles = npt + 1
    nz, nc = d_inner // tn, conv_dim // tn
    cc = lambda j: jnp.maximum(j - nz, 0)
    samp_col = lambda i, j: jnp.where(i >= npt, cc(j), 0)
    const = lambda i, j: (0, 0)
    kern = functools.partial(_ssd_in_kernel, n_prompt_tiles=npt, tiles_per_seq=seq_len_p // tm,
                             sample_len=seq_len_s, n_z_tiles=nz)
    return pl.pallas_call(
        kern,
        grid=(n_tiles, nz + nc),
        in_specs=[
            pl.BlockSpec((tm, d), lambda i, j: (i, 0)),
            pl.BlockSpec((1, d), const),
            pl.BlockSpec((d, tn), lambda i, j: (0, j)),
            pl.BlockSpec((d, hp), const),
            pl.BlockSpec((hp, d), const),
            pl.BlockSpec((1, hp), const),
            pl.BlockSpec((hp, 1), const),
            pl.BlockSpec((taps.shape[0], tn), lambda i, j: (0, cc(j))),
            pl.BlockSpec((1, tn), lambda i, j: (0, cc(j))),
            pl.BlockSpec((tm, tn), lambda i, j: (0, samp_col(i, j))),
        ],
        out_specs=[
            pl.BlockSpec((tm, tn), lambda i, j: (i, jnp.minimum(j, nz - 1))),
            pl.BlockSpec((tm, tn), lambda i, j: (i, cc(j))),
            pl.BlockSpec((V7X_SUBLANES, tn), lambda i, j: (i, cc(j))),
            pl.BlockSpec((tm, tn), lambda i, j: (0, samp_col(i, j))),
            pl.BlockSpec((tm, hp), lambda i, j: (i, 0)),
            pl.BlockSpec((hp, tm), lambda i, j: (0, i)),
        ],
        out_shape=[
            jax.ShapeDtypeStruct((m, d_inner), BF16),
            jax.ShapeDtypeStruct((m, conv_dim), BF16),
            jax.ShapeDtypeStruct((n_tiles * V7X_SUBLANES, conv_dim), F32),
            jax.ShapeDtypeStruct((ms, conv_dim), F32),
            jax.ShapeDtypeStruct((m, hp), F32),
            jax.ShapeDtypeStruct((hp, m), F32),
        ],
        scratch_shapes=[pltpu.VMEM((tm, d), BF16), pltpu.VMEM((nc, V7X_SUBLANES, tn), F32)],
        compiler_params=_params("arbitrary", "arbitrary"),
        name="ssd_in",
    )(h, g, w_zx, w_dt, w_dtT, dt_b, dt_bT, taps, conv_b, prev)


def _ssd_group_out(x, zs, cb, acum, acumT, mask, xdt, inter, ng, head0, heads_per_group, head_dim):
    lanes = lax.broadcasted_iota(jnp.int32, (x.shape[0], V7X_LANES), 1)
    heads_per_slab = V7X_LANES // head_dim
    parts = []
    for q in range(heads_per_group // heads_per_slab):
        slab = xdt[:, q * V7X_LANES:(q + 1) * V7X_LANES]
        acc = None
        for r in range(heads_per_slab):
            hd = head0 + q * heads_per_slab + r
            seg = acum[:, hd:hd + 1] - acumT[hd:hd + 1, :]
            m = (cb * jnp.exp(jnp.where(mask, seg, MASKED))).astype(BF16)
            in_head = jnp.logical_and(lanes >= r * head_dim, lanes < (r + 1) * head_dim)
            part = _dot(m, jnp.where(in_head, slab, 0.0).astype(BF16))
            acc = part if acc is None else acc + part
        parts.append(acc)
    y = jnp.concatenate(parts, axis=1) + inter
    gated = y * zs
    ms = jnp.mean(gated * gated, axis=-1, keepdims=True)
    return (gated * lax.rsqrt(ms + EPS) * ng).astype(BF16)


def _ssd_prompt_kernel(xs_ref, b_ref, c_ref, zs_ref, dt_ref, dtT_ref, alr_ref, alc_ref, e_ref,
                       d_ref, ng_ref, y_ref, state_ref, st_sc, *, n_groups, head_dim, d_state):
    c = pl.program_id(1)

    @pl.when(c == 0)
    def _():
        st_sc[...] = jnp.zeros_like(st_sc)

    q_rows, d_inner = xs_ref.shape
    gw = d_inner // n_groups
    hpg = gw // head_dim
    row = lax.broadcasted_iota(jnp.int32, (q_rows, q_rows), 0)
    col = lax.broadcasted_iota(jnp.int32, (q_rows, q_rows), 1)
    causal = col <= row
    tril = jnp.where(causal, 1.0, 0.0).astype(BF16)
    triu = jnp.where(row <= col, 1.0, 0.0).astype(BF16)
    dt = dt_ref[...]
    acum = _dot01_lhs(tril, dt * -jnp.exp(alr_ref[...]))
    acumT = _dot01_rhs(dtT_ref[...] * -jnp.exp(alc_ref[...]), triu)
    e = e_ref[...]
    dt_x = _dot01_rhs(dt, e)
    ac_x = _dot01_rhs(acum, e)
    a_end = ac_x[q_rows - 1:q_rows, :]
    x = xs_ref[...].astype(F32)
    xdt = x * dt_x
    to_end = (xdt * jnp.exp(a_end - ac_x)).astype(BF16)
    from_start = jnp.exp(ac_x)
    skip = x * d_ref[...]
    for g in range(n_groups):
        sl = slice(g * gw, (g + 1) * gw)
        ns = slice(g * d_state, (g + 1) * d_state)
        bg, cg = b_ref[:, ns], c_ref[:, ns]
        st = st_sc[:, sl]
        inter = from_start[:, sl] * _dot(cg, st.astype(BF16)) + skip[:, sl]
        y_ref[:, sl] = _ssd_group_out(x[:, sl], zs_ref[:, sl].astype(F32), _dot_nt(cg, bg), acum,
                                      acumT, causal, xdt[:, sl], inter, ng_ref[:, sl],
                                      g * hpg, hpg, head_dim)
        st_sc[:, sl] = jnp.exp(a_end[:, sl]) * st + _dot_tn(bg, to_end[:, sl])

    @pl.when(c == pl.num_programs(1) - 1)
    def _():
        for g in range(n_groups):
            state_ref[g * gw:(g + 1) * gw, :] = st_sc[:, g * gw:(g + 1) * gw].T


def _ssd_prompt(xbc, zs, dt, dtT, alog_row, alog_col, expand, d_x, ng,
                *, n_seq, seq_len, d_inner, n_groups, head_dim, d_state):
    q = SSD_CHUNK
    assert seq_len % q == 0
    nc = seq_len // q
    hp = dt.shape[1]
    gn = n_groups * d_state
    assert d_inner % gn == 0
    rows = lambda b, c: b * nc + c
    const = lambda b, c: (0, 0)
    kern = functools.partial(_ssd_prompt_kernel, n_groups=n_groups, head_dim=head_dim, d_state=d_state)
    return pl.pallas_call(
        kern,
        grid=(n_seq, nc),
        in_specs=[
            pl.BlockSpec((q, d_inner), lambda b, c: (rows(b, c), 0)),
            pl.BlockSpec((q, gn), lambda b, c: (rows(b, c), d_inner // gn)),
            pl.BlockSpec((q, gn), lambda b, c: (rows(b, c), d_inner // gn + 1)),
            pl.BlockSpec((q, d_inner), lambda b, c: (rows(b, c), 0)),
            pl.BlockSpec((q, hp), lambda b, c: (rows(b, c), 0)),
            pl.BlockSpec((hp, q), lambda b, c: (0, rows(b, c))),
            pl.BlockSpec((1, hp), const),
            pl.BlockSpec((hp, 1), const),
            pl.BlockSpec((hp, d_inner), const),
            pl.BlockSpec((1, d_inner), const),
            pl.BlockSpec((1, d_inner), const),
        ],
        out_specs=[
            pl.BlockSpec((q, d_inner), lambda b, c: (rows(b, c), 0)),
            pl.BlockSpec((d_inner, d_state), lambda b, c: (b, 0)),
        ],
        out_shape=[
            jax.ShapeDtypeStruct((n_seq * seq_len, d_inner), BF16),
            jax.ShapeDtypeStruct((n_seq * d_inner, d_state), F32),
        ],
        scratch_shapes=[pltpu.VMEM((d_state, d_inner), F32)],
        compiler_params=_params("arbitrary", "arbitrary"),
        name="ssd_scan_prompt",
    )(xbc, xbc, xbc, zs, dt, dtT, alog_row, alog_col, expand, d_x, ng)


def _ssd_sample_kernel(xs_ref, b_ref, c_ref, zs_ref, dt_ref, dtT_ref, alr_ref, alc_ref, e_ref,
                       d_ref, ng_ref, st_ref, y_ref, nst_ref, *, seq_len, head_dim):
    q_rows, gw = xs_ref.shape
    n_seq = q_rows // seq_len
    row = lax.broadcasted_iota(jnp.int32, (q_rows, q_rows), 0)
    col = lax.broadcasted_iota(jnp.int32, (q_rows, q_rows), 1)
    same = (row // seq_len) == (col // seq_len)
    mask = jnp.logical_and(same, col <= row)
    tril = jnp.where(mask, 1.0, 0.0).astype(BF16)
    triu = jnp.where(jnp.logical_and(same, row <= col), 1.0, 0.0).astype(BF16)
    ends = jnp.where(col == (row // seq_len) * seq_len + (seq_len - 1), 1.0, 0.0).astype(BF16)
    dt = dt_ref[...]
    acum = _dot01_lhs(tril, dt * -jnp.exp(alr_ref[...]))
    acumT = _dot01_rhs(dtT_ref[...] * -jnp.exp(alc_ref[...]), triu)
    e = e_ref[...]
    dt_x = _dot01_rhs(dt, e)
    ac_x = _dot01_rhs(acum, e)
    a_end = _dot01_lhs(ends, ac_x)
    x = xs_ref[...].astype(F32)
    xdt = x * dt_x
    to_endT = (xdt * jnp.exp(a_end - ac_x)).T.astype(BF16)
    decayT = jnp.exp(a_end).T
    bg = b_ref[...].astype(F32)
    cg = c_ref[...].astype(F32)
    seq_of_row = lax.broadcasted_iota(jnp.int32, bg.shape, 0) // seq_len
    inter = jnp.zeros((q_rows, gw), F32)
    for s in range(n_seq):
        st = st_ref[s]
        mine = seq_of_row == s
        inter = inter + _dot_nt(jnp.where(mine, cg, 0.0).astype(BF16), st.astype(BF16))
        bm = jnp.where(mine, bg, 0.0).astype(BF16)
        nst_ref[s] = decayT[:, s * seq_len:s * seq_len + 1] * st + _dot(to_endT, bm)
    inter = jnp.exp(ac_x) * inter + x * d_ref[...]
    y_ref[...] = _ssd_group_out(x, zs_ref[...].astype(F32), _dot_nt(c_ref[...], b_ref[...]), acum,
                                acumT, mask, xdt, inter, ng_ref[...], 0, gw // head_dim, head_dim)


def _ssd_sample(xbc, zs, dt_g, dtT_g, alog_row_g, alog_col_g, expand, d_x, ng, state,
                *, row0, n_rows, seq_len, d_inner, n_groups, head_dim, d_state):
    q = SSD_CHUNK
    assert n_rows % q == 0 and q % seq_len == 0 and row0 % q == 0
    nb = n_rows // q
    spb = q // seq_len
    gw = d_inner // n_groups
    hp = dt_g.shape[2]
    rb0 = row0 // q
    b_col0 = d_inner // d_state
    kern = functools.partial(_ssd_sample_kernel, seq_len=seq_len, head_dim=head_dim)
    return pl.pallas_call(
        kern,
        grid=(nb, n_groups),
        in_specs=[
            pl.BlockSpec((q, gw), lambda s, g: (rb0 + s, g)),
            pl.BlockSpec((q, d_state), lambda s, g: (rb0 + s, b_col0 + g)),
            pl.BlockSpec((q, d_state), lambda s, g: (rb0 + s, b_col0 + n_groups + g)),
            pl.BlockSpec((q, gw), lambda s, g: (rb0 + s, g)),
            pl.BlockSpec((None, q, hp), lambda s, g: (g, s, 0)),
            pl.BlockSpec((None, dtT_g.shape[1], q), lambda s, g: (g, 0, s)),
            pl.BlockSpec((None, 1, hp), lambda s, g: (g, 0, 0)),
            pl.BlockSpec((None, dtT_g.shape[1], 1), lambda s, g: (g, 0, 0)),
            pl.BlockSpec((hp, gw), lambda s, g: (0, 0)),
            pl.BlockSpec((1, gw), lambda s, g: (0, g)),
            pl.BlockSpec((1, gw), lambda s, g: (0, g)),
            pl.BlockSpec((spb, None, gw, d_state), lambda s, g: (s, g, 0, 0)),
        ],
        out_specs=[
            pl.BlockSpec((q, gw), lambda s, g: (s, g)),
            pl.BlockSpec((spb, None, gw, d_state), lambda s, g: (s, g, 0, 0)),
        ],
        out_shape=[
            jax.ShapeDtypeStruct((n_rows, d_inner), BF16),
            jax.ShapeDtypeStruct(state.shape, F32),
        ],
        compiler_params=_params("arbitrary", "arbitrary"),
        name="ssd_scan_sample",
    )(xbc, xbc, xbc, zs, dt_g, dtT_g, alog_row_g, alog_col_g, expand, d_x, ng, state)


def _conv_states(tail, samp, *, n_prompt_tiles, tiles_per_seq, n_seq_s, seq_len_s, km1):
    c = tail.shape[1]
    t = tail.reshape(-1, V7X_SUBLANES, c)[:n_prompt_tiles]
    prompt = t[tiles_per_seq - 1::tiles_per_seq, V7X_SUBLANES - km1:, :]
    sample = samp.reshape(n_seq_s, seq_len_s, c)[:, seq_len_s - km1:, :]
    return prompt, sample


def kernel(x_prompt, x_sample, p_prompt, p_sample, state_sc_conv, state_ssd_conv, state_ssd, g_mix, g_ffn, g_ple, g_final, sc_w_in, sc_w_conv, sc_w_out, ssd_w_in, ssd_conv_w, ssd_conv_b, ssd_dt_bias, ssd_a_log, ssd_d, ssd_norm_g, ssd_w_out, ffn_w_gate, ffn_w_up, ffn_w_down, ple_w_proj, ple_w_gate):
    bp, lp, d = x_prompt.shape
    bs, ls, _ = x_sample.shape
    depth = g_mix.shape[0]
    mp, ms = bp * lp, bs * ls
    pdim = p_prompt.shape[-1]
    n_heads, head_dim, d_state = state_ssd.shape[2:]
    d_inner = n_heads * head_dim
    conv_dim = ssd_conv_w.shape[-1]
    n_groups = (conv_dim - d_inner) // (2 * d_state)
    hpg = n_heads // n_groups
    assert n_heads <= V7X_LANES and V7X_LANES % head_dim == 0 and d_state == V7X_LANES
    npt = mp // ROW_TILE
    tps = lp // ROW_TILE
    row = lambda v: v.reshape(1, -1)

    h = jnp.concatenate([x_prompt.reshape(mp, d), x_sample.reshape(ms, d)])
    sc_p, sc_s, cv_p, cv_s, st_p, st_s = [], [], [], [], [], []
    y_out = None
    for i in range(depth):
        j = i // 2
        if i % 2 == 0:
            km1 = sc_w_conv.shape[1] - 1
            gated, tail, usamp = _short_conv_in(
                h, row(g_mix[i]), sc_w_in[j].astype(BF16), sc_w_conv[j],
                _shifted_state(state_sc_conv[j], ls), n_prompt_rows=mp, seq_len_p=lp, seq_len_s=ls)
            p_state, s_state = _conv_states(tail, usamp, n_prompt_tiles=npt, tiles_per_seq=tps,
                                            n_seq_s=bs, seq_len_s=ls, km1=km1)
            sc_p.append(p_state)
            sc_s.append(s_state)
            h = _matmul_residual(h, gated, sc_w_out[j].astype(BF16), n_prompt_rows=mp)
        else:
            km1 = ssd_conv_w.shape[1] - 1
            w_in = ssd_w_in[j]
            zx = d_inner + conv_dim
            pad_h = V7X_LANES - n_heads
            w_dt = jnp.pad(w_in[:, zx:], ((0, 0), (0, pad_h))).astype(BF16)
            dt_b = jnp.pad(ssd_dt_bias[j], (0, pad_h))
            alog = jnp.pad(ssd_a_log[j], (0, pad_h))
            zs, xbc, tail, xsamp, dt, dtT = _ssd_in(
                h, row(g_mix[i]), w_in[:, :zx].astype(BF16), w_dt, w_dt.T, row(dt_b),
                dt_b.reshape(-1, 1), ssd_conv_w[j], row(ssd_conv_b[j]),
                _shifted_state(state_ssd_conv[j], ls),
                n_prompt_rows=mp, seq_len_p=lp, seq_len_s=ls, d_inner=d_inner)
            p_state, s_state = _conv_states(tail, xsamp, n_prompt_tiles=npt, tiles_per_seq=tps,
                                            n_seq_s=bs, seq_len_s=ls, km1=km1)
            cv_p.append(p_state)
            cv_s.append(s_state)
            head_of_lane = jnp.arange(d_inner, dtype=jnp.int32) // head_dim
            expand = (jnp.arange(V7X_LANES, dtype=jnp.int32)[:, None] == head_of_lane[None, :]).astype(BF16)
            d_x = row(jnp.repeat(ssd_d[j], head_dim))
            ng = row(ssd_norm_g[j])
            geom = dict(d_inner=d_inner, n_groups=n_groups, head_dim=head_dim, d_state=d_state)
            y_p, new_p = _ssd_prompt(xbc, zs, dt, dtT, row(alog), alog.reshape(-1, 1), expand, d_x, ng,
                                     n_seq=bp, seq_len=lp, **geom)
            dt_s = dt[mp:]
            dt_g = jnp.stack([jnp.roll(dt_s, -g * hpg, axis=1) for g in range(n_groups)])
            alog_g = jnp.stack([jnp.roll(alog, -g * hpg) for g in range(n_groups)])
            dtT_g = dtT[:n_heads, mp:].reshape(n_groups, hpg, ms)
            y_s, new_s = _ssd_sample(
                xbc, zs, dt_g, dtT_g, alog_g.reshape(n_groups, 1, -1),
                ssd_a_log[j].reshape(n_groups, hpg, 1), expand[:, :d_inner // n_groups], d_x, ng,
                state_ssd[j].reshape(bs, n_groups, hpg * head_dim, d_state),
                row0=mp, n_rows=ms, seq_len=ls, **geom)
            st_p.append(new_p.reshape(bp, n_heads, head_dim, d_state))
            st_s.append(new_s.reshape(bs, n_heads, head_dim, d_state))
            h = _matmul_residual(h, (y_p, y_s), ssd_w_out[j].astype(BF16), n_prompt_rows=mp)
        h = _ffn(h, row(g_ffn[i]), ffn_w_gate[i].astype(BF16), ffn_w_up[i].astype(BF16),
                 ffn_w_down[i].astype(BF16))
        ple_args = (h, p_prompt[i].reshape(mp, pdim), p_sample[i].reshape(ms, pdim), row(g_ple[i]),
                    ple_w_gate[i].astype(BF16), ple_w_proj[i].astype(BF16))
        if i == depth - 1:
            y_out = _ple(*ple_args, g_final=row(g_final))
        else:
            h = _ple(*ple_args)
    y_p, y_s = y_out
    return (y_p.reshape(bp, lp, d), y_s.reshape(bs, ls, d), jnp.stack(sc_p), jnp.stack(sc_s),
            jnp.stack(cv_p), jnp.stack(cv_s), jnp.stack(st_p), jnp.stack(st_s))
```

```python
import functools

import jax
import jax.numpy as jnp
from jax import lax
from jax.experimental import pallas as pl
from jax.experimental.pallas import tpu as pltpu

F32 = jnp.float32
BF16 = jnp.bfloat16
EPS = 1e-6
MASKED = -1e30
V7X_LANES = 128
V7X_SUBLANES = 8
V7X_BF16_ROWS = 16
V7X_VMEM_LIMIT = 56 * 1024 * 1024

ROW_TILE = 1024
HALF_ROW_TILE = 512
PLE_ROW_TILE = 256
COL_TILE = 512
NARROW_COL_TILE = 256
SSD_CHUNK = 128


def _params(*sem):
    return pltpu.CompilerParams(dimension_semantics=sem, vmem_limit_bytes=V7X_VMEM_LIMIT)


def _dot(a, b):
    return jnp.dot(a, b, preferred_element_type=F32)


def _dot_nt(a, b):
    return lax.dot_general(a, b, (((1,), (1,)), ((), ())), preferred_element_type=F32)


def _dot_tn(a, b):
    return lax.dot_general(a, b, (((0,), (0,)), ((), ())), preferred_element_type=F32)


def _split3(a):
    a1 = a.astype(BF16)
    r1 = a - a1.astype(F32)
    a2 = r1.astype(BF16)
    a3 = (r1 - a2.astype(F32)).astype(BF16)
    return a3, a2, a1


def _dot01_rhs(a, e):
    p3, p2, p1 = _split3(a)
    return (_dot(p3, e) + _dot(p2, e)) + _dot(p1, e)


def _dot01_lhs(t, a):
    p3, p2, p1 = _split3(a)
    return (_dot(t, p3) + _dot(t, p2)) + _dot(t, p1)


def _rmsnorm(x, g):
    ms = jnp.mean(x * x, axis=-1, keepdims=True)
    return x * lax.rsqrt(ms + EPS) * g


def _softplus(x):
    return jnp.maximum(x, 0.0) + jnp.log1p(jnp.exp(-jnp.abs(x)))


def _silu(x):
    return x * jax.nn.sigmoid(x)


def _layer_spec(block, layer, imap):
    return pl.BlockSpec((None,) + tuple(block), lambda *a: (layer,) + tuple(imap(*a)))


def _conv_rolled(u, taps, seq_len=None):
    k = taps.shape[0]
    out = taps[k - 1:k, :] * u
    if seq_len is not None:
        t = lax.broadcasted_iota(jnp.int32, u.shape, 0) % seq_len
    for d in range(1, k):
        sh = pltpu.roll(u, d, 0)
        if seq_len is not None:
            sh = jnp.where(t >= d, sh, 0.0)
        out = out + taps[k - 1 - d:k - d, :] * sh
    return out


def _conv_head(u_head, taps, carry):
    n = carry.shape[0]
    return _conv_rolled(jnp.concatenate([carry, u_head], axis=0), taps)[n:, :]


def _state_correction(buf_refs, taps, stage_sc, seq_len):
    k = taps.shape[0]
    km1 = k - 1
    n_seq = stage_sc.shape[1] // seq_len
    stage_sc[...] = jnp.zeros_like(stage_sc)
    rows = [r[...] for r in buf_refs]
    for t in range(km1):
        acc = None
        for d in range(t + 1, k):
            term = taps[k - 1 - d:k - d, :] * rows[km1 + t - d]
            acc = term if acc is None else acc + term
        for c in range(stage_sc.shape[0]):
            stage_sc[c, pl.ds(t, n_seq, stride=seq_len), :] = acc[:, c * V7X_LANES:(c + 1) * V7X_LANES]


def _staged(stage_sc):
    return jnp.concatenate([stage_sc[c] for c in range(stage_sc.shape[0])], axis=1)


def _emit_sample_state(u, stage_sc, nstate_ref, seq_len):
    km1 = nstate_ref.shape[0]
    n_chunks = stage_sc.shape[0]
    n_seq = stage_sc.shape[1] // seq_len
    for c in range(n_chunks):
        stage_sc[c] = u[:, c * V7X_LANES:(c + 1) * V7X_LANES]
    for r in range(km1):
        nstate_ref[r] = jnp.concatenate(
            [stage_sc[c, pl.ds(seq_len - km1 + r, n_seq, stride=seq_len), :] for c in range(n_chunks)],
            axis=1)


def _sc_in_kernel(h_ref, g_ref, wb_ref, wc_ref, wv_ref, taps_ref, *rest,
                  n_prompt_tiles, tiles_per_seq, sample_len):
    *buf_refs, gated_ref, tail_ref, nstate_ref, hn_sc, carry_sc, stage_sc = rest
    i = pl.program_id(0)
    j = pl.program_id(1)

    @pl.when(j == 0)
    def _():
        hn_sc[...] = _rmsnorm(h_ref[...], g_ref[...]).astype(BF16)

    def project():
        hn = hn_sc[...]
        bg = _dot(hn, wb_ref[...].astype(BF16))
        u = _dot(hn, wc_ref[...].astype(BF16)) * _dot(hn, wv_ref[...].astype(BF16))
        return bg, u

    nc = V7X_BF16_ROWS

    @pl.when(i < n_prompt_tiles)
    def _():
        bg, u = project()
        taps = taps_ref[...]
        carry = jnp.where(i % tiles_per_seq == 0, 0.0, carry_sc[j])
        gated_ref[...] = (bg * _conv_rolled(u, taps)).astype(BF16)
        gated_ref[:nc, :] = (bg[:nc, :] * _conv_head(u[:nc, :], taps, carry)).astype(BF16)
        carry_sc[j] = u[u.shape[0] - nc:, :]
        tail_ref[...] = u[u.shape[0] - V7X_SUBLANES:, :]

    @pl.when(i >= n_prompt_tiles)
    def _():
        taps = taps_ref[...]
        _state_correction(buf_refs, taps, stage_sc, sample_len)
        bg, u = project()
        gated_ref[...] = (bg * (_conv_rolled(u, taps, sample_len) + _staged(stage_sc))).astype(BF16)
        tail_ref[...] = u[u.shape[0] - V7X_SUBLANES:, :]
        _emit_sample_state(u, stage_sc, nstate_ref, sample_len)


def _state_row_specs(layer, n_seq, km1, width, tn, col_of):
    return [_layer_spec((n_seq, tn), layer,
                        lambda i, j, r=r: (0, jnp.where(col_of(i, j) >= 0, r * (width // tn) + col_of(i, j), 0)))
            for r in range(km1)]


def _short_conv_in(h, g, w_in, layer, taps, buf, *, n_prompt_rows, seq_len_p, seq_len_s):
    m, d = h.shape
    mp, ms = n_prompt_rows, m - n_prompt_rows
    k = taps.shape[1]
    tm, tn = ROW_TILE, NARROW_COL_TILE
    assert mp % tm == 0 and ms == tm and seq_len_p % tm == 0 and tm % seq_len_s == 0
    assert d % tn == 0 and tn % V7X_LANES == 0 and k - 1 <= min(seq_len_s, V7X_SUBLANES)
    npt, nj = mp // tm, d // tn
    n_tiles = npt + 1
    n_seq_s = ms // seq_len_s
    samp_col = lambda i, j: jnp.where(i >= npt, j, -1)
    kern = functools.partial(_sc_in_kernel, n_prompt_tiles=npt, tiles_per_seq=seq_len_p // tm,
                             sample_len=seq_len_s)
    return pl.pallas_call(
        kern,
        grid=(n_tiles, nj),
        in_specs=[
            pl.BlockSpec((tm, d), lambda i, j: (i, 0)),
            pl.BlockSpec((1, d), lambda i, j: (0, 0)),
            _layer_spec((d, tn), layer, lambda i, j: (0, j)),
            _layer_spec((d, tn), layer, lambda i, j: (0, nj + j)),
            _layer_spec((d, tn), layer, lambda i, j: (0, 2 * nj + j)),
            _layer_spec((k, tn), layer, lambda i, j: (0, j)),
        ] + _state_row_specs(layer, n_seq_s, k - 1, d, tn, samp_col),
        out_specs=[
            pl.BlockSpec((tm, tn), lambda i, j: (i, j)),
            pl.BlockSpec((V7X_SUBLANES, tn), lambda i, j: (i, j)),
            pl.BlockSpec((k - 1, n_seq_s, tn), lambda i, j: (0, 0, jnp.maximum(samp_col(i, j), 0))),
        ],
        out_shape=[
            jax.ShapeDtypeStruct((m, d), BF16),
            jax.ShapeDtypeStruct((n_tiles * V7X_SUBLANES, d), F32),
            jax.ShapeDtypeStruct((k - 1, n_seq_s, d), F32),
        ],
        scratch_shapes=[pltpu.VMEM((tm, d), BF16), pltpu.VMEM((nj, V7X_BF16_ROWS, tn), F32),
                        pltpu.VMEM((tn // V7X_LANES, tm, V7X_LANES), F32)],
        compiler_params=_params("arbitrary", "arbitrary"),
        name="short_conv_in",
    )(h, g, w_in, w_in, w_in, taps, *([buf] * (k - 1)))


def _res_kernel(*refs, n_prompt_tiles):
    res_ref, *a_refs, w_ref, out_ref, wb_sc = refs
    i = pl.program_id(1)

    @pl.when(i == 0)
    def _():
        wb_sc[...] = w_ref[...].astype(BF16)

    def body(a_ref):
        out_ref[...] = res_ref[...] + _dot(a_ref[...], wb_sc[...])

    if len(a_refs) == 1:
        body(a_refs[0])
        return
    pl.when(i < n_prompt_tiles)(lambda: body(a_refs[0]))
    pl.when(i >= n_prompt_tiles)(lambda: body(a_refs[1]))


def _matmul_residual(res, a, w, layer, *, n_prompt_rows):
    a_pair = isinstance(a, tuple)
    _, k, n = w.shape
    m = res.shape[0]
    tm, tn = HALF_ROW_TILE, COL_TILE
    assert n_prompt_rows % tm == 0 and m % tm == 0 and n % tn == 0
    npt = n_prompt_rows // tm
    if a_pair:
        a_specs = [pl.BlockSpec((tm, k), lambda j, i: (jnp.minimum(i, npt - 1), 0)),
                   pl.BlockSpec((tm, k), lambda j, i: (jnp.maximum(i - npt, 0), 0))]
    else:
        a_specs = [pl.BlockSpec((tm, k), lambda j, i: (i, 0))]
    return pl.pallas_call(
        functools.partial(_res_kernel, n_prompt_tiles=npt),
        grid=(n // tn, m // tm),
        in_specs=[pl.BlockSpec((tm, tn), lambda j, i: (i, j))] + a_specs
        + [_layer_spec((k, tn), layer, lambda j, i: (0, j))],
        out_specs=pl.BlockSpec((tm, tn), lambda j, i: (i, j)),
        out_shape=jax.ShapeDtypeStruct((m, n), F32),
        scratch_shapes=[pltpu.VMEM((k, tn), BF16)],
        compiler_params=_params("arbitrary", "arbitrary"),
        name="matmul_residual",
    )(res, *(a if a_pair else (a,)), w)


def _ffn_kernel(h_ref, g_ref, wg_ref, wu_ref, wd_ref, out_ref, hn_sc):
    @pl.when(pl.program_id(1) == 0)
    def _():
        h = h_ref[...]
        hn_sc[...] = _rmsnorm(h, g_ref[...]).astype(BF16)
        out_ref[...] = h

    hn = hn_sc[...]
    gate = _dot(hn, wg_ref[...].astype(BF16))
    act = (_silu(gate) * _dot(hn, wu_ref[...].astype(BF16))).astype(BF16)
    out_ref[...] += _dot(act, wd_ref[...].astype(BF16))


def _ffn(h, g, w_gate, w_up, w_down, layer):
    m, d = h.shape
    f = w_gate.shape[2]
    tm, tf = ROW_TILE, NARROW_COL_TILE
    assert m % tm == 0 and f % tf == 0
    return pl.pallas_call(
        _ffn_kernel,
        grid=(m // tm, f // tf),
        in_specs=[
            pl.BlockSpec((tm, d), lambda i, j: (i, 0)),
            pl.BlockSpec((1, d), lambda i, j: (0, 0)),
            _layer_spec((d, tf), layer, lambda i, j: (0, j)),
            _layer_spec((d, tf), layer, lambda i, j: (0, j)),
            _layer_spec((tf, d), layer, lambda i, j: (j, 0)),
        ],
        out_specs=pl.BlockSpec((tm, d), lambda i, j: (i, 0)),
        out_shape=jax.ShapeDtypeStruct((m, d), F32),
        scratch_shapes=[pltpu.VMEM((tm, d), BF16)],
        compiler_params=_params("arbitrary", "arbitrary"),
        name="swiglu_ffn",
    )(h, g, w_gate, w_up, w_down)


def _ple_kernel(*refs, n_prompt_tiles, final):
    i = pl.program_id(0)
    h_ref, pp_ref, ps_ref, g_ref, wg_ref, wp_ref = refs[:6]
    wgb_sc, wpb_sc = refs[-2:]

    @pl.when(i == 0)
    def _():
        wgb_sc[...] = wg_ref[...].astype(BF16)
        wpb_sc[...] = wp_ref[...].astype(BF16)

    h = h_ref[...]
    gate = jax.nn.sigmoid(_dot(_rmsnorm(h, g_ref[...]).astype(BF16), wgb_sc[...]))
    p = jnp.where(i < n_prompt_tiles, pp_ref[...], ps_ref[...]).astype(BF16)
    out = h + _dot(p, wpb_sc[...]) * gate
    if not final:
        refs[6][...] = out
        return
    gf_ref, yp_ref, ys_ref = refs[6:9]
    y = _rmsnorm(out, gf_ref[...])

    @pl.when(i < n_prompt_tiles)
    def _():
        yp_ref[...] = y

    @pl.when(i >= n_prompt_tiles)
    def _():
        ys_ref[...] = y


def _ple(h, pp, ps, g, w_gate, w_proj, layer, g_final=None):
    m, d = h.shape
    _, mp, pdim = pp.shape
    tm = PLE_ROW_TILE
    assert m % tm == 0 and mp % tm == 0
    npt = mp // tm
    final = g_final is not None
    const = lambda i: (0, 0)
    resident = dict(pipeline_mode=pl.Buffered(1))
    in_specs = [
        pl.BlockSpec((tm, d), lambda i: (i, 0)),
        _layer_spec((tm, pdim), layer, lambda i: (jnp.minimum(i, npt - 1), 0)),
        _layer_spec((tm, pdim), layer, lambda i: (jnp.maximum(i - npt, 0), 0)),
        pl.BlockSpec((1, d), const),
        pl.BlockSpec((None, d, d), lambda i: (layer, 0, 0), **resident),
        pl.BlockSpec((None, pdim, d), lambda i: (layer, 0, 0), **resident),
    ]
    args = [h, pp, ps, g, w_gate, w_proj]
    if final:
        in_specs.append(pl.BlockSpec((1, d), const))
        args.append(g_final)
        out_specs = [pl.BlockSpec((tm, d), lambda i: (jnp.minimum(i, npt - 1), 0)),
                     pl.BlockSpec((tm, d), lambda i: (jnp.maximum(i - npt, 0), 0))]
        out_shape = [jax.ShapeDtypeStruct((mp, d), F32), jax.ShapeDtypeStruct((m - mp, d), F32)]
    else:
        out_specs = pl.BlockSpec((tm, d), lambda i: (i, 0))
        out_shape = jax.ShapeDtypeStruct((m, d), F32)
    return pl.pallas_call(
        functools.partial(_ple_kernel, n_prompt_tiles=npt, final=final),
        grid=(m // tm,),
        in_specs=in_specs,
        out_specs=out_specs,
        out_shape=out_shape,
        scratch_shapes=[pltpu.VMEM((d, d), BF16), pltpu.VMEM((pdim, d), BF16)],
        compiler_params=_params("arbitrary"),
        name="ple_final" if final else "ple",
    )(*args)


def _ssd_in_kernel(h_ref, g_ref, w_ref, wdt_ref, wdtT_ref, dtb_ref, dtbT_ref, taps_ref, cb_ref,
                   *rest, n_prompt_tiles, tiles_per_seq, sample_len, n_z_tiles):
    (*buf_refs, zs_ref, xbc_ref, tail_ref, nstate_ref, dt_ref, dtT_ref,
     hn_sc, carry_sc, stage_sc) = rest
    i = pl.program_id(0)
    j = pl.program_id(1)

    @pl.when(j == 0)
    def _():
        hn = _rmsnorm(h_ref[...], g_ref[...]).astype(BF16)
        hn_sc[...] = hn
        dt_ref[...] = _softplus(_dot(hn, wdt_ref[...]) + dtb_ref[...])
        dtT_ref[...] = _softplus(_dot_nt(wdtT_ref[...], hn) + dtbT_ref[...])

    def project():
        return _dot(hn_sc[...], w_ref[...].astype(BF16))

    @pl.when(j < n_z_tiles)
    def _():
        zs_ref[...] = _silu(project()).astype(BF16)

    jc = j - n_z_tiles
    nc = V7X_BF16_ROWS

    def finish(conv):
        return _silu(conv + cb_ref[...]).astype(BF16)

    @pl.when(jnp.logical_and(j >= n_z_tiles, i < n_prompt_tiles))
    def _():
        u = project()
        taps = taps_ref[...]
        carry = jnp.where(i % tiles_per_seq == 0, 0.0, carry_sc[jc])
        xbc_ref[...] = finish(_conv_rolled(u, taps))
        xbc_ref[:nc, :] = finish(_conv_head(u[:nc, :], taps, carry))
        carry_sc[jc] = u[u.shape[0] - nc:, :]
        tail_ref[...] = u[u.shape[0] - V7X_SUBLANES:, :]

    @pl.when(jnp.logical_and(j >= n_z_tiles, i >= n_prompt_tiles))
    def _():
        taps = taps_ref[...]
        _state_correction(buf_refs, taps, stage_sc, sample_len)
        u = project()
        xbc_ref[...] = finish(_conv_rolled(u, taps, sample_len) + _staged(stage_sc))
        tail_ref[...] = u[u.shape[0] - V7X_SUBLANES:, :]
        _emit_sample_state(u, stage_sc, nstate_ref, sample_len)


def _ssd_in(h, g, w_in, layer, w_dt, w_dtT, dt_b, dt_bT, taps, conv_b, buf,
            *, n_prompt_rows, seq_len_p, seq_len_s, d_inner):
    m, d = h.shape
    _, k, conv_dim = taps.shape
    hp = w_dt.shape[1]
    tm, tn = ROW_TILE, COL_TILE
    ms = m - n_prompt_rows
    assert n_prompt_rows % tm == 0 and ms == tm and seq_len_p % tm == 0 and tm % seq_len_s == 0
    assert d_inner % tn == 0 and conv_dim % tn == 0 and k - 1 <= min(seq_len_s, V7X_SUBLANES)
    npt = n_prompt_rows // tm
    n_tiles = npt + 1
    n_seq_s = ms // seq_len_s
    nz, nc = d_inner // tn, conv_dim // tn
    cc = lambda j: jnp.maximum(j - nz, 0)
    samp_col = lambda i, j: jnp.where(jnp.logical_and(i >= npt, j >= nz), j - nz, -1)
    const = lambda i, j: (0, 0)
    kern = functools.partial(_ssd_in_kernel, n_prompt_tiles=npt, tiles_per_seq=seq_len_p // tm,
                             sample_len=seq_len_s, n_z_tiles=nz)
    return pl.pallas_call(
        kern,
        grid=(n_tiles, nz + nc),
        in_specs=[
            pl.BlockSpec((tm, d), lambda i, j: (i, 0)),
            pl.BlockSpec((1, d), const),
            _layer_spec((d, tn), layer, lambda i, j: (0, j)),
            pl.BlockSpec((d, hp), const),
            pl.BlockSpec((hp, d), const),
            pl.BlockSpec((1, hp), const),
            pl.BlockSpec((hp, 1), const),
            _layer_spec((k, tn), layer, lambda i, j: (0, cc(j))),
            _layer_spec((1, tn), layer, lambda i, j: (0, cc(j))),
        ] + _state_row_specs(layer, n_seq_s, k - 1, conv_dim, tn, samp_col),
        out_specs=[
            pl.BlockSpec((tm, tn), lambda i, j: (i, jnp.minimum(j, nz - 1))),
            pl.BlockSpec((tm, tn), lambda i, j: (i, cc(j))),
            pl.BlockSpec((V7X_SUBLANES, tn), lambda i, j: (i, cc(j))),
            pl.BlockSpec((k - 1, n_seq_s, tn), lambda i, j: (0, 0, jnp.maximum(samp_col(i, j), 0))),
            pl.BlockSpec((tm, hp), lambda i, j: (i, 0)),
            pl.BlockSpec((hp, tm), lambda i, j: (0, i)),
        ],
        out_shape=[
            jax.ShapeDtypeStruct((m, d_inner), BF16),
            jax.ShapeDtypeStruct((m, conv_dim), BF16),
            jax.ShapeDtypeStruct((n_tiles * V7X_SUBLANES, conv_dim), F32),
            jax.ShapeDtypeStruct((k - 1, n_seq_s, conv_dim), F32),
            jax.ShapeDtypeStruct((m, hp), F32),
            jax.ShapeDtypeStruct((hp, m), F32),
        ],
        scratch_shapes=[pltpu.VMEM((tm, d), BF16), pltpu.VMEM((nc, V7X_BF16_ROWS, tn), F32),
                        pltpu.VMEM((tn // V7X_LANES, tm, V7X_LANES), F32)],
        compiler_params=_params("arbitrary", "arbitrary"),
        name="ssd_in",
    )(h, g, w_in, w_dt, w_dtT, dt_b, dt_bT, taps, conv_b, *([buf] * (k - 1)))


def _ssd_group_out(x, zs, cb, acum, acumT, mask, xdt, inter, ng, head0, heads_per_group, head_dim):
    lanes = lax.broadcasted_iota(jnp.int32, (x.shape[0], V7X_LANES), 1)
    heads_per_slab = V7X_LANES // head_dim
    parts = []
    for q in range(heads_per_group // heads_per_slab):
        slab = xdt[:, q * V7X_LANES:(q + 1) * V7X_LANES]
        acc = None
        for r in range(heads_per_slab):
            hd = head0 + q * heads_per_slab + r
            seg = acum[:, hd:hd + 1] - acumT[hd:hd + 1, :]
            m = (cb * jnp.exp(jnp.where(mask, seg, MASKED))).astype(BF16)
            in_head = jnp.logical_and(lanes >= r * head_dim, lanes < (r + 1) * head_dim)
            part = _dot(m, jnp.where(in_head, slab, 0.0).astype(BF16))
            acc = part if acc is None else acc + part
        parts.append(acc)
    y = jnp.concatenate(parts, axis=1) + inter
    gated = y * zs
    ms = jnp.mean(gated * gated, axis=-1, keepdims=True)
    return (gated * lax.rsqrt(ms + EPS) * ng).astype(BF16)


def _ssd_prompt_kernel(xs_ref, b_ref, c_ref, zs_ref, dt_ref, dtT_ref, alr_ref, alc_ref, e_ref,
                       d_ref, ng_ref, y_ref, state_ref, st_sc, *, n_groups, head_dim, d_state):
    c = pl.program_id(1)

    @pl.when(c == 0)
    def _():
        st_sc[...] = jnp.zeros_like(st_sc)

    q_rows, d_inner = xs_ref.shape
    gw = d_inner // n_groups
    hpg = gw // head_dim
    row = lax.broadcasted_iota(jnp.int32, (q_rows, q_rows), 0)
    col = lax.broadcasted_iota(jnp.int32, (q_rows, q_rows), 1)
    causal = col <= row
    tril = jnp.where(causal, 1.0, 0.0).astype(BF16)
    triu = jnp.where(row <= col, 1.0, 0.0).astype(BF16)
    dt = dt_ref[...]
    acum = _dot01_lhs(tril, dt * -jnp.exp(alr_ref[...]))
    acumT = _dot01_rhs(dtT_ref[...] * -jnp.exp(alc_ref[...]), triu)
    e = e_ref[...]
    dt_x = _dot01_rhs(dt, e)
    ac_x = _dot01_rhs(acum, e)
    a_end = ac_x[q_rows - 1:q_rows, :]
    x = xs_ref[...].astype(F32)
    xdt = x * dt_x
    to_end = (xdt * jnp.exp(a_end - ac_x)).astype(BF16)
    from_start = jnp.exp(ac_x)
    skip = x * d_ref[...]
    for g in range(n_groups):
        sl = slice(g * gw, (g + 1) * gw)
        ns = slice(g * d_state, (g + 1) * d_state)
        bg, cg = b_ref[:, ns], c_ref[:, ns]
        st = st_sc[:, sl]
        inter = from_start[:, sl] * _dot(cg, st.astype(BF16)) + skip[:, sl]
        y_ref[:, sl] = _ssd_group_out(x[:, sl], zs_ref[:, sl].astype(F32), _dot_nt(cg, bg), acum,
                                      acumT, causal, xdt[:, sl], inter, ng_ref[:, sl],
                                      g * hpg, hpg, head_dim)
        st_sc[:, sl] = jnp.exp(a_end[:, sl]) * st + _dot_tn(bg, to_end[:, sl])

    @pl.when(c == pl.num_programs(1) - 1)
    def _():
        for g in range(n_groups):
            state_ref[g * gw:(g + 1) * gw, :] = st_sc[:, g * gw:(g + 1) * gw].T


def _ssd_prompt(xbc, zs, dt, dtT, alog_row, alog_col, expand, d_x, ng,
                *, n_seq, seq_len, d_inner, n_groups, head_dim, d_state):
    q = SSD_CHUNK
    assert seq_len % q == 0
    nc = seq_len // q
    hp = dt.shape[1]
    gn = n_groups * d_state
    assert d_inner % gn == 0
    rows = lambda b, c: b * nc + c
    const = lambda b, c: (0, 0)
    kern = functools.partial(_ssd_prompt_kernel, n_groups=n_groups, head_dim=head_dim, d_state=d_state)
    return pl.pallas_call(
        kern,
        grid=(n_seq, nc),
        in_specs=[
            pl.BlockSpec((q, d_inner), lambda b, c: (rows(b, c), 0)),
            pl.BlockSpec((q, gn), lambda b, c: (rows(b, c), d_inner // gn)),
            pl.BlockSpec((q, gn), lambda b, c: (rows(b, c), d_inner // gn + 1)),
            pl.BlockSpec((q, d_inner), lambda b, c: (rows(b, c), 0)),
            pl.BlockSpec((q, hp), lambda b, c: (rows(b, c), 0)),
            pl.BlockSpec((hp, q), lambda b, c: (0, rows(b, c))),
            pl.BlockSpec((1, hp), const),
            pl.BlockSpec((hp, 1), const),
            pl.BlockSpec((hp, d_inner), const),
            pl.BlockSpec((1, d_inner), const),
            pl.BlockSpec((1, d_inner), const),
        ],
        out_specs=[
            pl.BlockSpec((q, d_inner), lambda b, c: (rows(b, c), 0)),
            pl.BlockSpec((d_inner, d_state), lambda b, c: (b, 0)),
        ],
        out_shape=[
            jax.ShapeDtypeStruct((n_seq * seq_len, d_inner), BF16),
            jax.ShapeDtypeStruct((n_seq * d_inner, d_state), F32),
        ],
        scratch_shapes=[pltpu.VMEM((d_state, d_inner), F32)],
        compiler_params=_params("arbitrary", "arbitrary"),
        name="ssd_scan_prompt",
    )(xbc, xbc, xbc, zs, dt, dtT, alog_row, alog_col, expand, d_x, ng)


def _ssd_sample_kernel(xs_ref, b_ref, c_ref, zs_ref, dt_ref, dtT_ref, alr_ref, alc_ref, e_ref,
                       d_ref, ng_ref, st_ref, y_ref, nst_ref, *, seq_len, head_dim):
    q_rows, gw = xs_ref.shape
    n_seq = q_rows // seq_len
    row = lax.broadcasted_iota(jnp.int32, (q_rows, q_rows), 0)
    col = lax.broadcasted_iota(jnp.int32, (q_rows, q_rows), 1)
    same = (row // seq_len) == (col // seq_len)
    mask = jnp.logical_and(same, col <= row)
    tril = jnp.where(mask, 1.0, 0.0).astype(BF16)
    triu = jnp.where(jnp.logical_and(same, row <= col), 1.0, 0.0).astype(BF16)
    ends = jnp.where(col == (row // seq_len) * seq_len + (seq_len - 1), 1.0, 0.0).astype(BF16)
    dt = dt_ref[...]
    acum = _dot01_lhs(tril, dt * -jnp.exp(alr_ref[...]))
    acumT = _dot01_rhs(dtT_ref[...] * -jnp.exp(alc_ref[...]), triu)
    e = e_ref[...]
    dt_x = _dot01_rhs(dt, e)
    ac_x = _dot01_rhs(acum, e)
    a_end = _dot01_lhs(ends, ac_x)
    x = xs_ref[...].astype(F32)
    xdt = x * dt_x
    to_endT = (xdt * jnp.exp(a_end - ac_x)).T.astype(BF16)
    decayT = jnp.exp(a_end).T
    bg = b_ref[...].astype(F32)
    cg = c_ref[...].astype(F32)
    seq_of_row = lax.broadcasted_iota(jnp.int32, bg.shape, 0) // seq_len
    inter = jnp.zeros((q_rows, gw), F32)
    for s in range(n_seq):
        st = st_ref[s]
        mine = seq_of_row == s
        inter = inter + _dot_nt(jnp.where(mine, cg, 0.0).astype(BF16), st.astype(BF16))
        bm = jnp.where(mine, bg, 0.0).astype(BF16)
        nst_ref[s] = decayT[:, s * seq_len:s * seq_len + 1] * st + _dot(to_endT, bm)
    inter = jnp.exp(ac_x) * inter + x * d_ref[...]
    y_ref[...] = _ssd_group_out(x, zs_ref[...].astype(F32), _dot_nt(c_ref[...], b_ref[...]), acum,
                                acumT, mask, xdt, inter, ng_ref[...], 0, gw // head_dim, head_dim)


def _ssd_sample(xbc, zs, dt_g, dtT_g, alog_row_g, alog_col_g, expand, d_x, ng, state,
                *, row0, n_rows, seq_len, d_inner, n_groups, head_dim, d_state):
    q = SSD_CHUNK
    assert n_rows % q == 0 and q % seq_len == 0 and row0 % q == 0
    nb = n_rows // q
    spb = q // seq_len
    gw = d_inner // n_groups
    hp = dt_g.shape[2]
    rb0 = row0 // q
    b_col0 = d_inner // d_state
    kern = functools.partial(_ssd_sample_kernel, seq_len=seq_len, head_dim=head_dim)
    return pl.pallas_call(
        kern,
        grid=(nb, n_groups),
        in_specs=[
            pl.BlockSpec((q, gw), lambda s, g: (rb0 + s, g)),
            pl.BlockSpec((q, d_state), lambda s, g: (rb0 + s, b_col0 + g)),
            pl.BlockSpec((q, d_state), lambda s, g: (rb0 + s, b_col0 + n_groups + g)),
            pl.BlockSpec((q, gw), lambda s, g: (rb0 + s, g)),
            pl.BlockSpec((None, q, hp), lambda s, g: (g, s, 0)),
            pl.BlockSpec((None, dtT_g.shape[1], q), lambda s, g: (g, 0, s)),
            pl.BlockSpec((None, 1, hp), lambda s, g: (g, 0, 0)),
            pl.BlockSpec((None, dtT_g.shape[1], 1), lambda s, g: (g, 0, 0)),
            pl.BlockSpec((hp, gw), lambda s, g: (0, 0)),
            pl.BlockSpec((1, gw), lambda s, g: (0, g)),
            pl.BlockSpec((1, gw), lambda s, g: (0, g)),
            pl.BlockSpec((spb, None, gw, d_state), lambda s, g: (s, g, 0, 0)),
        ],
        out_specs=[
            pl.BlockSpec((q, gw), lambda s, g: (s, g)),
            pl.BlockSpec((spb, None, gw, d_state), lambda s, g: (s, g, 0, 0)),
        ],
        out_shape=[
            jax.ShapeDtypeStruct((n_rows, d_inner), BF16),
            jax.ShapeDtypeStruct(state.shape, F32),
        ],
        compiler_params=_params("arbitrary", "arbitrary"),
        name="ssd_scan_sample",
    )(xbc, xbc, xbc, zs, dt_g, dtT_g, alog_row_g, alog_col_g, expand, d_x, ng, state)


def _prompt_conv_state(tail, *, n_prompt_tiles, tiles_per_seq, km1):
    t = tail.reshape(-1, V7X_SUBLANES, tail.shape[1])[:n_prompt_tiles]
    return t[tiles_per_seq - 1::tiles_per_seq, V7X_SUBLANES - km1:, :]


def kernel(x_prompt, x_sample, p_prompt, p_sample, state_sc_conv, state_ssd_conv, state_ssd, g_mix, g_ffn, g_ple, g_final, sc_w_in, sc_w_conv, sc_w_out, ssd_w_in, ssd_conv_w, ssd_conv_b, ssd_dt_bias, ssd_a_log, ssd_d, ssd_norm_g, ssd_w_out, ffn_w_gate, ffn_w_up, ffn_w_down, ple_w_proj, ple_w_gate):
    bp, lp, d = x_prompt.shape
    bs, ls, _ = x_sample.shape
    depth = g_mix.shape[0]
    mp, ms = bp * lp, bs * ls
    pdim = p_prompt.shape[-1]
    n_heads, head_dim, d_state = state_ssd.shape[2:]
    d_inner = n_heads * head_dim
    conv_dim = ssd_conv_w.shape[-1]
    n_groups = (conv_dim - d_inner) // (2 * d_state)
    hpg = n_heads // n_groups
    assert n_heads <= V7X_LANES and V7X_LANES % head_dim == 0 and d_state == V7X_LANES
    npt = mp // ROW_TILE
    tps = lp // ROW_TILE
    row = lambda v: v.reshape(1, -1)
    pp = p_prompt.reshape(depth, mp, pdim)
    ps = p_sample.reshape(depth, ms, pdim)

    h = jnp.concatenate([x_prompt.reshape(mp, d), x_sample.reshape(ms, d)])
    sc_p, sc_s, cv_p, cv_s, st_p, st_s = [], [], [], [], [], []
    y_out = None
    for i in range(depth):
        j = i // 2
        if i % 2 == 0:
            km1 = sc_w_conv.shape[1] - 1
            gated, tail, nstate = _short_conv_in(
                h, row(g_mix[i]), sc_w_in, j, sc_w_conv,
                state_sc_conv.reshape(state_sc_conv.shape[0], bs, km1 * d),
                n_prompt_rows=mp, seq_len_p=lp, seq_len_s=ls)
            sc_p.append(_prompt_conv_state(tail, n_prompt_tiles=npt, tiles_per_seq=tps, km1=km1))
            sc_s.append(jnp.swapaxes(nstate, 0, 1))
            h = _matmul_residual(h, gated, sc_w_out, j, n_prompt_rows=mp)
        else:
            km1 = ssd_conv_w.shape[1] - 1
            zx = d_inner + conv_dim
            pad_h = V7X_LANES - n_heads
            w_dt = jnp.pad(ssd_w_in[j, :, zx:], ((0, 0), (0, pad_h))).astype(BF16)
            dt_b = jnp.pad(ssd_dt_bias[j], (0, pad_h))
            alog = jnp.pad(ssd_a_log[j], (0, pad_h))
            zs, xbc, tail, nstate, dt, dtT = _ssd_in(
                h, row(g_mix[i]), ssd_w_in, j, w_dt, w_dt.T, row(dt_b), dt_b.reshape(-1, 1),
                ssd_conv_w, ssd_conv_b.reshape(ssd_conv_b.shape[0], 1, conv_dim),
                state_ssd_conv.reshape(state_ssd_conv.shape[0], bs, km1 * conv_dim),
                n_prompt_rows=mp, seq_len_p=lp, seq_len_s=ls, d_inner=d_inner)
            cv_p.append(_prompt_conv_state(tail, n_prompt_tiles=npt, tiles_per_seq=tps, km1=km1))
            cv_s.append(jnp.swapaxes(nstate, 0, 1))
            head_of_lane = jnp.arange(d_inner, dtype=jnp.int32) // head_dim
            expand = (jnp.arange(V7X_LANES, dtype=jnp.int32)[:, None] == head_of_lane[None, :]).astype(BF16)
            d_x = row(jnp.repeat(ssd_d[j], head_dim))
            ng = row(ssd_norm_g[j])
            geom = dict(d_inner=d_inner, n_groups=n_groups, head_dim=head_dim, d_state=d_state)
            y_p, new_p = _ssd_prompt(xbc, zs, dt, dtT, row(alog), alog.reshape(-1, 1), expand, d_x, ng,
                                     n_seq=bp, seq_len=lp, **geom)
            dt_s = dt[mp:]
            dt_g = jnp.stack([jnp.roll(dt_s, -g * hpg, axis=1) for g in range(n_groups)])
            alog_g = jnp.stack([jnp.roll(alog, -g * hpg) for g in range(n_groups)])
            dtT_g = dtT[:n_heads, mp:].reshape(n_groups, hpg, ms)
            y_s, new_s = _ssd_sample(
                xbc, zs, dt_g, dtT_g, alog_g.reshape(n_groups, 1, -1),
                ssd_a_log[j].reshape(n_groups, hpg, 1), expand[:, :d_inner // n_groups], d_x, ng,
                state_ssd[j].reshape(bs, n_groups, hpg * head_dim, d_state),
                row0=mp, n_rows=ms, seq_len=ls, **geom)
            st_p.append(new_p.reshape(bp, n_heads, head_dim, d_state))
            st_s.append(new_s.reshape(bs, n_heads, head_dim, d_state))
            h = _matmul_residual(h, (y_p, y_s), ssd_w_out, j, n_prompt_rows=mp)
        h = _ffn(h, row(g_ffn[i]), ffn_w_gate, ffn_w_up, ffn_w_down, i)
        ple_args = (h, pp, ps, row(g_ple[i]), ple_w_gate, ple_w_proj, i)
        if i == depth - 1:
            y_out = _ple(*ple_args, g_final=row(g_final))
        else:
            h = _ple(*ple_args)
    y_p, y_s = y_out
    return (y_p.reshape(bp, lp, d), y_s.reshape(bs, ls, d), jnp.stack(sc_p), jnp.stack(sc_s),
            jnp.stack(cv_p), jnp.stack(cv_s), jnp.stack(st_p), jnp.stack(st_s))
```

```python
import functools

import jax
import jax.numpy as jnp
from jax import lax
from jax.experimental import pallas as pl
from jax.experimental.pallas import tpu as pltpu

F32 = jnp.float32
BF16 = jnp.bfloat16
EPS = 1e-6
MASKED = -1e30
V7X_LANES = 128
V7X_SUBLANES = 8
HISTORY_ROWS = 16
V7X_VMEM_LIMIT = 56 * 1024 * 1024

ROW_TILE = 1024
HALF_ROW_TILE = 512
PLE_ROW_TILE = 256
COL_TILE = 512
NARROW_COL_TILE = 256
SSD_CHUNK = 128
WEIGHT_SLAB_BYTES = 16 * 1024 * 1024


def _params(*sem):
    return pltpu.CompilerParams(dimension_semantics=sem, vmem_limit_bytes=V7X_VMEM_LIMIT)


def _dot(a, b):
    return jnp.dot(a, b, preferred_element_type=F32)


def _dot_nt(a, b):
    return lax.dot_general(a, b, (((1,), (1,)), ((), ())), preferred_element_type=F32)


def _dot_tn(a, b):
    return lax.dot_general(a, b, (((0,), (0,)), ((), ())), preferred_element_type=F32)


def _split3(a):
    a1 = a.astype(BF16)
    r1 = a - a1.astype(F32)
    a2 = r1.astype(BF16)
    a3 = (r1 - a2.astype(F32)).astype(BF16)
    return a3, a2, a1


def _dot01_rhs(a, e):
    p3, p2, p1 = _split3(a)
    return (_dot(p3, e) + _dot(p2, e)) + _dot(p1, e)


def _spread(a, e):
    hi = a.astype(BF16)
    lo = (a - hi.astype(F32)).astype(BF16)
    return _dot(lo, e) + _dot(hi, e)


def _dot01_lhs(t, a):
    p3, p2, p1 = _split3(a)
    return (_dot(t, p3) + _dot(t, p2)) + _dot(t, p1)


def _rmsnorm(x, g):
    ms = jnp.mean(x * x, axis=-1, keepdims=True)
    return x * lax.rsqrt(ms + EPS) * g


def _softplus(x):
    return jnp.maximum(x, 0.0) + jnp.log1p(jnp.exp(-jnp.abs(x)))


def _silu(x):
    return x * jax.nn.sigmoid(x)


def _layer_spec(block, layer, imap):
    return pl.BlockSpec((None,) + tuple(block), lambda *a: (layer,) + tuple(imap(*a)))


def _conv(u, history, taps, hist_sc, seq_len=None):
    k = taps.shape[0]
    tm = u.shape[0]
    hist_sc[:HISTORY_ROWS, :] = history
    hist_sc[HISTORY_ROWS:, :] = u
    out = taps[k - 1:k, :] * u
    if seq_len is not None:
        t = lax.broadcasted_iota(jnp.int32, u.shape, 0) % seq_len
    for d in range(1, k):
        sh = hist_sc[pl.ds(HISTORY_ROWS - d, tm), :]
        if seq_len is not None:
            sh = jnp.where(t >= d, sh, 0.0)
        out = out + taps[k - 1 - d:k - d, :] * sh
    return out


def _state_correction(buf_ref, taps, stage_sc, seq_len):
    k = taps.shape[0]
    km1 = k - 1
    n_seq = stage_sc.shape[1] // seq_len
    stage_sc[...] = jnp.zeros_like(stage_sc)
    rows = [buf_ref[r] for r in range(km1)]
    for t in range(km1):
        acc = None
        for d in range(t + 1, k):
            term = taps[k - 1 - d:k - d, :] * rows[km1 + t - d]
            acc = term if acc is None else acc + term
        for c in range(stage_sc.shape[0]):
            stage_sc[c, pl.ds(t, n_seq, stride=seq_len), :] = acc[:, c * V7X_LANES:(c + 1) * V7X_LANES]


def _staged(stage_sc):
    return jnp.concatenate([stage_sc[c] for c in range(stage_sc.shape[0])], axis=1)


def _emit_sample_state(u, stage_sc, nstate_ref, seq_len):
    km1 = nstate_ref.shape[0]
    n_chunks = stage_sc.shape[0]
    n_seq = stage_sc.shape[1] // seq_len
    for c in range(n_chunks):
        stage_sc[c] = u[:, c * V7X_LANES:(c + 1) * V7X_LANES]
    for r in range(km1):
        nstate_ref[r] = jnp.concatenate(
            [stage_sc[c, pl.ds(seq_len - km1 + r, n_seq, stride=seq_len), :] for c in range(n_chunks)],
            axis=1)


def _sc_in_kernel(xp_ref, xs_ref, g_ref, wb_ref, wc_ref, wv_ref, taps_ref, buf_ref,
                  gated_ref, tail_ref, nstate_ref, hn_sc, carry_sc, hist_sc, stage_sc,
                  *, n_prompt_tiles, tiles_per_seq, sample_len):
    i = pl.program_id(0)
    j = pl.program_id(1)

    @pl.when(j == 0)
    def _():
        x = jnp.where(i < n_prompt_tiles, xp_ref[...], xs_ref[...])
        hn_sc[...] = _rmsnorm(x, g_ref[...]).astype(BF16)

    def project():
        hn = hn_sc[...]
        bg = _dot(hn, wb_ref[...].astype(BF16))
        u = _dot(hn, wc_ref[...].astype(BF16)) * _dot(hn, wv_ref[...].astype(BF16))
        return bg, u

    @pl.when(i < n_prompt_tiles)
    def _():
        bg, u = project()
        history = jnp.where(i % tiles_per_seq == 0, 0.0, carry_sc[j])
        gated_ref[...] = (bg * _conv(u, history, taps_ref[...], hist_sc)).astype(BF16)
        carry_sc[j] = u[u.shape[0] - HISTORY_ROWS:, :]
        tail_ref[...] = u[u.shape[0] - V7X_SUBLANES:, :]

    @pl.when(i >= n_prompt_tiles)
    def _():
        taps = taps_ref[...]
        _state_correction(buf_ref, taps, stage_sc, sample_len)
        bg, u = project()
        conv = _conv(u, jnp.zeros_like(carry_sc[j]), taps, hist_sc, sample_len) + _staged(stage_sc)
        gated_ref[...] = (bg * conv).astype(BF16)
        tail_ref[...] = u[u.shape[0] - V7X_SUBLANES:, :]
        _emit_sample_state(u, stage_sc, nstate_ref, sample_len)


def _short_conv_in(xp, xs, g, w_in, layer, taps, buf, *, seq_len_p, seq_len_s):
    mp, d = xp.shape
    ms = xs.shape[0]
    m = mp + ms
    k = taps.shape[1]
    tm, tn = ROW_TILE, NARROW_COL_TILE
    assert mp % tm == 0 and ms == tm and seq_len_p % tm == 0 and tm % seq_len_s == 0
    assert d % tn == 0 and tn % V7X_LANES == 0 and k - 1 <= min(seq_len_s, V7X_SUBLANES)
    npt, nj = mp // tm, d // tn
    n_tiles = npt + 1
    n_seq_s = ms // seq_len_s
    samp_col = lambda i, j: jnp.where(i >= npt, j, 0)
    kern = functools.partial(_sc_in_kernel, n_prompt_tiles=npt, tiles_per_seq=seq_len_p // tm,
                             sample_len=seq_len_s)
    return pl.pallas_call(
        kern,
        grid=(n_tiles, nj),
        in_specs=[
            pl.BlockSpec((tm, d), lambda i, j: (jnp.minimum(i, npt - 1), 0)),
            pl.BlockSpec((tm, d), lambda i, j: (0, 0), pipeline_mode=pl.Buffered(1)),
            pl.BlockSpec((1, d), lambda i, j: (0, 0)),
            _layer_spec((d, tn), layer, lambda i, j: (0, j)),
            _layer_spec((d, tn), layer, lambda i, j: (0, nj + j)),
            _layer_spec((d, tn), layer, lambda i, j: (0, 2 * nj + j)),
            _layer_spec((k, tn), layer, lambda i, j: (0, j)),
            _layer_spec((k - 1, n_seq_s, tn), layer, lambda i, j: (0, 0, samp_col(i, j))),
        ],
        out_specs=[
            pl.BlockSpec((tm, tn), lambda i, j: (i, j)),
            pl.BlockSpec((V7X_SUBLANES, tn), lambda i, j: (i, j)),
            pl.BlockSpec((k - 1, n_seq_s, tn), lambda i, j: (0, 0, samp_col(i, j))),
        ],
        out_shape=[
            jax.ShapeDtypeStruct((m, d), BF16),
            jax.ShapeDtypeStruct((n_tiles * V7X_SUBLANES, d), F32),
            jax.ShapeDtypeStruct((k - 1, n_seq_s, d), F32),
        ],
        scratch_shapes=[pltpu.VMEM((tm, d), BF16), pltpu.VMEM((nj, HISTORY_ROWS, tn), F32),
                        pltpu.VMEM((HISTORY_ROWS + tm, tn), F32),
                        pltpu.VMEM((tn // V7X_LANES, tm, V7X_LANES), F32)],
        compiler_params=_params("arbitrary", "arbitrary"),
        name="short_conv_in",
    )(xp, xs, g, w_in, w_in, w_in, taps, buf)


def _res_kernel(*refs, n_prompt_tiles, res_pair, a_pair):
    refs = list(refs)
    res_refs = [refs.pop(0) for _ in range(2 if res_pair else 1)]
    a_refs = [refs.pop(0) for _ in range(2 if a_pair else 1)]
    w_ref, out_ref, wb_sc = refs
    i = pl.program_id(1)

    @pl.when(i == 0)
    def _():
        wb_sc[...] = w_ref[...].astype(BF16)

    def body(which):
        out_ref[...] = res_refs[which * res_pair][...] + _dot(a_refs[which * a_pair][...], wb_sc[...])

    if not (res_pair or a_pair):
        body(0)
        return
    pl.when(i < n_prompt_tiles)(lambda: body(0))
    pl.when(i >= n_prompt_tiles)(lambda: body(1))


def _matmul_residual(res, a, w, layer, *, n_prompt_rows):
    res_pair, a_pair = isinstance(res, tuple), isinstance(a, tuple)
    _, k, n = w.shape
    m = sum(r.shape[0] for r in res) if res_pair else res.shape[0]
    tm = HALF_ROW_TILE
    tn = min(n, WEIGHT_SLAB_BYTES // (4 * k))
    assert n_prompt_rows % tm == 0 and m % tm == 0 and n % tn == 0 and tn % V7X_LANES == 0
    npt = n_prompt_rows // tm

    def row_specs(pair, block, col_of):
        if not pair:
            return [pl.BlockSpec(block, lambda j, i: (i, col_of(j)))]
        return [pl.BlockSpec(block, lambda j, i: (jnp.minimum(i, npt - 1), col_of(j))),
                pl.BlockSpec(block, lambda j, i: (jnp.maximum(i - npt, 0), col_of(j)))]

    return pl.pallas_call(
        functools.partial(_res_kernel, n_prompt_tiles=npt, res_pair=res_pair, a_pair=a_pair),
        grid=(n // tn, m // tm),
        in_specs=row_specs(res_pair, (tm, tn), lambda j: j) + row_specs(a_pair, (tm, k), lambda j: 0)
        + [pl.BlockSpec((None, k, tn), lambda j, i: (layer, 0, j), pipeline_mode=pl.Buffered(1))],
        out_specs=pl.BlockSpec((tm, tn), lambda j, i: (i, j)),
        out_shape=jax.ShapeDtypeStruct((m, n), F32),
        scratch_shapes=[pltpu.VMEM((k, tn), BF16)],
        compiler_params=_params("arbitrary", "arbitrary"),
        name="matmul_residual",
    )(*(res if res_pair else (res,)), *(a if a_pair else (a,)), w)


def _ffn_kernel(h_ref, g_ref, wg_ref, wu_ref, wd_ref, out_ref, hn_sc):
    @pl.when(pl.program_id(1) == 0)
    def _():
        h = h_ref[...]
        hn_sc[...] = _rmsnorm(h, g_ref[...]).astype(BF16)
        out_ref[...] = h

    hn = hn_sc[...]
    gate = _dot(hn, wg_ref[...].astype(BF16))
    act = (_silu(gate) * _dot(hn, wu_ref[...].astype(BF16))).astype(BF16)
    out_ref[...] += _dot(act, wd_ref[...].astype(BF16))


def _ffn(h, g, w_gate, w_up, w_down, layer):
    m, d = h.shape
    f = w_gate.shape[2]
    tm, tf = ROW_TILE, NARROW_COL_TILE
    assert m % tm == 0 and f % tf == 0
    return pl.pallas_call(
        _ffn_kernel,
        grid=(m // tm, f // tf),
        in_specs=[
            pl.BlockSpec((tm, d), lambda i, j: (i, 0)),
            pl.BlockSpec((1, d), lambda i, j: (0, 0)),
            _layer_spec((d, tf), layer, lambda i, j: (0, j)),
            _layer_spec((d, tf), layer, lambda i, j: (0, j)),
            _layer_spec((tf, d), layer, lambda i, j: (j, 0)),
        ],
        out_specs=pl.BlockSpec((tm, d), lambda i, j: (i, 0)),
        out_shape=jax.ShapeDtypeStruct((m, d), F32),
        scratch_shapes=[pltpu.VMEM((tm, d), BF16)],
        compiler_params=_params("arbitrary", "arbitrary"),
        name="swiglu_ffn",
    )(h, g, w_gate, w_up, w_down)


def _ple_kernel(*refs, n_prompt_tiles, final):
    i = pl.program_id(0)
    h_ref, pp_ref, ps_ref, g_ref, wg_ref, wp_ref = refs[:6]
    wgb_sc, wpb_sc = refs[-2:]

    @pl.when(i == 0)
    def _():
        wgb_sc[...] = wg_ref[...].astype(BF16)
        wpb_sc[...] = wp_ref[...].astype(BF16)

    h = h_ref[...]
    gate = jax.nn.sigmoid(_dot(_rmsnorm(h, g_ref[...]).astype(BF16), wgb_sc[...]))
    p = jnp.where(i < n_prompt_tiles, pp_ref[...], ps_ref[...]).astype(BF16)
    out = h + _dot(p, wpb_sc[...]) * gate
    if not final:
        refs[6][...] = out
        return
    gf_ref, yp_ref, ys_ref = refs[6:9]
    y = _rmsnorm(out, gf_ref[...])

    @pl.when(i < n_prompt_tiles)
    def _():
        yp_ref[...] = y

    @pl.when(i >= n_prompt_tiles)
    def _():
        ys_ref[...] = y


def _ple(h, pp, ps, g, w_gate, w_proj, layer, g_final=None):
    m, d = h.shape
    _, mp, pdim = pp.shape
    tm = PLE_ROW_TILE
    assert m % tm == 0 and mp % tm == 0
    npt = mp // tm
    final = g_final is not None
    const = lambda i: (0, 0)
    resident = dict(pipeline_mode=pl.Buffered(1))
    in_specs = [
        pl.BlockSpec((tm, d), lambda i: (i, 0)),
        _layer_spec((tm, pdim), layer, lambda i: (jnp.minimum(i, npt - 1), 0)),
        _layer_spec((tm, pdim), layer, lambda i: (jnp.maximum(i - npt, 0), 0)),
        pl.BlockSpec((1, d), const),
        pl.BlockSpec((None, d, d), lambda i: (layer, 0, 0), **resident),
        pl.BlockSpec((None, pdim, d), lambda i: (layer, 0, 0), **resident),
    ]
    args = [h, pp, ps, g, w_gate, w_proj]
    if final:
        in_specs.append(pl.BlockSpec((1, d), const))
        args.append(g_final)
        out_specs = [pl.BlockSpec((tm, d), lambda i: (jnp.minimum(i, npt - 1), 0)),
                     pl.BlockSpec((tm, d), lambda i: (jnp.maximum(i - npt, 0), 0))]
        out_shape = [jax.ShapeDtypeStruct((mp, d), F32), jax.ShapeDtypeStruct((m - mp, d), F32)]
    else:
        out_specs = pl.BlockSpec((tm, d), lambda i: (i, 0))
        out_shape = jax.ShapeDtypeStruct((m, d), F32)
    return pl.pallas_call(
        functools.partial(_ple_kernel, n_prompt_tiles=npt, final=final),
        grid=(m // tm,),
        in_specs=in_specs,
        out_specs=out_specs,
        out_shape=out_shape,
        scratch_shapes=[pltpu.VMEM((d, d), BF16), pltpu.VMEM((pdim, d), BF16)],
        compiler_params=_params("arbitrary"),
        name="ple_final" if final else "ple",
    )(*args)


def _ssd_in_kernel(h_ref, g_ref, w_ref, wdt_ref, dtb_ref, taps_ref, cb_ref, buf_ref,
                   zs_ref, xbc_ref, tail_ref, nstate_ref, dt_ref, dtT_ref,
                   hn_sc, carry_sc, hist_sc, stage_sc,
                   *, n_prompt_tiles, tiles_per_seq, sample_len, n_z_tiles):
    i = pl.program_id(0)
    j = pl.program_id(1)

    @pl.when(j == 0)
    def _():
        hn = _rmsnorm(h_ref[...], g_ref[...]).astype(BF16)
        hn_sc[...] = hn
        dt = _softplus(_dot(hn, wdt_ref[...]) + dtb_ref[...])
        dt_ref[...] = dt
        dtT_ref[...] = dt.T

    def project():
        return _dot(hn_sc[...], w_ref[...].astype(BF16))

    @pl.when(j < n_z_tiles)
    def _():
        zs_ref[...] = _silu(project()).astype(BF16)

    jc = j - n_z_tiles

    def finish(conv):
        return _silu(conv + cb_ref[...]).astype(BF16)

    @pl.when(jnp.logical_and(j >= n_z_tiles, i < n_prompt_tiles))
    def _():
        u = project()
        history = jnp.where(i % tiles_per_seq == 0, 0.0, carry_sc[jc])
        xbc_ref[...] = finish(_conv(u, history, taps_ref[...], hist_sc))
        carry_sc[jc] = u[u.shape[0] - HISTORY_ROWS:, :]
        tail_ref[...] = u[u.shape[0] - V7X_SUBLANES:, :]

    @pl.when(jnp.logical_and(j >= n_z_tiles, i >= n_prompt_tiles))
    def _():
        taps = taps_ref[...]
        _state_correction(buf_ref, taps, stage_sc, sample_len)
        u = project()
        conv = _conv(u, jnp.zeros_like(carry_sc[jc]), taps, hist_sc, sample_len) + _staged(stage_sc)
        xbc_ref[...] = finish(conv)
        tail_ref[...] = u[u.shape[0] - V7X_SUBLANES:, :]
        _emit_sample_state(u, stage_sc, nstate_ref, sample_len)


def _ssd_in(h, g, w_in, layer, w_dt, dt_b, taps, conv_b, buf,
            *, n_prompt_rows, seq_len_p, seq_len_s, d_inner):
    m, d = h.shape
    _, k, conv_dim = taps.shape
    hp = w_dt.shape[1]
    tm, tn = ROW_TILE, COL_TILE
    ms = m - n_prompt_rows
    assert n_prompt_rows % tm == 0 and ms == tm and seq_len_p % tm == 0 and tm % seq_len_s == 0
    assert d_inner % tn == 0 and conv_dim % tn == 0 and k - 1 <= min(seq_len_s, V7X_SUBLANES)
    npt = n_prompt_rows // tm
    n_tiles = npt + 1
    n_seq_s = ms // seq_len_s
    nz, nc = d_inner // tn, conv_dim // tn
    cc = lambda j: jnp.maximum(j - nz, 0)
    samp_col = lambda i, j: jnp.where(i >= npt, cc(j), 0)
    const = lambda i, j: (0, 0)
    kern = functools.partial(_ssd_in_kernel, n_prompt_tiles=npt, tiles_per_seq=seq_len_p // tm,
                             sample_len=seq_len_s, n_z_tiles=nz)
    return pl.pallas_call(
        kern,
        grid=(n_tiles, nz + nc),
        in_specs=[
            pl.BlockSpec((tm, d), lambda i, j: (i, 0)),
            pl.BlockSpec((1, d), const),
            _layer_spec((d, tn), layer, lambda i, j: (0, j)),
            pl.BlockSpec((d, hp), const),
            pl.BlockSpec((1, hp), const),
            _layer_spec((k, tn), layer, lambda i, j: (0, cc(j))),
            _layer_spec((1, tn), layer, lambda i, j: (0, cc(j))),
            _layer_spec((k - 1, n_seq_s, tn), layer, lambda i, j: (0, 0, samp_col(i, j))),
        ],
        out_specs=[
            pl.BlockSpec((tm, tn), lambda i, j: (i, jnp.minimum(j, nz - 1))),
            pl.BlockSpec((tm, tn), lambda i, j: (i, cc(j))),
            pl.BlockSpec((V7X_SUBLANES, tn), lambda i, j: (i, cc(j))),
            pl.BlockSpec((k - 1, n_seq_s, tn), lambda i, j: (0, 0, samp_col(i, j))),
            pl.BlockSpec((tm, hp), lambda i, j: (i, 0)),
            pl.BlockSpec((hp, tm), lambda i, j: (0, i)),
        ],
        out_shape=[
            jax.ShapeDtypeStruct((m, d_inner), BF16),
            jax.ShapeDtypeStruct((m, conv_dim), BF16),
            jax.ShapeDtypeStruct((n_tiles * V7X_SUBLANES, conv_dim), F32),
            jax.ShapeDtypeStruct((k - 1, n_seq_s, conv_dim), F32),
            jax.ShapeDtypeStruct((m, hp), F32),
            jax.ShapeDtypeStruct((hp, m), F32),
        ],
        scratch_shapes=[pltpu.VMEM((tm, d), BF16), pltpu.VMEM((nc, HISTORY_ROWS, tn), F32),
                        pltpu.VMEM((HISTORY_ROWS + tm, tn), F32),
                        pltpu.VMEM((tn // V7X_LANES, tm, V7X_LANES), F32)],
        compiler_params=_params("arbitrary", "arbitrary"),
        name="ssd_in",
    )(h, g, w_in, w_dt, dt_b, taps, conv_b, buf)


def _ssd_group_out(x, zs, cb, acum, acumT, dtT, mask, extra, ng, head0, heads_per_group, head_dim,
                   carried=None):
    heads_per_slab = V7X_LANES // head_dim
    parts = []
    for q in range(heads_per_group // heads_per_slab):
        cols = slice(q * V7X_LANES, (q + 1) * V7X_LANES)
        rhs = x[:, cols]
        if carried is not None:
            cg, st_t = carried
            rhs = jnp.concatenate([rhs, st_t[:, cols].astype(BF16)], axis=0)
        lanes = lax.broadcasted_iota(jnp.int32, rhs.shape, 1)
        acc = None
        for r in range(heads_per_slab):
            hd = head0 + q * heads_per_slab + r
            a_t = jnp.broadcast_to(acum[:, hd:hd + 1], cb.shape)
            seg = a_t - acumT[hd:hd + 1, :]
            lhs = (cb * jnp.exp(jnp.where(mask, seg, MASKED)) * dtT[hd:hd + 1, :]).astype(BF16)
            if carried is not None:
                lhs = jnp.concatenate([lhs, (cg * jnp.exp(a_t)).astype(BF16)], axis=1)
            in_head = jnp.logical_and(lanes >= r * head_dim, lanes < (r + 1) * head_dim)
            part = _dot(lhs, jnp.where(in_head, rhs, jnp.zeros_like(rhs)))
            acc = part if acc is None else acc + part
        parts.append(acc)
    y = jnp.concatenate(parts, axis=1) + extra
    gated = y * zs
    ms = jnp.mean(gated * gated, axis=-1, keepdims=True)
    return (gated * lax.rsqrt(ms + EPS) * ng).astype(BF16)


def _ssd_prompt_kernel(xs_ref, b_ref, c_ref, zs_ref, dt_ref, dtT_ref, alr_ref, alc_ref, e_ref,
                       d_ref, ng_ref, y_ref, state_ref, st_sc, *, n_groups, head_dim, d_state):
    c = pl.program_id(1)

    @pl.when(c == 0)
    def _():
        st_sc[...] = jnp.zeros_like(st_sc)

    q_rows, d_inner = xs_ref.shape
    gw = d_inner // n_groups
    hpg = gw // head_dim
    row = lax.broadcasted_iota(jnp.int32, (q_rows, q_rows), 0)
    col = lax.broadcasted_iota(jnp.int32, (q_rows, q_rows), 1)
    causal = col <= row
    tril = jnp.where(causal, 1.0, 0.0).astype(BF16)
    triu = jnp.where(row <= col, 1.0, 0.0).astype(BF16)
    dt = dt_ref[...]
    dtT = dtT_ref[...]
    acum = _dot01_lhs(tril, dt * -jnp.exp(alr_ref[...]))
    acumT = _dot01_rhs(dtT * -jnp.exp(alc_ref[...]), triu)
    a_end = acum[q_rows - 1:q_rows, :]
    e = e_ref[...]
    x = xs_ref[...]
    xf = x.astype(F32)
    to_end = (xf * _dot((dt * jnp.exp(a_end - acum)).astype(BF16), e)).astype(BF16)
    decay = _dot01_rhs(jnp.broadcast_to(jnp.exp(a_end), (V7X_SUBLANES, a_end.shape[1])), e)[:1, :]
    skip = xf * d_ref[...]
    for g in range(n_groups):
        sl = slice(g * gw, (g + 1) * gw)
        ns = slice(g * d_state, (g + 1) * d_state)
        bg, cg = b_ref[:, ns], c_ref[:, ns]
        st = st_sc[:, sl]
        y_ref[:, sl] = _ssd_group_out(x[:, sl], zs_ref[:, sl].astype(F32), _dot_nt(cg, bg), acum,
                                      acumT, dtT, causal, skip[:, sl], ng_ref[:, sl],
                                      g * hpg, hpg, head_dim,
                                      carried=(cg.astype(F32), st))
        st_sc[:, sl] = decay[:, sl] * st + _dot_tn(bg, to_end[:, sl])

    @pl.when(c == pl.num_programs(1) - 1)
    def _():
        for g in range(n_groups):
            state_ref[g * gw:(g + 1) * gw, :] = st_sc[:, g * gw:(g + 1) * gw].T


def _ssd_prompt(xbc, zs, dt, dtT, alog_row, alog_col, expand, d_x, ng,
                *, n_seq, seq_len, d_inner, n_groups, head_dim, d_state):
    q = SSD_CHUNK
    assert seq_len % q == 0 and q == d_state
    nc = seq_len // q
    hp = dt.shape[1]
    gn = n_groups * d_state
    assert d_inner % gn == 0
    rows = lambda b, c: b * nc + c
    const = lambda b, c: (0, 0)
    kern = functools.partial(_ssd_prompt_kernel, n_groups=n_groups, head_dim=head_dim, d_state=d_state)
    return pl.pallas_call(
        kern,
        grid=(n_seq, nc),
        in_specs=[
            pl.BlockSpec((q, d_inner), lambda b, c: (rows(b, c), 0)),
            pl.BlockSpec((q, gn), lambda b, c: (rows(b, c), d_inner // gn)),
            pl.BlockSpec((q, gn), lambda b, c: (rows(b, c), d_inner // gn + 1)),
            pl.BlockSpec((q, d_inner), lambda b, c: (rows(b, c), 0)),
            pl.BlockSpec((q, hp), lambda b, c: (rows(b, c), 0)),
            pl.BlockSpec((hp, q), lambda b, c: (0, rows(b, c))),
            pl.BlockSpec((1, hp), const),
            pl.BlockSpec((hp, 1), const),
            pl.BlockSpec((hp, d_inner), const),
            pl.BlockSpec((1, d_inner), const),
            pl.BlockSpec((1, d_inner), const),
        ],
        out_specs=[
            pl.BlockSpec((q, d_inner), lambda b, c: (rows(b, c), 0)),
            pl.BlockSpec((d_inner, d_state), lambda b, c: (b, 0)),
        ],
        out_shape=[
            jax.ShapeDtypeStruct((n_seq * seq_len, d_inner), BF16),
            jax.ShapeDtypeStruct((n_seq * d_inner, d_state), F32),
        ],
        scratch_shapes=[pltpu.VMEM((d_state, d_inner), F32)],
        compiler_params=_params("arbitrary", "arbitrary"),
        name="ssd_scan_prompt",
    )(xbc, xbc, xbc, zs, dt, dtT, alog_row, alog_col, expand, d_x, ng)


def _ssd_sample_kernel(xs_ref, b_ref, c_ref, zs_ref, dt_ref, dtT_ref, alr_ref, alc_ref, e_ref,
                       d_ref, ng_ref, st_ref, y_ref, nst_ref, *, seq_len, head_dim):
    q_rows, gw = xs_ref.shape
    n_seq = q_rows // seq_len
    row = lax.broadcasted_iota(jnp.int32, (q_rows, q_rows), 0)
    col = lax.broadcasted_iota(jnp.int32, (q_rows, q_rows), 1)
    same = (row // seq_len) == (col // seq_len)
    mask = jnp.logical_and(same, col <= row)
    tril = jnp.where(mask, 1.0, 0.0).astype(BF16)
    triu = jnp.where(jnp.logical_and(same, row <= col), 1.0, 0.0).astype(BF16)
    ends = jnp.where(col == (row // seq_len) * seq_len + (seq_len - 1), 1.0, 0.0).astype(BF16)
    dt = dt_ref[...]
    dtT = dtT_ref[...]
    acum = _dot01_lhs(tril, dt * -jnp.exp(alr_ref[...]))
    acumT = _dot01_rhs(dtT * -jnp.exp(alc_ref[...]), triu)
    a_end = _dot01_lhs(ends, acum)
    e = e_ref[...]
    x = xs_ref[...].astype(F32)
    to_endT = (x * _dot((dt * jnp.exp(a_end - acum)).astype(BF16), e)).T.astype(BF16)
    decayT = _dot01_rhs(jnp.exp(a_end), e).T
    from_start = _spread(jnp.exp(acum), e)
    bg = b_ref[...].astype(F32)
    cg = c_ref[...].astype(F32)
    seq_of_row = lax.broadcasted_iota(jnp.int32, bg.shape, 0) // seq_len
    inter = jnp.zeros((q_rows, gw), F32)
    for s in range(n_seq):
        st = st_ref[s]
        mine = seq_of_row == s
        inter = inter + _dot_nt(jnp.where(mine, cg, 0.0).astype(BF16), st.astype(BF16))
        bm = jnp.where(mine, bg, 0.0).astype(BF16)
        nst_ref[s] = decayT[:, s * seq_len:s * seq_len + 1] * st + _dot(to_endT, bm)
    inter = from_start * inter + x * d_ref[...]
    y_ref[...] = _ssd_group_out(xs_ref[...], zs_ref[...].astype(F32), _dot_nt(c_ref[...], b_ref[...]),
                                acum, acumT, dtT, mask, inter, ng_ref[...], 0, gw // head_dim, head_dim)


def _ssd_sample(xbc, zs, dt_g, dtT_g, alog_row_g, alog_col_g, expand, d_x, ng, state,
                *, row0, n_rows, seq_len, d_inner, n_groups, head_dim, d_state):
    q = SSD_CHUNK
    assert n_rows % q == 0 and q % seq_len == 0 and row0 % q == 0
    nb = n_rows // q
    spb = q // seq_len
    gw = d_inner // n_groups
    hp = dt_g.shape[2]
    rb0 = row0 // q
    b_col0 = d_inner // d_state
    kern = functools.partial(_ssd_sample_kernel, seq_len=seq_len, head_dim=head_dim)
    return pl.pallas_call(
        kern,
        grid=(nb, n_groups),
        in_specs=[
            pl.BlockSpec((q, gw), lambda s, g: (rb0 + s, g)),
            pl.BlockSpec((q, d_state), lambda s, g: (rb0 + s, b_col0 + g)),
            pl.BlockSpec((q, d_state), lambda s, g: (rb0 + s, b_col0 + n_groups + g)),
            pl.BlockSpec((q, gw), lambda s, g: (rb0 + s, g)),
            pl.BlockSpec((None, q, hp), lambda s, g: (g, s, 0)),
            pl.BlockSpec((None, dtT_g.shape[1], q), lambda s, g: (g, 0, s)),
            pl.BlockSpec((None, 1, hp), lambda s, g: (g, 0, 0)),
            pl.BlockSpec((None, dtT_g.shape[1], 1), lambda s, g: (g, 0, 0)),
            pl.BlockSpec((hp, gw), lambda s, g: (0, 0)),
            pl.BlockSpec((1, gw), lambda s, g: (0, g)),
            pl.BlockSpec((1, gw), lambda s, g: (0, g)),
            pl.BlockSpec((spb, None, gw, d_state), lambda s, g: (s, g, 0, 0)),
        ],
        out_specs=[
            pl.BlockSpec((q, gw), lambda s, g: (s, g)),
            pl.BlockSpec((spb, None, gw, d_state), lambda s, g: (s, g, 0, 0)),
        ],
        out_shape=[
            jax.ShapeDtypeStruct((n_rows, d_inner), BF16),
            jax.ShapeDtypeStruct(state.shape, F32),
        ],
        compiler_params=_params("arbitrary", "arbitrary"),
        name="ssd_scan_sample",
    )(xbc, xbc, xbc, zs, dt_g, dtT_g, alog_row_g, alog_col_g, expand, d_x, ng, state)


def _prompt_conv_state(tail, *, n_prompt_tiles, tiles_per_seq, km1):
    t = tail.reshape(-1, V7X_SUBLANES, tail.shape[1])[:n_prompt_tiles]
    return t[tiles_per_seq - 1::tiles_per_seq, V7X_SUBLANES - km1:, :]


def kernel(x_prompt, x_sample, p_prompt, p_sample, state_sc_conv, state_ssd_conv, state_ssd, g_mix, g_ffn, g_ple, g_final, sc_w_in, sc_w_conv, sc_w_out, ssd_w_in, ssd_conv_w, ssd_conv_b, ssd_dt_bias, ssd_a_log, ssd_d, ssd_norm_g, ssd_w_out, ffn_w_gate, ffn_w_up, ffn_w_down, ple_w_proj, ple_w_gate):
    bp, lp, d = x_prompt.shape
    bs, ls, _ = x_sample.shape
    depth = g_mix.shape[0]
    mp, ms = bp * lp, bs * ls
    pdim = p_prompt.shape[-1]
    n_heads, head_dim, d_state = state_ssd.shape[2:]
    d_inner = n_heads * head_dim
    conv_dim = ssd_conv_w.shape[-1]
    n_groups = (conv_dim - d_inner) // (2 * d_state)
    hpg = n_heads // n_groups
    assert n_heads <= V7X_LANES and V7X_LANES % head_dim == 0 and d_state == V7X_LANES
    npt = mp // ROW_TILE
    tps = lp // ROW_TILE
    row = lambda v: v.reshape(1, -1)
    pp = p_prompt.reshape(depth, mp, pdim)
    ps = p_sample.reshape(depth, ms, pdim)

    h = (x_prompt.reshape(mp, d), x_sample.reshape(ms, d))
    sc_p, sc_s, cv_p, cv_s, st_p, st_s = [], [], [], [], [], []
    y_out = None
    for i in range(depth):
        j = i // 2
        if i % 2 == 0:
            km1 = sc_w_conv.shape[1] - 1
            if not isinstance(h, tuple):
                h = (h[:mp], h[mp:])
            gated, tail, nstate = _short_conv_in(
                *h, row(g_mix[i]), sc_w_in, j, sc_w_conv, jnp.swapaxes(state_sc_conv, 1, 2),
                seq_len_p=lp, seq_len_s=ls)
            sc_p.append(_prompt_conv_state(tail, n_prompt_tiles=npt, tiles_per_seq=tps, km1=km1))
            sc_s.append(jnp.swapaxes(nstate, 0, 1))
            h = _matmul_residual(h, gated, sc_w_out, j, n_prompt_rows=mp)
        else:
            if isinstance(h, tuple):
                h = jnp.concatenate(h)
            km1 = ssd_conv_w.shape[1] - 1
            zx = d_inner + conv_dim
            pad_h = V7X_LANES - n_heads
            w_dt = jnp.pad(ssd_w_in[j, :, zx:], ((0, 0), (0, pad_h))).astype(BF16)
            dt_b = jnp.pad(ssd_dt_bias[j], (0, pad_h))
            alog = jnp.pad(ssd_a_log[j], (0, pad_h))
            zs, xbc, tail, nstate, dt, dtT = _ssd_in(
                h, row(g_mix[i]), ssd_w_in, j, w_dt, row(dt_b),
                ssd_conv_w, ssd_conv_b.reshape(ssd_conv_b.shape[0], 1, conv_dim),
                jnp.swapaxes(state_ssd_conv, 1, 2),
                n_prompt_rows=mp, seq_len_p=lp, seq_len_s=ls, d_inner=d_inner)
            cv_p.append(_prompt_conv_state(tail, n_prompt_tiles=npt, tiles_per_seq=tps, km1=km1))
            cv_s.append(jnp.swapaxes(nstate, 0, 1))
            head_of_lane = jnp.arange(d_inner, dtype=jnp.int32) // head_dim
            expand = (jnp.arange(V7X_LANES, dtype=jnp.int32)[:, None] == head_of_lane[None, :]).astype(BF16)
            d_x = row(jnp.repeat(ssd_d[j], head_dim))
            ng = row(ssd_norm_g[j])
            geom = dict(d_inner=d_inner, n_groups=n_groups, head_dim=head_dim, d_state=d_state)
            y_p, new_p = _ssd_prompt(xbc, zs, dt, dtT, row(alog), alog.reshape(-1, 1), expand, d_x, ng,
                                     n_seq=bp, seq_len=lp, **geom)
            dt_s = dt[mp:]
            dt_g = jnp.stack([jnp.roll(dt_s, -g * hpg, axis=1) for g in range(n_groups)])
            alog_g = jnp.stack([jnp.roll(alog, -g * hpg) for g in range(n_groups)])
            dtT_g = dtT[:n_heads, mp:].reshape(n_groups, hpg, ms)
            y_s, new_s = _ssd_sample(
                xbc, zs, dt_g, dtT_g, alog_g.reshape(n_groups, 1, -1),
                ssd_a_log[j].reshape(n_groups, hpg, 1), expand[:, :d_inner // n_groups], d_x, ng,
                state_ssd[j].reshape(bs, n_groups, hpg * head_dim, d_state),
                row0=mp, n_rows=ms, seq_len=ls, **geom)
            st_p.append(new_p.reshape(bp, n_heads, head_dim, d_state))
            st_s.append(new_s.reshape(bs, n_heads, head_dim, d_state))
            h = _matmul_residual(h, (y_p, y_s), ssd_w_out, j, n_prompt_rows=mp)
        h = _ffn(h, row(g_ffn[i]), ffn_w_gate, ffn_w_up, ffn_w_down, i)
        ple_args = (h, pp, ps, row(g_ple[i]), ple_w_gate, ple_w_proj, i)
        if i == depth - 1:
            y_out = _ple(*ple_args, g_final=row(g_final))
        else:
            h = _ple(*ple_args)
    y_p, y_s = y_out
    return (y_p.reshape(bp, lp, d), y_s.reshape(bs, ls, d), jnp.stack(sc_p), jnp.stack(sc_s),
            jnp.stack(cv_p), jnp.stack(cv_s), jnp.stack(st_p), jnp.stack(st_s))
```

```python
import functools

import jax
import jax.numpy as jnp
from jax import lax
from jax.experimental import pallas as pl
from jax.experimental.pallas import tpu as pltpu

F32 = jnp.float32
BF16 = jnp.bfloat16
EPS = 1e-6
MASKED = -1e30
V7X_LANES = 128
V7X_SUBLANES = 8
HISTORY_ROWS = 16
ROW_PARTS = 4
V7X_VMEM_LIMIT = 56 * 1024 * 1024

ROW_TILE = 1024
HALF_ROW_TILE = 512
PLE_ROW_TILE = 256
COL_TILE = 512
NARROW_COL_TILE = 256
SSD_CHUNK = 128
WEIGHT_SLAB_BYTES = 16 * 1024 * 1024


def _params(*sem):
    return pltpu.CompilerParams(dimension_semantics=sem, vmem_limit_bytes=V7X_VMEM_LIMIT)


def _dot(a, b):
    return jnp.dot(a, b, preferred_element_type=F32)


def _dot_nt(a, b):
    return lax.dot_general(a, b, (((1,), (1,)), ((), ())), preferred_element_type=F32)


def _dot_tn(a, b):
    return lax.dot_general(a, b, (((0,), (0,)), ((), ())), preferred_element_type=F32)


def _split3(a):
    a1 = a.astype(BF16)
    r1 = a - a1.astype(F32)
    a2 = r1.astype(BF16)
    a3 = (r1 - a2.astype(F32)).astype(BF16)
    return a3, a2, a1


def _dot01_rhs(a, e):
    p3, p2, p1 = _split3(a)
    return (_dot(p3, e) + _dot(p2, e)) + _dot(p1, e)


def _spread(a, e):
    hi = a.astype(BF16)
    lo = (a - hi.astype(F32)).astype(BF16)
    return _dot(lo, e) + _dot(hi, e)


def _dot01_lhs(t, a):
    p3, p2, p1 = _split3(a)
    return (_dot(t, p3) + _dot(t, p2)) + _dot(t, p1)


def _rmsnorm(x, g):
    ms = jnp.mean(x * x, axis=-1, keepdims=True)
    return x * lax.rsqrt(ms + EPS) * g


def _softplus(x):
    return jnp.maximum(x, 0.0) + jnp.log1p(jnp.exp(-jnp.abs(x)))


def _silu(x):
    return x * jax.nn.sigmoid(x)


def _layer_spec(block, layer, imap):
    return pl.BlockSpec((None,) + tuple(block), lambda *a: (layer,) + tuple(imap(*a)))


def _conv(hist_ref, taps, seq_len=None, row0=0, n_rows=None):
    k = taps.shape[0]
    if n_rows is None:
        n_rows = hist_ref.shape[0] - HISTORY_ROWS
    out = taps[k - 1:k, :] * hist_ref[pl.ds(HISTORY_ROWS + row0, n_rows), :]
    if seq_len is not None:
        assert row0 % seq_len == 0
        t = lax.broadcasted_iota(jnp.int32, out.shape, 0) % seq_len
    for d in range(1, k):
        sh = hist_ref[pl.ds(HISTORY_ROWS + row0 - d, n_rows), :]
        if seq_len is not None:
            sh = jnp.where(t >= d, sh, 0.0)
        out = out + taps[k - 1 - d:k - d, :] * sh
    return out


def _state_correction(buf_ref, taps, stage_sc, seq_len):
    k = taps.shape[0]
    km1 = k - 1
    n_seq = stage_sc.shape[1] // seq_len
    stage_sc[...] = jnp.zeros_like(stage_sc)
    rows = [buf_ref[r] for r in range(km1)]
    for t in range(km1):
        acc = None
        for d in range(t + 1, k):
            term = taps[k - 1 - d:k - d, :] * rows[km1 + t - d]
            acc = term if acc is None else acc + term
        for c in range(stage_sc.shape[0]):
            stage_sc[c, pl.ds(t, n_seq, stride=seq_len), :] = acc[:, c * V7X_LANES:(c + 1) * V7X_LANES]


def _staged(stage_sc):
    return jnp.concatenate([stage_sc[c] for c in range(stage_sc.shape[0])], axis=1)


def _emit_sample_state(u, stage_sc, nstate_ref, seq_len):
    km1 = nstate_ref.shape[0]
    n_chunks = stage_sc.shape[0]
    n_seq = stage_sc.shape[1] // seq_len
    for c in range(n_chunks):
        stage_sc[c] = u[:, c * V7X_LANES:(c + 1) * V7X_LANES]
    for r in range(km1):
        nstate_ref[r] = jnp.concatenate(
            [stage_sc[c, pl.ds(seq_len - km1 + r, n_seq, stride=seq_len), :] for c in range(n_chunks)],
            axis=1)


def _sc_in_kernel(xp_ref, xs_ref, g_ref, wb_ref, wc_ref, wv_ref, taps_ref, buf_ref,
                  gated_ref, tail_ref, nstate_ref, hn_sc, carry_sc, hist_sc, stage_sc,
                  *, n_prompt_tiles, tiles_per_seq, sample_len):
    i = pl.program_id(0)
    j = pl.program_id(1)

    @pl.when(j == 0)
    def _():
        x = jnp.where(i < n_prompt_tiles, xp_ref[...], xs_ref[...])
        hn_sc[...] = _rmsnorm(x, g_ref[...]).astype(BF16)

    def project():
        hn = hn_sc[...]
        bg = _dot(hn, wb_ref[...].astype(BF16))
        u = _dot(hn, wc_ref[...].astype(BF16)) * _dot(hn, wv_ref[...].astype(BF16))
        return bg, u

    @pl.when(i < n_prompt_tiles)
    def _():
        bg, u = project()
        hist_sc[:HISTORY_ROWS, :] = jnp.where(i % tiles_per_seq == 0, 0.0, carry_sc[j])
        hist_sc[HISTORY_ROWS:, :] = u
        gated_ref[...] = (bg * _conv(hist_sc, taps_ref[...])).astype(BF16)
        carry_sc[j] = u[u.shape[0] - HISTORY_ROWS:, :]
        tail_ref[...] = u[u.shape[0] - V7X_SUBLANES:, :]

    @pl.when(i >= n_prompt_tiles)
    def _():
        taps = taps_ref[...]
        _state_correction(buf_ref, taps, stage_sc, sample_len)
        bg, u = project()
        hist_sc[:HISTORY_ROWS, :] = jnp.zeros_like(carry_sc[j])
        hist_sc[HISTORY_ROWS:, :] = u
        conv = _conv(hist_sc, taps, sample_len) + _staged(stage_sc)
        gated_ref[...] = (bg * conv).astype(BF16)
        tail_ref[...] = u[u.shape[0] - V7X_SUBLANES:, :]
        _emit_sample_state(u, stage_sc, nstate_ref, sample_len)


def _short_conv_in(xp, xs, g, w_in, layer, taps, buf, *, seq_len_p, seq_len_s):
    mp, d = xp.shape
    ms = xs.shape[0]
    m = mp + ms
    k = taps.shape[1]
    tm, tn = ROW_TILE, NARROW_COL_TILE
    assert mp % tm == 0 and ms == tm and seq_len_p % tm == 0 and tm % seq_len_s == 0
    assert d % tn == 0 and tn % V7X_LANES == 0 and k - 1 <= min(seq_len_s, V7X_SUBLANES)
    npt, nj = mp // tm, d // tn
    n_tiles = npt + 1
    n_seq_s = ms // seq_len_s
    samp_col = lambda i, j: jnp.where(i >= npt, j, 0)
    kern = functools.partial(_sc_in_kernel, n_prompt_tiles=npt, tiles_per_seq=seq_len_p // tm,
                             sample_len=seq_len_s)
    return pl.pallas_call(
        kern,
        grid=(n_tiles, nj),
        in_specs=[
            pl.BlockSpec((tm, d), lambda i, j: (jnp.minimum(i, npt - 1), 0)),
            pl.BlockSpec((tm, d), lambda i, j: (0, 0), pipeline_mode=pl.Buffered(1)),
            pl.BlockSpec((1, d), lambda i, j: (0, 0)),
            _layer_spec((d, tn), layer, lambda i, j: (0, j)),
            _layer_spec((d, tn), layer, lambda i, j: (0, nj + j)),
            _layer_spec((d, tn), layer, lambda i, j: (0, 2 * nj + j)),
            _layer_spec((k, tn), layer, lambda i, j: (0, j)),
            _layer_spec((k - 1, n_seq_s, tn), layer, lambda i, j: (0, 0, samp_col(i, j))),
        ],
        out_specs=[
            pl.BlockSpec((tm, tn), lambda i, j: (i, j)),
            pl.BlockSpec((V7X_SUBLANES, tn), lambda i, j: (i, j)),
            pl.BlockSpec((k - 1, n_seq_s, tn), lambda i, j: (0, 0, samp_col(i, j))),
        ],
        out_shape=[
            jax.ShapeDtypeStruct((m, d), BF16),
            jax.ShapeDtypeStruct((n_tiles * V7X_SUBLANES, d), F32),
            jax.ShapeDtypeStruct((k - 1, n_seq_s, d), F32),
        ],
        scratch_shapes=[pltpu.VMEM((tm, d), BF16), pltpu.VMEM((nj, HISTORY_ROWS, tn), F32),
                        pltpu.VMEM((HISTORY_ROWS + tm, tn), F32),
                        pltpu.VMEM((tn // V7X_LANES, tm, V7X_LANES), F32)],
        compiler_params=_params("arbitrary", "arbitrary"),
        name="short_conv_in",
    )(xp, xs, g, w_in, w_in, w_in, taps, buf)


def _res_kernel(*refs, n_prompt_tiles, res_pair, a_pair):
    refs = list(refs)
    res_refs = [refs.pop(0) for _ in range(2 if res_pair else 1)]
    a_refs = [refs.pop(0) for _ in range(2 if a_pair else 1)]
    w_ref, out_ref, wb_sc = refs
    i = pl.program_id(1)

    @pl.when(i == 0)
    def _():
        wb_sc[...] = w_ref[...].astype(BF16)

    def body(which):
        out_ref[...] = res_refs[which * res_pair][...] + _dot(a_refs[which * a_pair][...], wb_sc[...])

    if not (res_pair or a_pair):
        body(0)
        return
    pl.when(i < n_prompt_tiles)(lambda: body(0))
    pl.when(i >= n_prompt_tiles)(lambda: body(1))


def _matmul_residual(res, a, w, layer, *, n_prompt_rows):
    res_pair, a_pair = isinstance(res, tuple), isinstance(a, tuple)
    _, k, n = w.shape
    m = sum(r.shape[0] for r in res) if res_pair else res.shape[0]
    tm = HALF_ROW_TILE
    tn = min(n, WEIGHT_SLAB_BYTES // (4 * k))
    assert n_prompt_rows % tm == 0 and m % tm == 0 and n % tn == 0 and tn % V7X_LANES == 0
    npt = n_prompt_rows // tm

    def row_specs(pair, block, col_of):
        if not pair:
            return [pl.BlockSpec(block, lambda j, i: (i, col_of(j)))]
        return [pl.BlockSpec(block, lambda j, i: (jnp.minimum(i, npt - 1), col_of(j))),
                pl.BlockSpec(block, lambda j, i: (jnp.maximum(i - npt, 0), col_of(j)))]

    return pl.pallas_call(
        functools.partial(_res_kernel, n_prompt_tiles=npt, res_pair=res_pair, a_pair=a_pair),
        grid=(n // tn, m // tm),
        in_specs=row_specs(res_pair, (tm, tn), lambda j: j) + row_specs(a_pair, (tm, k), lambda j: 0)
        + [pl.BlockSpec((None, k, tn), lambda j, i: (layer, 0, j), pipeline_mode=pl.Buffered(1))],
        out_specs=pl.BlockSpec((tm, tn), lambda j, i: (i, j)),
        out_shape=jax.ShapeDtypeStruct((m, n), F32),
        scratch_shapes=[pltpu.VMEM((k, tn), BF16)],
        compiler_params=_params("arbitrary", "arbitrary"),
        name="matmul_residual",
    )(*(res if res_pair else (res,)), *(a if a_pair else (a,)), w)


def _ffn_kernel(h_ref, g_ref, wg_ref, wu_ref, wd_ref, out_ref, hn_sc):
    @pl.when(pl.program_id(1) == 0)
    def _():
        h = h_ref[...]
        hn_sc[...] = _rmsnorm(h, g_ref[...]).astype(BF16)
        out_ref[...] = h

    hn = hn_sc[...]
    gate = _dot(hn, wg_ref[...].astype(BF16))
    act = (_silu(gate) * _dot(hn, wu_ref[...].astype(BF16))).astype(BF16)
    out_ref[...] += _dot(act, wd_ref[...].astype(BF16))


def _ffn(h, g, w_gate, w_up, w_down, layer):
    m, d = h.shape
    f = w_gate.shape[2]
    tm, tf = ROW_TILE, NARROW_COL_TILE
    assert m % tm == 0 and f % tf == 0
    return pl.pallas_call(
        _ffn_kernel,
        grid=(m // tm, f // tf),
        in_specs=[
            pl.BlockSpec((tm, d), lambda i, j: (i, 0)),
            pl.BlockSpec((1, d), lambda i, j: (0, 0)),
            _layer_spec((d, tf), layer, lambda i, j: (0, j)),
            _layer_spec((d, tf), layer, lambda i, j: (0, j)),
            _layer_spec((tf, d), layer, lambda i, j: (j, 0)),
        ],
        out_specs=pl.BlockSpec((tm, d), lambda i, j: (i, 0)),
        out_shape=jax.ShapeDtypeStruct((m, d), F32),
        scratch_shapes=[pltpu.VMEM((tm, d), BF16)],
        compiler_params=_params("arbitrary", "arbitrary"),
        name="swiglu_ffn",
    )(h, g, w_gate, w_up, w_down)


def _ple_kernel(*refs, n_prompt_tiles, final):
    i = pl.program_id(0)
    h_ref, pp_ref, ps_ref, g_ref, wg_ref, wp_ref = refs[:6]
    wgb_sc, wpb_sc = refs[-2:]

    @pl.when(i == 0)
    def _():
        wgb_sc[...] = wg_ref[...].astype(BF16)
        wpb_sc[...] = wp_ref[...].astype(BF16)

    h = h_ref[...]
    gate = jax.nn.sigmoid(_dot(_rmsnorm(h, g_ref[...]).astype(BF16), wgb_sc[...]))
    p = jnp.where(i < n_prompt_tiles, pp_ref[...], ps_ref[...]).astype(BF16)
    out = h + _dot(p, wpb_sc[...]) * gate
    if not final:
        refs[6][...] = out
        return
    gf_ref, yp_ref, ys_ref = refs[6:9]
    y = _rmsnorm(out, gf_ref[...])

    @pl.when(i < n_prompt_tiles)
    def _():
        yp_ref[...] = y

    @pl.when(i >= n_prompt_tiles)
    def _():
        ys_ref[...] = y


def _ple(h, pp, ps, g, w_gate, w_proj, layer, g_final=None):
    m, d = h.shape
    _, mp, pdim = pp.shape
    tm = PLE_ROW_TILE
    assert m % tm == 0 and mp % tm == 0
    npt = mp // tm
    final = g_final is not None
    const = lambda i: (0, 0)
    resident = dict(pipeline_mode=pl.Buffered(1))
    in_specs = [
        pl.BlockSpec((tm, d), lambda i: (i, 0)),
        _layer_spec((tm, pdim), layer, lambda i: (jnp.minimum(i, npt - 1), 0)),
        _layer_spec((tm, pdim), layer, lambda i: (jnp.maximum(i - npt, 0), 0)),
        pl.BlockSpec((1, d), const),
        pl.BlockSpec((None, d, d), lambda i: (layer, 0, 0), **resident),
        pl.BlockSpec((None, pdim, d), lambda i: (layer, 0, 0), **resident),
    ]
    args = [h, pp, ps, g, w_gate, w_proj]
    if final:
        in_specs.append(pl.BlockSpec((1, d), const))
        args.append(g_final)
        out_specs = [pl.BlockSpec((tm, d), lambda i: (jnp.minimum(i, npt - 1), 0)),
                     pl.BlockSpec((tm, d), lambda i: (jnp.maximum(i - npt, 0), 0))]
        out_shape = [jax.ShapeDtypeStruct((mp, d), F32), jax.ShapeDtypeStruct((m - mp, d), F32)]
    else:
        out_specs = pl.BlockSpec((tm, d), lambda i: (i, 0))
        out_shape = jax.ShapeDtypeStruct((m, d), F32)
    return pl.pallas_call(
        functools.partial(_ple_kernel, n_prompt_tiles=npt, final=final),
        grid=(m // tm,),
        in_specs=in_specs,
        out_specs=out_specs,
        out_shape=out_shape,
        scratch_shapes=[pltpu.VMEM((d, d), BF16), pltpu.VMEM((pdim, d), BF16)],
        compiler_params=_params("arbitrary"),
        name="ple_final" if final else "ple",
    )(*args)


def _ssd_in_kernel(h_ref, g_ref, wt_ref, wdtT_ref, dtb_ref, taps_ref, cb_ref, buf_ref,
                   zs_ref, xbc_ref, tail_ref, nstate_ref, dt_ref, dtT_ref,
                   hn_sc, carry_sc, hist_sc, stage_sc,
                   *, n_prompt_tiles, tiles_per_seq, sample_len, n_z_tiles):
    i = pl.program_id(0)
    j = pl.program_id(1)
    jc = j - n_z_tiles
    tm = hn_sc.shape[0]
    rows_per_part = tm // ROW_PARTS

    @pl.when(j == 0)
    def _():
        hn = _rmsnorm(h_ref[...], g_ref[...]).astype(BF16)
        hn_sc[...] = hn
        dt = _softplus(_dot_nt(hn, wdtT_ref[...]) + dtb_ref[...])
        dt_ref[...] = dt
        dtT_ref[...] = dt.T

    def finish(conv):
        return _silu(conv + cb_ref[...]).astype(BF16)

    @pl.when(j < n_z_tiles)
    def _():
        wb = wt_ref[...].astype(BF16)
        for r in range(ROW_PARTS):
            rows = pl.ds(r * rows_per_part, rows_per_part)
            zs_ref[rows, :] = _silu(_dot_nt(hn_sc[rows, :], wb)).astype(BF16)

    def project_into_hist():
        wb = wt_ref[...].astype(BF16)
        for r in range(ROW_PARTS):
            rows = pl.ds(r * rows_per_part, rows_per_part)
            hist_sc[pl.ds(HISTORY_ROWS + r * rows_per_part, rows_per_part), :] = _dot_nt(hn_sc[rows, :], wb)

    @pl.when(jnp.logical_and(j >= n_z_tiles, i < n_prompt_tiles))
    def _():
        hist_sc[:HISTORY_ROWS, :] = jnp.where(i % tiles_per_seq == 0, 0.0, carry_sc[jc])
        project_into_hist()
        taps = taps_ref[...]
        for r in range(ROW_PARTS):
            xbc_ref[pl.ds(r * rows_per_part, rows_per_part), :] = finish(
                _conv(hist_sc, taps, None, r * rows_per_part, rows_per_part))
        carry_sc[jc] = hist_sc[tm:, :]
        tail_ref[...] = hist_sc[HISTORY_ROWS + tm - V7X_SUBLANES:, :]

    @pl.when(jnp.logical_and(j >= n_z_tiles, i >= n_prompt_tiles))
    def _():
        taps = taps_ref[...]
        _state_correction(buf_ref, taps, stage_sc, sample_len)
        hist_sc[:HISTORY_ROWS, :] = jnp.zeros_like(carry_sc[jc])
        project_into_hist()
        xbc_ref[...] = finish(_conv(hist_sc, taps, sample_len) + _staged(stage_sc))
        u = hist_sc[HISTORY_ROWS:, :]
        tail_ref[...] = u[tm - V7X_SUBLANES:, :]
        _emit_sample_state(u, stage_sc, nstate_ref, sample_len)


def _ssd_in(h, g, w_in_t, layer, w_dt_t, dt_b, taps, conv_b, buf,
            *, n_prompt_rows, seq_len_p, seq_len_s, d_inner):
    m, d = h.shape
    _, k, conv_dim = taps.shape
    hp = w_dt_t.shape[0]
    tm, tn = ROW_TILE, COL_TILE
    ms = m - n_prompt_rows
    assert n_prompt_rows % tm == 0 and ms == tm and seq_len_p % tm == 0 and tm % seq_len_s == 0
    assert d_inner % tn == 0 and conv_dim % tn == 0 and k - 1 <= min(seq_len_s, V7X_SUBLANES)
    npt = n_prompt_rows // tm
    n_tiles = npt + 1
    n_seq_s = ms // seq_len_s
    nz, nc = d_inner // tn, conv_dim // tn
    assert tm % (ROW_PARTS * seq_len_s) == 0
    ce = lambda i, j: jnp.maximum(j - nz, 0)
    samp_col = lambda i, j: jnp.where(i >= npt, ce(i, j), 0)
    const = lambda i, j: (0, 0)
    kern = functools.partial(_ssd_in_kernel, n_prompt_tiles=npt, tiles_per_seq=seq_len_p // tm,
                             sample_len=seq_len_s, n_z_tiles=nz)
    return pl.pallas_call(
        kern,
        grid=(n_tiles, nz + nc),
        in_specs=[
            pl.BlockSpec((tm, d), lambda i, j: (i, 0)),
            pl.BlockSpec((1, d), const),
            _layer_spec((tn, d), layer, lambda i, j: (j, 0)),
            pl.BlockSpec((hp, d), const),
            pl.BlockSpec((1, hp), const),
            _layer_spec((k, tn), layer, lambda i, j: (0, ce(i, j))),
            _layer_spec((1, tn), layer, lambda i, j: (0, ce(i, j))),
            _layer_spec((k - 1, n_seq_s, tn), layer, lambda i, j: (0, 0, samp_col(i, j))),
        ],
        out_specs=[
            pl.BlockSpec((tm, tn), lambda i, j: (i, jnp.minimum(j, nz - 1))),
            pl.BlockSpec((tm, tn), lambda i, j: (i, ce(i, j))),
            pl.BlockSpec((V7X_SUBLANES, tn), lambda i, j: (i, ce(i, j))),
            pl.BlockSpec((k - 1, n_seq_s, tn), lambda i, j: (0, 0, samp_col(i, j))),
            pl.BlockSpec((tm, hp), lambda i, j: (i, 0)),
            pl.BlockSpec((hp, tm), lambda i, j: (0, i)),
        ],
        out_shape=[
            jax.ShapeDtypeStruct((m, d_inner), BF16),
            jax.ShapeDtypeStruct((m, conv_dim), BF16),
            jax.ShapeDtypeStruct((n_tiles * V7X_SUBLANES, conv_dim), F32),
            jax.ShapeDtypeStruct((k - 1, n_seq_s, conv_dim), F32),
            jax.ShapeDtypeStruct((m, hp), F32),
            jax.ShapeDtypeStruct((hp, m), F32),
        ],
        scratch_shapes=[pltpu.VMEM((tm, d), BF16), pltpu.VMEM((nc, HISTORY_ROWS, tn), F32),
                        pltpu.VMEM((HISTORY_ROWS + tm, tn), F32),
                        pltpu.VMEM((tn // V7X_LANES, tm, V7X_LANES), F32)],
        compiler_params=_params("arbitrary", "arbitrary"),
        name="ssd_in",
    )(h, g, w_in_t, w_dt_t, dt_b, taps, conv_b, buf)


def _ssd_group_out(x, zs, cb, acum, acumT, dtT, mask, extra, ng, head0, heads_per_group, head_dim,
                   carried=None):
    heads_per_slab = V7X_LANES // head_dim
    parts = []
    for q in range(heads_per_group // heads_per_slab):
        cols = slice(q * V7X_LANES, (q + 1) * V7X_LANES)
        rhs = x[:, cols]
        if carried is not None:
            cg, st_t = carried
            rhs = jnp.concatenate([rhs, st_t[:, cols].astype(BF16)], axis=0)
        lanes = lax.broadcasted_iota(jnp.int32, rhs.shape, 1)
        acc = None
        for r in range(heads_per_slab):
            hd = head0 + q * heads_per_slab + r
            a_t = jnp.broadcast_to(acum[:, hd:hd + 1], cb.shape)
            seg = a_t - acumT[hd:hd + 1, :]
            lhs = (cb * jnp.exp(jnp.where(mask, seg, MASKED)) * dtT[hd:hd + 1, :]).astype(BF16)
            if carried is not None:
                lhs = jnp.concatenate([lhs, (cg * jnp.exp(a_t)).astype(BF16)], axis=1)
            in_head = jnp.logical_and(lanes >= r * head_dim, lanes < (r + 1) * head_dim)
            part = _dot(lhs, jnp.where(in_head, rhs, jnp.zeros_like(rhs)))
            acc = part if acc is None else acc + part
        parts.append(acc)
    y = jnp.concatenate(parts, axis=1) + extra
    gated = y * zs
    ms = jnp.mean(gated * gated, axis=-1, keepdims=True)
    return (gated * lax.rsqrt(ms + EPS) * ng).astype(BF16)


def _ssd_prompt_kernel(xs_ref, b_ref, c_ref, zs_ref, dt_ref, dtT_ref, alr_ref, alc_ref, e_ref,
                       d_ref, ng_ref, y_ref, state_ref, st_sc, *, n_groups, head_dim, d_state):
    c = pl.program_id(1)

    @pl.when(c == 0)
    def _():
        st_sc[...] = jnp.zeros_like(st_sc)

    q_rows, d_inner = xs_ref.shape
    gw = d_inner // n_groups
    hpg = gw // head_dim
    row = lax.broadcasted_iota(jnp.int32, (q_rows, q_rows), 0)
    col = lax.broadcasted_iota(jnp.int32, (q_rows, q_rows), 1)
    causal = col <= row
    tril = jnp.where(causal, 1.0, 0.0).astype(BF16)
    triu = jnp.where(row <= col, 1.0, 0.0).astype(BF16)
    dt = dt_ref[...]
    dtT = dtT_ref[...]
    acum = _dot01_lhs(tril, dt * -jnp.exp(alr_ref[...]))
    acumT = _dot01_rhs(dtT * -jnp.exp(alc_ref[...]), triu)
    a_end = acum[q_rows - 1:q_rows, :]
    e = e_ref[...]
    x = xs_ref[...]
    xf = x.astype(F32)
    to_end = (xf * _dot((dt * jnp.exp(a_end - acum)).astype(BF16), e)).astype(BF16)
    decay = _dot01_rhs(jnp.broadcast_to(jnp.exp(a_end), (V7X_SUBLANES, a_end.shape[1])), e)[:1, :]
    skip = xf * d_ref[...]
    for g in range(n_groups):
        sl = slice(g * gw, (g + 1) * gw)
        ns = slice(g * d_state, (g + 1) * d_state)
        bg, cg = b_ref[:, ns], c_ref[:, ns]
        st = st_sc[:, sl]
        y_ref[:, sl] = _ssd_group_out(x[:, sl], zs_ref[:, sl].astype(F32), _dot_nt(cg, bg), acum,
                                      acumT, dtT, causal, skip[:, sl], ng_ref[:, sl],
                                      g * hpg, hpg, head_dim,
                                      carried=(cg.astype(F32), st))
        st_sc[:, sl] = decay[:, sl] * st + _dot_tn(bg, to_end[:, sl])

    @pl.when(c == pl.num_programs(1) - 1)
    def _():
        for g in range(n_groups):
            state_ref[g * gw:(g + 1) * gw, :] = st_sc[:, g * gw:(g + 1) * gw].T


def _ssd_prompt(xbc, zs, dt, dtT, alog_row, alog_col, expand, d_x, ng,
                *, n_seq, seq_len, d_inner, n_groups, head_dim, d_state):
    q = SSD_CHUNK
    assert seq_len % q == 0 and q == d_state
    nc = seq_len // q
    hp = dt.shape[1]
    gn = n_groups * d_state
    assert d_inner % gn == 0
    rows = lambda b, c: b * nc + c
    const = lambda b, c: (0, 0)
    kern = functools.partial(_ssd_prompt_kernel, n_groups=n_groups, head_dim=head_dim, d_state=d_state)
    return pl.pallas_call(
        kern,
        grid=(n_seq, nc),
        in_specs=[
            pl.BlockSpec((q, d_inner), lambda b, c: (rows(b, c), 0)),
            pl.BlockSpec((q, gn), lambda b, c: (rows(b, c), d_inner // gn)),
            pl.BlockSpec((q, gn), lambda b, c: (rows(b, c), d_inner // gn + 1)),
            pl.BlockSpec((q, d_inner), lambda b, c: (rows(b, c), 0)),
            pl.BlockSpec((q, hp), lambda b, c: (rows(b, c), 0)),
            pl.BlockSpec((hp, q), lambda b, c: (0, rows(b, c))),
            pl.BlockSpec((1, hp), const),
            pl.BlockSpec((hp, 1), const),
            pl.BlockSpec((hp, d_inner), const),
            pl.BlockSpec((1, d_inner), const),
            pl.BlockSpec((1, d_inner), const),
        ],
        out_specs=[
            pl.BlockSpec((q, d_inner), lambda b, c: (rows(b, c), 0)),
            pl.BlockSpec((d_inner, d_state), lambda b, c: (b, 0)),
        ],
        out_shape=[
            jax.ShapeDtypeStruct((n_seq * seq_len, d_inner), BF16),
            jax.ShapeDtypeStruct((n_seq * d_inner, d_state), F32),
        ],
        scratch_shapes=[pltpu.VMEM((d_state, d_inner), F32)],
        compiler_params=_params("arbitrary", "arbitrary"),
        name="ssd_scan_prompt",
    )(xbc, xbc, xbc, zs, dt, dtT, alog_row, alog_col, expand, d_x, ng)


def _ssd_sample_kernel(xs_ref, b_ref, c_ref, zs_ref, dt_ref, dtT_ref, alr_ref, alc_ref, e_ref,
                       d_ref, ng_ref, st_ref, y_ref, nst_ref, *, seq_len, head_dim):
    q_rows, gw = xs_ref.shape
    n_seq = q_rows // seq_len
    row = lax.broadcasted_iota(jnp.int32, (q_rows, q_rows), 0)
    col = lax.broadcasted_iota(jnp.int32, (q_rows, q_rows), 1)
    same = (row // seq_len) == (col // seq_len)
    mask = jnp.logical_and(same, col <= row)
    tril = jnp.where(mask, 1.0, 0.0).astype(BF16)
    triu = jnp.where(jnp.logical_and(same, row <= col), 1.0, 0.0).astype(BF16)
    ends = jnp.where(col == (row // seq_len) * seq_len + (seq_len - 1), 1.0, 0.0).astype(BF16)
    dt = dt_ref[...]
    dtT = dtT_ref[...]
    acum = _dot01_lhs(tril, dt * -jnp.exp(alr_ref[...]))
    acumT = _dot01_rhs(dtT * -jnp.exp(alc_ref[...]), triu)
    a_end = _dot01_lhs(ends, acum)
    e = e_ref[...]
    x = xs_ref[...].astype(F32)
    to_endT = (x * _dot((dt * jnp.exp(a_end - acum)).astype(BF16), e)).T.astype(BF16)
    decayT = _dot01_rhs(jnp.exp(a_end), e).T
    from_start = _spread(jnp.exp(acum), e)
    bg = b_ref[...].astype(F32)
    cg = c_ref[...].astype(F32)
    seq_of_row = lax.broadcasted_iota(jnp.int32, bg.shape, 0) // seq_len
    inter = jnp.zeros((q_rows, gw), F32)
    for s in range(n_seq):
        st = st_ref[s]
        mine = seq_of_row == s
        inter = inter + _dot_nt(jnp.where(mine, cg, 0.0).astype(BF16), st.astype(BF16))
        bm = jnp.where(mine, bg, 0.0).astype(BF16)
        nst_ref[s] = decayT[:, s * seq_len:s * seq_len + 1] * st + _dot(to_endT, bm)
    inter = from_start * inter + x * d_ref[...]
    y_ref[...] = _ssd_group_out(xs_ref[...], zs_ref[...].astype(F32), _dot_nt(c_ref[...], b_ref[...]),
                                acum, acumT, dtT, mask, inter, ng_ref[...], 0, gw // head_dim, head_dim)


def _ssd_sample(xbc, zs, dt_g, dtT_g, alog_row_g, alog_col_g, expand, d_x, ng, state,
                *, row0, n_rows, seq_len, d_inner, n_groups, head_dim, d_state):
    q = SSD_CHUNK
    assert n_rows % q == 0 and q % seq_len == 0 and row0 % q == 0
    nb = n_rows // q
    spb = q // seq_len
    gw = d_inner // n_groups
    hp = dt_g.shape[2]
    rb0 = row0 // q
    b_col0 = d_inner // d_state
    kern = functools.partial(_ssd_sample_kernel, seq_len=seq_len, head_dim=head_dim)
    return pl.pallas_call(
        kern,
        grid=(nb, n_groups),
        in_specs=[
            pl.BlockSpec((q, gw), lambda s, g: (rb0 + s, g)),
            pl.BlockSpec((q, d_state), lambda s, g: (rb0 + s, b_col0 + g)),
            pl.BlockSpec((q, d_state), lambda s, g: (rb0 + s, b_col0 + n_groups + g)),
            pl.BlockSpec((q, gw), lambda s, g: (rb0 + s, g)),
            pl.BlockSpec((None, q, hp), lambda s, g: (g, s, 0)),
            pl.BlockSpec((None, dtT_g.shape[1], q), lambda s, g: (g, 0, s)),
            pl.BlockSpec((None, 1, hp), lambda s, g: (g, 0, 0)),
            pl.BlockSpec((None, dtT_g.shape[1], 1), lambda s, g: (g, 0, 0)),
            pl.BlockSpec((hp, gw), lambda s, g: (0, 0)),
            pl.BlockSpec((1, gw), lambda s, g: (0, g)),
            pl.BlockSpec((1, gw), lambda s, g: (0, g)),
            pl.BlockSpec((spb, None, gw, d_state), lambda s, g: (s, g, 0, 0)),
        ],
        out_specs=[
            pl.BlockSpec((q, gw), lambda s, g: (s, g)),
            pl.BlockSpec((spb, None, gw, d_state), lambda s, g: (s, g, 0, 0)),
        ],
        out_shape=[
            jax.ShapeDtypeStruct((n_rows, d_inner), BF16),
            jax.ShapeDtypeStruct(state.shape, F32),
        ],
        compiler_params=_params("arbitrary", "arbitrary"),
        name="ssd_scan_sample",
    )(xbc, xbc, xbc, zs, dt_g, dtT_g, alog_row_g, alog_col_g, expand, d_x, ng, state)


def _prompt_conv_state(tail, *, n_prompt_tiles, tiles_per_seq, km1):
    t = tail.reshape(-1, V7X_SUBLANES, tail.shape[1])[:n_prompt_tiles]
    return t[tiles_per_seq - 1::tiles_per_seq, V7X_SUBLANES - km1:, :]


def kernel(x_prompt, x_sample, p_prompt, p_sample, state_sc_conv, state_ssd_conv, state_ssd, g_mix, g_ffn, g_ple, g_final, sc_w_in, sc_w_conv, sc_w_out, ssd_w_in, ssd_conv_w, ssd_conv_b, ssd_dt_bias, ssd_a_log, ssd_d, ssd_norm_g, ssd_w_out, ffn_w_gate, ffn_w_up, ffn_w_down, ple_w_proj, ple_w_gate):
    bp, lp, d = x_prompt.shape
    bs, ls, _ = x_sample.shape
    depth = g_mix.shape[0]
    mp, ms = bp * lp, bs * ls
    pdim = p_prompt.shape[-1]
    n_heads, head_dim, d_state = state_ssd.shape[2:]
    d_inner = n_heads * head_dim
    conv_dim = ssd_conv_w.shape[-1]
    n_groups = (conv_dim - d_inner) // (2 * d_state)
    hpg = n_heads // n_groups
    assert n_heads <= V7X_LANES and V7X_LANES % head_dim == 0 and d_state == V7X_LANES
    npt = mp // ROW_TILE
    tps = lp // ROW_TILE
    row = lambda v: v.reshape(1, -1)
    pp = p_prompt.reshape(depth, mp, pdim)
    ps = p_sample.reshape(depth, ms, pdim)

    h = (x_prompt.reshape(mp, d), x_sample.reshape(ms, d))
    sc_p, sc_s, cv_p, cv_s, st_p, st_s = [], [], [], [], [], []
    y_out = None
    for i in range(depth):
        j = i // 2
        if i % 2 == 0:
            km1 = sc_w_conv.shape[1] - 1
            if not isinstance(h, tuple):
                h = (h[:mp], h[mp:])
            gated, tail, nstate = _short_conv_in(
                *h, row(g_mix[i]), sc_w_in, j, sc_w_conv, jnp.swapaxes(state_sc_conv, 1, 2),
                seq_len_p=lp, seq_len_s=ls)
            sc_p.append(_prompt_conv_state(tail, n_prompt_tiles=npt, tiles_per_seq=tps, km1=km1))
            sc_s.append(jnp.swapaxes(nstate, 0, 1))
            h = _matmul_residual(h, gated, sc_w_out, j, n_prompt_rows=mp)
        else:
            if isinstance(h, tuple):
                h = jnp.concatenate(h)
            km1 = ssd_conv_w.shape[1] - 1
            zx = d_inner + conv_dim
            pad_h = V7X_LANES - n_heads
            w_in_t = jnp.swapaxes(ssd_w_in, 1, 2)
            w_dt_t = jnp.pad(w_in_t[j, zx:, :], ((0, pad_h), (0, 0))).astype(BF16)
            dt_b = jnp.pad(ssd_dt_bias[j], (0, pad_h))
            alog = jnp.pad(ssd_a_log[j], (0, pad_h))
            zs, xbc, tail, nstate, dt, dtT = _ssd_in(
                h, row(g_mix[i]), w_in_t, j, w_dt_t, row(dt_b),
                ssd_conv_w, ssd_conv_b.reshape(ssd_conv_b.shape[0], 1, conv_dim),
                jnp.swapaxes(state_ssd_conv, 1, 2),
                n_prompt_rows=mp, seq_len_p=lp, seq_len_s=ls, d_inner=d_inner)
            cv_p.append(_prompt_conv_state(tail, n_prompt_tiles=npt, tiles_per_seq=tps, km1=km1))
            cv_s.append(jnp.swapaxes(nstate, 0, 1))
            head_of_lane = jnp.arange(d_inner, dtype=jnp.int32) // head_dim
            expand = (jnp.arange(V7X_LANES, dtype=jnp.int32)[:, None] == head_of_lane[None, :]).astype(BF16)
            d_x = row(jnp.repeat(ssd_d[j], head_dim))
            ng = row(ssd_norm_g[j])
            geom = dict(d_inner=d_inner, n_groups=n_groups, head_dim=head_dim, d_state=d_state)
            y_p, new_p = _ssd_prompt(xbc, zs, dt, dtT, row(alog), alog.reshape(-1, 1), expand, d_x, ng,
                                     n_seq=bp, seq_len=lp, **geom)
            dt_s = dt[mp:]
            dt_g = jnp.stack([jnp.roll(dt_s, -g * hpg, axis=1) for g in range(n_groups)])
            alog_g = jnp.stack([jnp.roll(alog, -g * hpg) for g in range(n_groups)])
            dtT_g = dtT[:n_heads, mp:].reshape(n_groups, hpg, ms)
            y_s, new_s = _ssd_sample(
                xbc, zs, dt_g, dtT_g, alog_g.reshape(n_groups, 1, -1),
                ssd_a_log[j].reshape(n_groups, hpg, 1), expand[:, :d_inner // n_groups], d_x, ng,
                state_ssd[j].reshape(bs, n_groups, hpg * head_dim, d_state),
                row0=mp, n_rows=ms, seq_len=ls, **geom)
            st_p.append(new_p.reshape(bp, n_heads, head_dim, d_state))
            st_s.append(new_s.reshape(bs, n_heads, head_dim, d_state))
            h = _matmul_residual(h, (y_p, y_s), ssd_w_out, j, n_prompt_rows=mp)
        h = _ffn(h, row(g_ffn[i]), ffn_w_gate, ffn_w_up, ffn_w_down, i)
        ple_args = (h, pp, ps, row(g_ple[i]), ple_w_gate, ple_w_proj, i)
        if i == depth - 1:
            y_out = _ple(*ple_args, g_final=row(g_final))
        else:
            h = _ple(*ple_args)
    y_p, y_s = y_out
    return (y_p.reshape(bp, lp, d), y_s.reshape(bs, ls, d), jnp.stack(sc_p), jnp.stack(sc_s),
            jnp.stack(cv_p), jnp.stack(cv_s), jnp.stack(st_p), jnp.stack(st_s))
```

```python
import functools

import jax
import jax.numpy as jnp
from jax import lax
from jax.experimental import pallas as pl
from jax.experimental.pallas import tpu as pltpu

F32 = jnp.float32
BF16 = jnp.bfloat16
EPS = 1e-6
MASKED = -1e30
V7X_LANES = 128
V7X_SUBLANES = 8
HISTORY_ROWS = 16
ROW_PARTS = 1
V7X_VMEM_LIMIT = 56 * 1024 * 1024

ROW_TILE = 1024
HALF_ROW_TILE = 512
PLE_ROW_TILE = 256
COL_TILE = 512
NARROW_COL_TILE = 256
SSD_CHUNK = 128
WEIGHT_SLAB_BYTES = 16 * 1024 * 1024


def _params(*sem):
    return pltpu.CompilerParams(dimension_semantics=sem, vmem_limit_bytes=V7X_VMEM_LIMIT)


def _dot(a, b):
    return jnp.dot(a, b, preferred_element_type=F32)


def _dot_nt(a, b):
    return lax.dot_general(a, b, (((1,), (1,)), ((), ())), preferred_element_type=F32)


def _dot_tn(a, b):
    return lax.dot_general(a, b, (((0,), (0,)), ((), ())), preferred_element_type=F32)


def _split3(a):
    a1 = a.astype(BF16)
    r1 = a - a1.astype(F32)
    a2 = r1.astype(BF16)
    a3 = (r1 - a2.astype(F32)).astype(BF16)
    return a3, a2, a1


def _dot01_rhs(a, e):
    p3, p2, p1 = _split3(a)
    return (_dot(p3, e) + _dot(p2, e)) + _dot(p1, e)


def _spread(a, e):
    hi = a.astype(BF16)
    lo = (a - hi.astype(F32)).astype(BF16)
    return _dot(lo, e) + _dot(hi, e)


def _dot01_lhs(t, a):
    p3, p2, p1 = _split3(a)
    return (_dot(t, p3) + _dot(t, p2)) + _dot(t, p1)


def _rmsnorm(x, g):
    ms = jnp.mean(x * x, axis=-1, keepdims=True)
    return x * lax.rsqrt(ms + EPS) * g


def _softplus(x):
    return jnp.maximum(x, 0.0) + jnp.log1p(jnp.exp(-jnp.abs(x)))


def _silu(x):
    return x * jax.nn.sigmoid(x)


def _layer_spec(block, layer, imap):
    return pl.BlockSpec((None,) + tuple(block), lambda *a: (layer,) + tuple(imap(*a)))


def _conv(hist_ref, taps, seq_len=None, row0=0, n_rows=None):
    k = taps.shape[0]
    if n_rows is None:
        n_rows = hist_ref.shape[0] - HISTORY_ROWS
    out = taps[k - 1:k, :] * hist_ref[pl.ds(HISTORY_ROWS + row0, n_rows), :]
    if seq_len is not None:
        assert row0 % seq_len == 0
        t = lax.broadcasted_iota(jnp.int32, out.shape, 0) % seq_len
    for d in range(1, k):
        sh = hist_ref[pl.ds(HISTORY_ROWS + row0 - d, n_rows), :]
        if seq_len is not None:
            sh = jnp.where(t >= d, sh, 0.0)
        out = out + taps[k - 1 - d:k - d, :] * sh
    return out


def _state_correction(buf_ref, taps, stage_sc, seq_len):
    k = taps.shape[0]
    km1 = k - 1
    n_seq = stage_sc.shape[1] // seq_len
    stage_sc[...] = jnp.zeros_like(stage_sc)
    rows = [buf_ref[r] for r in range(km1)]
    for t in range(km1):
        acc = None
        for d in range(t + 1, k):
            term = taps[k - 1 - d:k - d, :] * rows[km1 + t - d]
            acc = term if acc is None else acc + term
        for c in range(stage_sc.shape[0]):
            stage_sc[c, pl.ds(t, n_seq, stride=seq_len), :] = acc[:, c * V7X_LANES:(c + 1) * V7X_LANES]


def _staged(stage_sc):
    return jnp.concatenate([stage_sc[c] for c in range(stage_sc.shape[0])], axis=1)


def _emit_sample_state(u, stage_sc, nstate_ref, seq_len):
    km1 = nstate_ref.shape[0]
    n_chunks = stage_sc.shape[0]
    n_seq = stage_sc.shape[1] // seq_len
    for c in range(n_chunks):
        stage_sc[c] = u[:, c * V7X_LANES:(c + 1) * V7X_LANES]
    for r in range(km1):
        nstate_ref[r] = jnp.concatenate(
            [stage_sc[c, pl.ds(seq_len - km1 + r, n_seq, stride=seq_len), :] for c in range(n_chunks)],
            axis=1)


def _sc_in_kernel(xp_ref, xs_ref, g_ref, wb_ref, wc_ref, wv_ref, taps_ref, buf_ref,
                  gated_ref, tail_ref, nstate_ref, hn_sc, carry_sc, hist_sc, stage_sc,
                  *, n_prompt_tiles, tiles_per_seq, sample_len):
    i = pl.program_id(0)
    j = pl.program_id(1)

    @pl.when(j == 0)
    def _():
        x = jnp.where(i < n_prompt_tiles, xp_ref[...], xs_ref[...])
        hn_sc[...] = _rmsnorm(x, g_ref[...]).astype(BF16)

    def project():
        hn = hn_sc[...]
        bg = _dot(hn, wb_ref[...].astype(BF16))
        u = _dot(hn, wc_ref[...].astype(BF16)) * _dot(hn, wv_ref[...].astype(BF16))
        return bg, u

    @pl.when(i < n_prompt_tiles)
    def _():
        bg, u = project()
        hist_sc[:HISTORY_ROWS, :] = jnp.where(i % tiles_per_seq == 0, 0.0, carry_sc[j])
        hist_sc[HISTORY_ROWS:, :] = u
        gated_ref[...] = (bg * _conv(hist_sc, taps_ref[...])).astype(BF16)
        carry_sc[j] = u[u.shape[0] - HISTORY_ROWS:, :]
        tail_ref[...] = u[u.shape[0] - V7X_SUBLANES:, :]

    @pl.when(i >= n_prompt_tiles)
    def _():
        taps = taps_ref[...]
        _state_correction(buf_ref, taps, stage_sc, sample_len)
        bg, u = project()
        hist_sc[:HISTORY_ROWS, :] = jnp.zeros_like(carry_sc[j])
        hist_sc[HISTORY_ROWS:, :] = u
        conv = _conv(hist_sc, taps, sample_len) + _staged(stage_sc)
        gated_ref[...] = (bg * conv).astype(BF16)
        tail_ref[...] = u[u.shape[0] - V7X_SUBLANES:, :]
        _emit_sample_state(u, stage_sc, nstate_ref, sample_len)


def _short_conv_in(xp, xs, g, w_in, layer, taps, buf, *, seq_len_p, seq_len_s):
    mp, d = xp.shape
    ms = xs.shape[0]
    m = mp + ms
    k = taps.shape[1]
    tm, tn = ROW_TILE, NARROW_COL_TILE
    assert mp % tm == 0 and ms == tm and seq_len_p % tm == 0 and tm % seq_len_s == 0
    assert d % tn == 0 and tn % V7X_LANES == 0 and k - 1 <= min(seq_len_s, V7X_SUBLANES)
    npt, nj = mp // tm, d // tn
    n_tiles = npt + 1
    n_seq_s = ms // seq_len_s
    samp_col = lambda i, j: jnp.where(i >= npt, j, 0)
    kern = functools.partial(_sc_in_kernel, n_prompt_tiles=npt, tiles_per_seq=seq_len_p // tm,
                             sample_len=seq_len_s)
    return pl.pallas_call(
        kern,
        grid=(n_tiles, nj),
        in_specs=[
            pl.BlockSpec((tm, d), lambda i, j: (jnp.minimum(i, npt - 1), 0)),
            pl.BlockSpec((tm, d), lambda i, j: (0, 0), pipeline_mode=pl.Buffered(1)),
            pl.BlockSpec((1, d), lambda i, j: (0, 0)),
            _layer_spec((d, tn), layer, lambda i, j: (0, j)),
            _layer_spec((d, tn), layer, lambda i, j: (0, nj + j)),
            _layer_spec((d, tn), layer, lambda i, j: (0, 2 * nj + j)),
            _layer_spec((k, tn), layer, lambda i, j: (0, j)),
            _layer_spec((k - 1, n_seq_s, tn), layer, lambda i, j: (0, 0, samp_col(i, j))),
        ],
        out_specs=[
            pl.BlockSpec((tm, tn), lambda i, j: (i, j)),
            pl.BlockSpec((V7X_SUBLANES, tn), lambda i, j: (i, j)),
            pl.BlockSpec((k - 1, n_seq_s, tn), lambda i, j: (0, 0, samp_col(i, j))),
        ],
        out_shape=[
            jax.ShapeDtypeStruct((m, d), BF16),
            jax.ShapeDtypeStruct((n_tiles * V7X_SUBLANES, d), F32),
            jax.ShapeDtypeStruct((k - 1, n_seq_s, d), F32),
        ],
        scratch_shapes=[pltpu.VMEM((tm, d), BF16), pltpu.VMEM((nj, HISTORY_ROWS, tn), F32),
                        pltpu.VMEM((HISTORY_ROWS + tm, tn), F32),
                        pltpu.VMEM((tn // V7X_LANES, tm, V7X_LANES), F32)],
        compiler_params=_params("arbitrary", "arbitrary"),
        name="short_conv_in",
    )(xp, xs, g, w_in, w_in, w_in, taps, buf)


def _res_kernel(*refs, n_prompt_tiles, res_pair, a_pair):
    refs = list(refs)
    res_refs = [refs.pop(0) for _ in range(2 if res_pair else 1)]
    a_refs = [refs.pop(0) for _ in range(2 if a_pair else 1)]
    w_ref, out_ref, wb_sc = refs
    i = pl.program_id(1)

    @pl.when(i == 0)
    def _():
        wb_sc[...] = w_ref[...].astype(BF16)

    def body(which):
        out_ref[...] = res_refs[which * res_pair][...] + _dot(a_refs[which * a_pair][...], wb_sc[...])

    if not (res_pair or a_pair):
        body(0)
        return
    pl.when(i < n_prompt_tiles)(lambda: body(0))
    pl.when(i >= n_prompt_tiles)(lambda: body(1))


def _matmul_residual(res, a, w, layer, *, n_prompt_rows):
    res_pair, a_pair = isinstance(res, tuple), isinstance(a, tuple)
    _, k, n = w.shape
    m = sum(r.shape[0] for r in res) if res_pair else res.shape[0]
    tm = HALF_ROW_TILE
    tn = min(n, WEIGHT_SLAB_BYTES // (4 * k))
    assert n_prompt_rows % tm == 0 and m % tm == 0 and n % tn == 0 and tn % V7X_LANES == 0
    npt = n_prompt_rows // tm

    def row_specs(pair, block, col_of):
        if not pair:
            return [pl.BlockSpec(block, lambda j, i: (i, col_of(j)))]
        return [pl.BlockSpec(block, lambda j, i: (jnp.minimum(i, npt - 1), col_of(j))),
                pl.BlockSpec(block, lambda j, i: (jnp.maximum(i - npt, 0), col_of(j)))]

    return pl.pallas_call(
        functools.partial(_res_kernel, n_prompt_tiles=npt, res_pair=res_pair, a_pair=a_pair),
        grid=(n // tn, m // tm),
        in_specs=row_specs(res_pair, (tm, tn), lambda j: j) + row_specs(a_pair, (tm, k), lambda j: 0)
        + [pl.BlockSpec((None, k, tn), lambda j, i: (layer, 0, j), pipeline_mode=pl.Buffered(1))],
        out_specs=pl.BlockSpec((tm, tn), lambda j, i: (i, j)),
        out_shape=jax.ShapeDtypeStruct((m, n), F32),
        scratch_shapes=[pltpu.VMEM((k, tn), BF16)],
        compiler_params=_params("arbitrary", "arbitrary"),
        name="matmul_residual",
    )(*(res if res_pair else (res,)), *(a if a_pair else (a,)), w)


def _ffn_kernel(h_ref, g_ref, wg_ref, wu_ref, wd_ref, out_ref, *rest):
    *cast_refs, hn_sc = rest

    @pl.when(pl.program_id(1) == 0)
    def _():
        h = h_ref[...]
        hn_sc[...] = _rmsnorm(h, g_ref[...]).astype(BF16)
        out_ref[...] = h

    wg, wu, wd = wg_ref[...], wu_ref[...], wd_ref[...]
    if cast_refs:
        wg, wu, wd = wg.astype(BF16), wu.astype(BF16), wd.astype(BF16)
        for ref, w in zip(cast_refs, (wg, wu, wd)):
            ref[...] = w
    hn = hn_sc[...]
    gate = _dot(hn, wg)
    act = (_silu(gate) * _dot(hn, wu)).astype(BF16)
    out_ref[...] += _dot(act, wd)


def _ffn(h, g, w_gate, w_up, w_down, layer):
    m, d = h.shape
    f = w_gate.shape[2]
    tm = ROW_TILE
    assert m % tm == 0 and m > tm and f % COL_TILE == 0 and f % NARROW_COL_TILE == 0

    def call(row0_tile, n_row_tiles, tf, weights, weight_specs, cast_outputs, name):
        out_specs = [pl.BlockSpec((tm, d), lambda i, j: (i, 0))]
        out_shape = [jax.ShapeDtypeStruct((n_row_tiles * tm, d), F32)]
        if cast_outputs:
            out_specs += [pl.BlockSpec((d, tf), lambda i, j: (0, j)), pl.BlockSpec((d, tf), lambda i, j: (0, j)),
                          pl.BlockSpec((tf, d), lambda i, j: (j, 0))]
            out_shape += [jax.ShapeDtypeStruct((d, f), BF16), jax.ShapeDtypeStruct((d, f), BF16),
                          jax.ShapeDtypeStruct((f, d), BF16)]
        h_mode = dict(pipeline_mode=pl.Buffered(1)) if n_row_tiles == 1 else {}
        return pl.pallas_call(
            _ffn_kernel,
            grid=(n_row_tiles, f // tf),
            in_specs=[pl.BlockSpec((tm, d), lambda i, j: (i + row0_tile, 0), **h_mode),
                      pl.BlockSpec((1, d), lambda i, j: (0, 0))] + weight_specs(tf),
            out_specs=out_specs,
            out_shape=out_shape,
            scratch_shapes=[pltpu.VMEM((tm, d), BF16)],
            compiler_params=_params("arbitrary", "arbitrary"),
            name=name,
        )(h, g, *weights)

    f32_specs = lambda tf: [_layer_spec((d, tf), layer, lambda i, j: (0, j)),
                            _layer_spec((d, tf), layer, lambda i, j: (0, j)),
                            _layer_spec((tf, d), layer, lambda i, j: (j, 0))]
    bf16_specs = lambda tf: [pl.BlockSpec((d, tf), lambda i, j: (0, j)), pl.BlockSpec((d, tf), lambda i, j: (0, j)),
                             pl.BlockSpec((tf, d), lambda i, j: (j, 0))]
    first, wg_b, wu_b, wd_b = call(0, 1, NARROW_COL_TILE, (w_gate, w_up, w_down), f32_specs, True,
                                   "swiglu_ffn_first")
    rest, = call(1, m // tm - 1, COL_TILE, (wg_b, wu_b, wd_b), bf16_specs, False, "swiglu_ffn_rest")
    return first, rest


def _ple_kernel(ha_ref, hb_ref, pp_ref, ps_ref, g_ref, wg_ref, wp_ref, *rest,
                n_first_tiles, n_prompt_tiles, final):
    i = pl.program_id(0)
    *rest, wgb_sc, wpb_sc = rest

    @pl.when(i == 0)
    def _():
        wgb_sc[...] = wg_ref[...].astype(BF16)
        wpb_sc[...] = wp_ref[...].astype(BF16)

    h = jnp.where(i < n_first_tiles, ha_ref[...], hb_ref[...])
    gate = jax.nn.sigmoid(_dot(_rmsnorm(h, g_ref[...]).astype(BF16), wgb_sc[...]))
    p = jnp.where(i < n_prompt_tiles, pp_ref[...], ps_ref[...]).astype(BF16)
    out = h + _dot(p, wpb_sc[...]) * gate
    if not final:
        rest[0][...] = out
        return
    gf_ref, yp_ref, ys_ref = rest
    y = _rmsnorm(out, gf_ref[...])

    @pl.when(i < n_prompt_tiles)
    def _():
        yp_ref[...] = y

    @pl.when(i >= n_prompt_tiles)
    def _():
        ys_ref[...] = y


def _ple(h, pp, ps, g, w_gate, w_proj, layer, g_final=None):
    ha, hb = h
    d = ha.shape[1]
    m = ha.shape[0] + hb.shape[0]
    _, mp, pdim = pp.shape
    tm = PLE_ROW_TILE
    assert m % tm == 0 and mp % tm == 0 and ha.shape[0] % tm == 0
    npt = mp // tm
    nft = ha.shape[0] // tm
    final = g_final is not None
    const = lambda i: (0, 0)
    resident = dict(pipeline_mode=pl.Buffered(1))
    in_specs = [
        pl.BlockSpec((tm, d), lambda i: (jnp.minimum(i, nft - 1), 0)),
        pl.BlockSpec((tm, d), lambda i: (jnp.maximum(i - nft, 0), 0)),
        _layer_spec((tm, pdim), layer, lambda i: (jnp.minimum(i, npt - 1), 0)),
        _layer_spec((tm, pdim), layer, lambda i: (jnp.maximum(i - npt, 0), 0)),
        pl.BlockSpec((1, d), const),
        pl.BlockSpec((None, d, d), lambda i: (layer, 0, 0), **resident),
        pl.BlockSpec((None, pdim, d), lambda i: (layer, 0, 0), **resident),
    ]
    args = [ha, hb, pp, ps, g, w_gate, w_proj]
    if final:
        in_specs.append(pl.BlockSpec((1, d), const))
        args.append(g_final)
        out_specs = [pl.BlockSpec((tm, d), lambda i: (jnp.minimum(i, npt - 1), 0)),
                     pl.BlockSpec((tm, d), lambda i: (jnp.maximum(i - npt, 0), 0))]
        out_shape = [jax.ShapeDtypeStruct((mp, d), F32), jax.ShapeDtypeStruct((m - mp, d), F32)]
    else:
        out_specs = pl.BlockSpec((tm, d), lambda i: (i, 0))
        out_shape = jax.ShapeDtypeStruct((m, d), F32)
    return pl.pallas_call(
        functools.partial(_ple_kernel, n_first_tiles=nft, n_prompt_tiles=npt, final=final),
        grid=(m // tm,),
        in_specs=in_specs,
        out_specs=out_specs,
        out_shape=out_shape,
        scratch_shapes=[pltpu.VMEM((d, d), BF16), pltpu.VMEM((pdim, d), BF16)],
        compiler_params=_params("arbitrary"),
        name="ple_final" if final else "ple",
    )(*args)


def _ssd_in_kernel(h_ref, g_ref, wt_ref, wdtT_ref, dtb_ref, taps_ref, cb_ref, buf_ref,
                   zs_ref, xbc_ref, tail_ref, nstate_ref, dt_ref, dtT_ref,
                   hn_sc, carry_sc, hist_sc, stage_sc,
                   *, n_prompt_tiles, tiles_per_seq, sample_len, n_z_tiles):
    i = pl.program_id(0)
    j = pl.program_id(1)
    jc = j - n_z_tiles
    tm = hn_sc.shape[0]
    rows_per_part = tm // ROW_PARTS

    @pl.when(j == 0)
    def _():
        hn = _rmsnorm(h_ref[...], g_ref[...]).astype(BF16)
        hn_sc[...] = hn
        dt = _softplus(_dot_nt(hn, wdtT_ref[...]) + dtb_ref[...])
        dt_ref[...] = dt
        dtT_ref[...] = dt.T

    def finish(conv):
        return _silu(conv + cb_ref[...]).astype(BF16)

    @pl.when(j < n_z_tiles)
    def _():
        wb = wt_ref[...].astype(BF16)
        for r in range(ROW_PARTS):
            rows = pl.ds(r * rows_per_part, rows_per_part)
            zs_ref[rows, :] = _silu(_dot_nt(hn_sc[rows, :], wb)).astype(BF16)

    def project_into_hist():
        wb = wt_ref[...].astype(BF16)
        for r in range(ROW_PARTS):
            rows = pl.ds(r * rows_per_part, rows_per_part)
            hist_sc[pl.ds(HISTORY_ROWS + r * rows_per_part, rows_per_part), :] = _dot_nt(hn_sc[rows, :], wb)

    @pl.when(jnp.logical_and(j >= n_z_tiles, i < n_prompt_tiles))
    def _():
        hist_sc[:HISTORY_ROWS, :] = jnp.where(i % tiles_per_seq == 0, 0.0, carry_sc[jc])
        project_into_hist()
        taps = taps_ref[...]
        for r in range(ROW_PARTS):
            xbc_ref[pl.ds(r * rows_per_part, rows_per_part), :] = finish(
                _conv(hist_sc, taps, None, r * rows_per_part, rows_per_part))
        carry_sc[jc] = hist_sc[tm:, :]
        tail_ref[...] = hist_sc[HISTORY_ROWS + tm - V7X_SUBLANES:, :]

    @pl.when(jnp.logical_and(j >= n_z_tiles, i >= n_prompt_tiles))
    def _():
        taps = taps_ref[...]
        _state_correction(buf_ref, taps, stage_sc, sample_len)
        hist_sc[:HISTORY_ROWS, :] = jnp.zeros_like(carry_sc[jc])
        project_into_hist()
        xbc_ref[...] = finish(_conv(hist_sc, taps, sample_len) + _staged(stage_sc))
        u = hist_sc[HISTORY_ROWS:, :]
        tail_ref[...] = u[tm - V7X_SUBLANES:, :]
        _emit_sample_state(u, stage_sc, nstate_ref, sample_len)


def _ssd_in(h, g, w_in_t, layer, w_dt_t, dt_b, taps, conv_b, buf,
            *, n_prompt_rows, seq_len_p, seq_len_s, d_inner):
    m, d = h.shape
    _, k, conv_dim = taps.shape
    hp = w_dt_t.shape[0]
    tm, tn = ROW_TILE, COL_TILE
    ms = m - n_prompt_rows
    assert n_prompt_rows % tm == 0 and ms == tm and seq_len_p % tm == 0 and tm % seq_len_s == 0
    assert d_inner % tn == 0 and conv_dim % tn == 0 and k - 1 <= min(seq_len_s, V7X_SUBLANES)
    npt = n_prompt_rows // tm
    n_tiles = npt + 1
    n_seq_s = ms // seq_len_s
    nz, nc = d_inner // tn, conv_dim // tn
    assert tm % (ROW_PARTS * seq_len_s) == 0
    ce = lambda i, j: jnp.maximum(j - nz, 0)
    samp_col = lambda i, j: jnp.where(i >= npt, ce(i, j), 0)
    const = lambda i, j: (0, 0)
    kern = functools.partial(_ssd_in_kernel, n_prompt_tiles=npt, tiles_per_seq=seq_len_p // tm,
                             sample_len=seq_len_s, n_z_tiles=nz)
    return pl.pallas_call(
        kern,
        grid=(n_tiles, nz + nc),
        in_specs=[
            pl.BlockSpec((tm, d), lambda i, j: (i, 0)),
            pl.BlockSpec((1, d), const),
            _layer_spec((tn, d), layer, lambda i, j: (j, 0)),
            pl.BlockSpec((hp, d), const),
            pl.BlockSpec((1, hp), const),
            _layer_spec((k, tn), layer, lambda i, j: (0, ce(i, j))),
            _layer_spec((1, tn), layer, lambda i, j: (0, ce(i, j))),
            _layer_spec((k - 1, n_seq_s, tn), layer, lambda i, j: (0, 0, samp_col(i, j))),
        ],
        out_specs=[
            pl.BlockSpec((tm, tn), lambda i, j: (i, jnp.minimum(j, nz - 1))),
            pl.BlockSpec((tm, tn), lambda i, j: (i, ce(i, j))),
            pl.BlockSpec((V7X_SUBLANES, tn), lambda i, j: (i, ce(i, j))),
            pl.BlockSpec((k - 1, n_seq_s, tn), lambda i, j: (0, 0, samp_col(i, j))),
            pl.BlockSpec((tm, hp), lambda i, j: (i, 0)),
            pl.BlockSpec((hp, tm), lambda i, j: (0, i)),
        ],
        out_shape=[
            jax.ShapeDtypeStruct((m, d_inner), BF16),
            jax.ShapeDtypeStruct((m, conv_dim), BF16),
            jax.ShapeDtypeStruct((n_tiles * V7X_SUBLANES, conv_dim), F32),
            jax.ShapeDtypeStruct((k - 1, n_seq_s, conv_dim), F32),
            jax.ShapeDtypeStruct((m, hp), F32),
            jax.ShapeDtypeStruct((hp, m), F32),
        ],
        scratch_shapes=[pltpu.VMEM((tm, d), BF16), pltpu.VMEM((nc, HISTORY_ROWS, tn), F32),
                        pltpu.VMEM((HISTORY_ROWS + tm, tn), F32),
                        pltpu.VMEM((tn // V7X_LANES, tm, V7X_LANES), F32)],
        compiler_params=_params("arbitrary", "arbitrary"),
        name="ssd_in",
    )(h, g, w_in_t, w_dt_t, dt_b, taps, conv_b, buf)


def _ssd_group_out(x, zs, cb, acum, acumT, dtT, mask, extra, ng, head0, heads_per_group, head_dim,
                   carried=None):
    heads_per_slab = V7X_LANES // head_dim
    parts = []
    for q in range(heads_per_group // heads_per_slab):
        cols = slice(q * V7X_LANES, (q + 1) * V7X_LANES)
        rhs = x[:, cols]
        if carried is not None:
            cg, st_t = carried
            rhs = jnp.concatenate([rhs, st_t[:, cols].astype(BF16)], axis=0)
        lanes = lax.broadcasted_iota(jnp.int32, rhs.shape, 1)
        acc = None
        for r in range(heads_per_slab):
            hd = head0 + q * heads_per_slab + r
            a_t = jnp.broadcast_to(acum[:, hd:hd + 1], cb.shape)
            seg = a_t - acumT[hd:hd + 1, :]
            lhs = (cb * jnp.exp(jnp.where(mask, seg, MASKED)) * dtT[hd:hd + 1, :]).astype(BF16)
            if carried is not None:
                lhs = jnp.concatenate([lhs, (cg * jnp.exp(a_t)).astype(BF16)], axis=1)
            in_head = jnp.logical_and(lanes >= r * head_dim, lanes < (r + 1) * head_dim)
            part = _dot(lhs, jnp.where(in_head, rhs, jnp.zeros_like(rhs)))
            acc = part if acc is None else acc + part
        parts.append(acc)
    y = jnp.concatenate(parts, axis=1) + extra
    gated = y * zs
    ms = jnp.mean(gated * gated, axis=-1, keepdims=True)
    return (gated * lax.rsqrt(ms + EPS) * ng).astype(BF16)


def _ssd_prompt_kernel(xs_ref, b_ref, c_ref, zs_ref, dt_ref, dtT_ref, alr_ref, alc_ref, e_ref,
                       d_ref, ng_ref, y_ref, state_ref, st_sc, *, n_groups, head_dim, d_state):
    c = pl.program_id(1)

    @pl.when(c == 0)
    def _():
        st_sc[...] = jnp.zeros_like(st_sc)

    q_rows, d_inner = xs_ref.shape
    gw = d_inner // n_groups
    hpg = gw // head_dim
    row = lax.broadcasted_iota(jnp.int32, (q_rows, q_rows), 0)
    col = lax.broadcasted_iota(jnp.int32, (q_rows, q_rows), 1)
    causal = col <= row
    tril = jnp.where(causal, 1.0, 0.0).astype(BF16)
    triu = jnp.where(row <= col, 1.0, 0.0).astype(BF16)
    dt = dt_ref[...]
    dtT = dtT_ref[...]
    acum = _dot01_lhs(tril, dt * -jnp.exp(alr_ref[...]))
    acumT = _dot01_rhs(dtT * -jnp.exp(alc_ref[...]), triu)
    a_end = acum[q_rows - 1:q_rows, :]
    e = e_ref[...]
    x = xs_ref[...]
    xf = x.astype(F32)
    to_end = (xf * _dot((dt * jnp.exp(a_end - acum)).astype(BF16), e)).astype(BF16)
    decay = _dot01_rhs(jnp.broadcast_to(jnp.exp(a_end), (V7X_SUBLANES, a_end.shape[1])), e)[:1, :]
    skip = xf * d_ref[...]
    for g in range(n_groups):
        sl = slice(g * gw, (g + 1) * gw)
        ns = slice(g * d_state, (g + 1) * d_state)
        bg, cg = b_ref[:, ns], c_ref[:, ns]
        st = st_sc[:, sl]
        y_ref[:, sl] = _ssd_group_out(x[:, sl], zs_ref[:, sl].astype(F32), _dot_nt(cg, bg), acum,
                                      acumT, dtT, causal, skip[:, sl], ng_ref[:, sl],
                                      g * hpg, hpg, head_dim,
                                      carried=(cg.astype(F32), st))
        st_sc[:, sl] = decay[:, sl] * st + _dot_tn(bg, to_end[:, sl])

    @pl.when(c == pl.num_programs(1) - 1)
    def _():
        for g in range(n_groups):
            state_ref[g * gw:(g + 1) * gw, :] = st_sc[:, g * gw:(g + 1) * gw].T


def _ssd_prompt(xbc, zs, dt, dtT, alog_row, alog_col, expand, d_x, ng,
                *, n_seq, seq_len, d_inner, n_groups, head_dim, d_state):
    q = SSD_CHUNK
    assert seq_len % q == 0 and q == d_state
    nc = seq_len // q
    hp = dt.shape[1]
    gn = n_groups * d_state
    assert d_inner % gn == 0
    rows = lambda b, c: b * nc + c
    const = lambda b, c: (0, 0)
    kern = functools.partial(_ssd_prompt_kernel, n_groups=n_groups, head_dim=head_dim, d_state=d_state)
    return pl.pallas_call(
        kern,
        grid=(n_seq, nc),
        in_specs=[
            pl.BlockSpec((q, d_inner), lambda b, c: (rows(b, c), 0)),
            pl.BlockSpec((q, gn), lambda b, c: (rows(b, c), d_inner // gn)),
            pl.BlockSpec((q, gn), lambda b, c: (rows(b, c), d_inner // gn + 1)),
            pl.BlockSpec((q, d_inner), lambda b, c: (rows(b, c), 0)),
            pl.BlockSpec((q, hp), lambda b, c: (rows(b, c), 0)),
            pl.BlockSpec((hp, q), lambda b, c: (0, rows(b, c))),
            pl.BlockSpec((1, hp), const),
            pl.BlockSpec((hp, 1), const),
            pl.BlockSpec((hp, d_inner), const),
            pl.BlockSpec((1, d_inner), const),
            pl.BlockSpec((1, d_inner), const),
        ],
        out_specs=[
            pl.BlockSpec((q, d_inner), lambda b, c: (rows(b, c), 0)),
            pl.BlockSpec((d_inner, d_state), lambda b, c: (b, 0)),
        ],
        out_shape=[
            jax.ShapeDtypeStruct((n_seq * seq_len, d_inner), BF16),
            jax.ShapeDtypeStruct((n_seq * d_inner, d_state), F32),
        ],
        scratch_shapes=[pltpu.VMEM((d_state, d_inner), F32)],
        compiler_params=_params("arbitrary", "arbitrary"),
        name="ssd_scan_prompt",
    )(xbc, xbc, xbc, zs, dt, dtT, alog_row, alog_col, expand, d_x, ng)


def _ssd_sample_kernel(xs_ref, b_ref, c_ref, zs_ref, dt_ref, dtT_ref, alr_ref, alc_ref, e_ref,
                       d_ref, ng_ref, st_ref, y_ref, nst_ref, *, seq_len, head_dim):
    q_rows, gw = xs_ref.shape
    n_seq = q_rows // seq_len
    row = lax.broadcasted_iota(jnp.int32, (q_rows, q_rows), 0)
    col = lax.broadcasted_iota(jnp.int32, (q_rows, q_rows), 1)
    same = (row // seq_len) == (col // seq_len)
    mask = jnp.logical_and(same, col <= row)
    tril = jnp.where(mask, 1.0, 0.0).astype(BF16)
    triu = jnp.where(jnp.logical_and(same, row <= col), 1.0, 0.0).astype(BF16)
    ends = jnp.where(col == (row // seq_len) * seq_len + (seq_len - 1), 1.0, 0.0).astype(BF16)
    dt = dt_ref[...]
    dtT = dtT_ref[...]
    acum = _dot01_lhs(tril, dt * -jnp.exp(alr_ref[...]))
    acumT = _dot01_rhs(dtT * -jnp.exp(alc_ref[...]), triu)
    a_end = _dot01_lhs(ends, acum)
    e = e_ref[...]
    x = xs_ref[...].astype(F32)
    to_endT = (x * _dot((dt * jnp.exp(a_end - acum)).astype(BF16), e)).T.astype(BF16)
    decayT = _dot01_rhs(jnp.exp(a_end), e).T
    from_start = _spread(jnp.exp(acum), e)
    bg = b_ref[...].astype(F32)
    cg = c_ref[...].astype(F32)
    seq_of_row = lax.broadcasted_iota(jnp.int32, bg.shape, 0) // seq_len
    inter = jnp.zeros((q_rows, gw), F32)
    for s in range(n_seq):
        st = st_ref[s]
        mine = seq_of_row == s
        inter = inter + _dot_nt(jnp.where(mine, cg, 0.0).astype(BF16), st.astype(BF16))
        bm = jnp.where(mine, bg, 0.0).astype(BF16)
        nst_ref[s] = decayT[:, s * seq_len:s * seq_len + 1] * st + _dot(to_endT, bm)
    inter = from_start * inter + x * d_ref[...]
    y_ref[...] = _ssd_group_out(xs_ref[...], zs_ref[...].astype(F32), _dot_nt(c_ref[...], b_ref[...]),
                                acum, acumT, dtT, mask, inter, ng_ref[...], 0, gw // head_dim, head_dim)


def _ssd_sample(xbc, zs, dt_g, dtT_g, alog_row_g, alog_col_g, expand, d_x, ng, state,
                *, row0, n_rows, seq_len, d_inner, n_groups, head_dim, d_state):
    q = SSD_CHUNK
    assert n_rows % q == 0 and q % seq_len == 0 and row0 % q == 0
    nb = n_rows // q
    spb = q // seq_len
    gw = d_inner // n_groups
    hp = dt_g.shape[2]
    rb0 = row0 // q
    b_col0 = d_inner // d_state
    kern = functools.partial(_ssd_sample_kernel, seq_len=seq_len, head_dim=head_dim)
    return pl.pallas_call(
        kern,
        grid=(nb, n_groups),
        in_specs=[
            pl.BlockSpec((q, gw), lambda s, g: (rb0 + s, g)),
            pl.BlockSpec((q, d_state), lambda s, g: (rb0 + s, b_col0 + g)),
            pl.BlockSpec((q, d_state), lambda s, g: (rb0 + s, b_col0 + n_groups + g)),
            pl.BlockSpec((q, gw), lambda s, g: (rb0 + s, g)),
            pl.BlockSpec((None, q, hp), lambda s, g: (g, s, 0)),
            pl.BlockSpec((None, dtT_g.shape[1], q), lambda s, g: (g, 0, s)),
            pl.BlockSpec((None, 1, hp), lambda s, g: (g, 0, 0)),
            pl.BlockSpec((None, dtT_g.shape[1], 1), lambda s, g: (g, 0, 0)),
            pl.BlockSpec((hp, gw), lambda s, g: (0, 0)),
            pl.BlockSpec((1, gw), lambda s, g: (0, g)),
            pl.BlockSpec((1, gw), lambda s, g: (0, g)),
            pl.BlockSpec((spb, None, gw, d_state), lambda s, g: (s, g, 0, 0)),
        ],
        out_specs=[
            pl.BlockSpec((q, gw), lambda s, g: (s, g)),
            pl.BlockSpec((spb, None, gw, d_state), lambda s, g: (s, g, 0, 0)),
        ],
        out_shape=[
            jax.ShapeDtypeStruct((n_rows, d_inner), BF16),
            jax.ShapeDtypeStruct(state.shape, F32),
        ],
        compiler_params=_params("arbitrary", "arbitrary"),
        name="ssd_scan_sample",
    )(xbc, xbc, xbc, zs, dt_g, dtT_g, alog_row_g, alog_col_g, expand, d_x, ng, state)


def _prompt_conv_state(tail, *, n_prompt_tiles, tiles_per_seq, km1):
    t = tail.reshape(-1, V7X_SUBLANES, tail.shape[1])[:n_prompt_tiles]
    return t[tiles_per_seq - 1::tiles_per_seq, V7X_SUBLANES - km1:, :]


def kernel(x_prompt, x_sample, p_prompt, p_sample, state_sc_conv, state_ssd_conv, state_ssd, g_mix, g_ffn, g_ple, g_final, sc_w_in, sc_w_conv, sc_w_out, ssd_w_in, ssd_conv_w, ssd_conv_b, ssd_dt_bias, ssd_a_log, ssd_d, ssd_norm_g, ssd_w_out, ffn_w_gate, ffn_w_up, ffn_w_down, ple_w_proj, ple_w_gate):
    bp, lp, d = x_prompt.shape
    bs, ls, _ = x_sample.shape
    depth = g_mix.shape[0]
    mp, ms = bp * lp, bs * ls
    pdim = p_prompt.shape[-1]
    n_heads, head_dim, d_state = state_ssd.shape[2:]
    d_inner = n_heads * head_dim
    conv_dim = ssd_conv_w.shape[-1]
    n_groups = (conv_dim - d_inner) // (2 * d_state)
    hpg = n_heads // n_groups
    assert n_heads <= V7X_LANES and V7X_LANES % head_dim == 0 and d_state == V7X_LANES
    npt = mp // ROW_TILE
    tps = lp // ROW_TILE
    row = lambda v: v.reshape(1, -1)
    pp = p_prompt.reshape(depth, mp, pdim)
    ps = p_sample.reshape(depth, ms, pdim)

    h = (x_prompt.reshape(mp, d), x_sample.reshape(ms, d))
    sc_p, sc_s, cv_p, cv_s, st_p, st_s = [], [], [], [], [], []
    y_out = None
    for i in range(depth):
        j = i // 2
        if i % 2 == 0:
            km1 = sc_w_conv.shape[1] - 1
            if not isinstance(h, tuple):
                h = (h[:mp], h[mp:])
            gated, tail, nstate = _short_conv_in(
                *h, row(g_mix[i]), sc_w_in, j, sc_w_conv, jnp.swapaxes(state_sc_conv, 1, 2),
                seq_len_p=lp, seq_len_s=ls)
            sc_p.append(_prompt_conv_state(tail, n_prompt_tiles=npt, tiles_per_seq=tps, km1=km1))
            sc_s.append(jnp.swapaxes(nstate, 0, 1))
            h = _matmul_residual(h, gated, sc_w_out, j, n_prompt_rows=mp)
        else:
            if isinstance(h, tuple):
                h = jnp.concatenate(h)
            km1 = ssd_conv_w.shape[1] - 1
            zx = d_inner + conv_dim
            pad_h = V7X_LANES - n_heads
            w_in_t = jnp.swapaxes(ssd_w_in, 1, 2)
            w_dt_t = jnp.pad(w_in_t[j, zx:, :], ((0, pad_h), (0, 0))).astype(BF16)
            dt_b = jnp.pad(ssd_dt_bias[j], (0, pad_h))
            alog = jnp.pad(ssd_a_log[j], (0, pad_h))
            zs, xbc, tail, nstate, dt, dtT = _ssd_in(
                h, row(g_mix[i]), w_in_t, j, w_dt_t, row(dt_b),
                ssd_conv_w, ssd_conv_b.reshape(ssd_conv_b.shape[0], 1, conv_dim),
                jnp.swapaxes(state_ssd_conv, 1, 2),
                n_prompt_rows=mp, seq_len_p=lp, seq_len_s=ls, d_inner=d_inner)
            cv_p.append(_prompt_conv_state(tail, n_prompt_tiles=npt, tiles_per_seq=tps, km1=km1))
            cv_s.append(jnp.swapaxes(nstate, 0, 1))
            head_of_lane = jnp.arange(d_inner, dtype=jnp.int32) // head_dim
            expand = (jnp.arange(V7X_LANES, dtype=jnp.int32)[:, None] == head_of_lane[None, :]).astype(BF16)
            d_x = row(jnp.repeat(ssd_d[j], head_dim))
            ng = row(ssd_norm_g[j])
            geom = dict(d_inner=d_inner, n_groups=n_groups, head_dim=head_dim, d_state=d_state)
            y_p, new_p = _ssd_prompt(xbc, zs, dt, dtT, row(alog), alog.reshape(-1, 1), expand, d_x, ng,
                                     n_seq=bp, seq_len=lp, **geom)
            dt_s = dt[mp:]
            dt_g = jnp.stack([jnp.roll(dt_s, -g * hpg, axis=1) for g in range(n_groups)])
            alog_g = jnp.stack([jnp.roll(alog, -g * hpg) for g in range(n_groups)])
            dtT_g = dtT[:n_heads, mp:].reshape(n_groups, hpg, ms)
            y_s, new_s = _ssd_sample(
                xbc, zs, dt_g, dtT_g, alog_g.reshape(n_groups, 1, -1),
                ssd_a_log[j].reshape(n_groups, hpg, 1), expand[:, :d_inner // n_groups], d_x, ng,
                state_ssd[j].reshape(bs, n_groups, hpg * head_dim, d_state),
                row0=mp, n_rows=ms, seq_len=ls, **geom)
            st_p.append(new_p.reshape(bp, n_heads, head_dim, d_state))
            st_s.append(new_s.reshape(bs, n_heads, head_dim, d_state))
            h = _matmul_residual(h, (y_p, y_s), ssd_w_out, j, n_prompt_rows=mp)
        h = _ffn(h, row(g_ffn[i]), ffn_w_gate, ffn_w_up, ffn_w_down, i)
        ple_args = (h, pp, ps, row(g_ple[i]), ple_w_gate, ple_w_proj, i)
        if i == depth - 1:
            y_out = _ple(*ple_args, g_final=row(g_final))
        else:
            h = _ple(*ple_args)
    y_p, y_s = y_out
    return (y_p.reshape(bp, lp, d), y_s.reshape(bs, ls, d), jnp.stack(sc_p), jnp.stack(sc_s),
            jnp.stack(cv_p), jnp.stack(cv_s), jnp.stack(st_p), jnp.stack(st_s))
```

```python
import functools

import jax
import jax.numpy as jnp
from jax import lax
from jax.experimental import pallas as pl
from jax.experimental.pallas import tpu as pltpu

F32 = jnp.float32
BF16 = jnp.bfloat16
EPS = 1e-6
MASKED = -1e30
V7X_LANES = 128
V7X_SUBLANES = 8
HISTORY_ROWS = 16
ROW_PARTS = 1
V7X_VMEM_LIMIT = 56 * 1024 * 1024

ROW_TILE = 1024
HALF_ROW_TILE = 512
PLE_ROW_TILE = 256
COL_TILE = 512
NARROW_COL_TILE = 256
SSD_CHUNK = 128
WEIGHT_SLAB_BYTES = 16 * 1024 * 1024


def _params(*sem):
    return pltpu.CompilerParams(dimension_semantics=sem, vmem_limit_bytes=V7X_VMEM_LIMIT)


def _dot(a, b):
    return jnp.dot(a, b, preferred_element_type=F32)


def _dot_nt(a, b):
    return lax.dot_general(a, b, (((1,), (1,)), ((), ())), preferred_element_type=F32)


def _dot_tn(a, b):
    return lax.dot_general(a, b, (((0,), (0,)), ((), ())), preferred_element_type=F32)


def _split3(a):
    a1 = a.astype(BF16)
    r1 = a - a1.astype(F32)
    a2 = r1.astype(BF16)
    a3 = (r1 - a2.astype(F32)).astype(BF16)
    return a3, a2, a1


def _dot01_rhs(a, e):
    p3, p2, p1 = _split3(a)
    return (_dot(p3, e) + _dot(p2, e)) + _dot(p1, e)


def _spread(a, e):
    hi = a.astype(BF16)
    lo = (a - hi.astype(F32)).astype(BF16)
    return _dot(lo, e) + _dot(hi, e)


def _dot01_lhs(t, a):
    p3, p2, p1 = _split3(a)
    return (_dot(t, p3) + _dot(t, p2)) + _dot(t, p1)


def _rmsnorm(x, g):
    ms = jnp.mean(x * x, axis=-1, keepdims=True)
    return x * lax.rsqrt(ms + EPS) * g


def _softplus(x):
    return jnp.maximum(x, 0.0) + jnp.log1p(jnp.exp(-jnp.abs(x)))


def _silu(x):
    return x * jax.nn.sigmoid(x)


def _layer_spec(block, layer, imap):
    return pl.BlockSpec((None,) + tuple(block), lambda *a: (layer,) + tuple(imap(*a)))


def _conv(hist_ref, taps, seq_len=None, row0=0, n_rows=None):
    k = taps.shape[0]
    if n_rows is None:
        n_rows = hist_ref.shape[0] - HISTORY_ROWS
    out = taps[k - 1:k, :] * hist_ref[pl.ds(HISTORY_ROWS + row0, n_rows), :]
    if seq_len is not None:
        assert row0 % seq_len == 0
        t = lax.broadcasted_iota(jnp.int32, out.shape, 0) % seq_len
    for d in range(1, k):
        sh = hist_ref[pl.ds(HISTORY_ROWS + row0 - d, n_rows), :]
        if seq_len is not None:
            sh = jnp.where(t >= d, sh, 0.0)
        out = out + taps[k - 1 - d:k - d, :] * sh
    return out


def _state_correction(buf_ref, taps, stage_sc, seq_len):
    k = taps.shape[0]
    km1 = k - 1
    n_seq = stage_sc.shape[1] // seq_len
    stage_sc[...] = jnp.zeros_like(stage_sc)
    rows = [buf_ref[r] for r in range(km1)]
    for t in range(km1):
        acc = None
        for d in range(t + 1, k):
            term = taps[k - 1 - d:k - d, :] * rows[km1 + t - d]
            acc = term if acc is None else acc + term
        for c in range(stage_sc.shape[0]):
            stage_sc[c, pl.ds(t, n_seq, stride=seq_len), :] = acc[:, c * V7X_LANES:(c + 1) * V7X_LANES]


def _staged(stage_sc):
    return jnp.concatenate([stage_sc[c] for c in range(stage_sc.shape[0])], axis=1)


def _emit_sample_state(u, stage_sc, nstate_ref, seq_len):
    km1 = nstate_ref.shape[0]
    n_chunks = stage_sc.shape[0]
    n_seq = stage_sc.shape[1] // seq_len
    for c in range(n_chunks):
        stage_sc[c] = u[:, c * V7X_LANES:(c + 1) * V7X_LANES]
    for r in range(km1):
        nstate_ref[r] = jnp.concatenate(
            [stage_sc[c, pl.ds(seq_len - km1 + r, n_seq, stride=seq_len), :] for c in range(n_chunks)],
            axis=1)


def _sc_in_kernel(xp_ref, xs_ref, g_ref, wb_ref, wc_ref, wv_ref, taps_ref, buf_ref,
                  gated_ref, tail_ref, nstate_ref, hn_sc, carry_sc, hist_sc, stage_sc,
                  *, n_prompt_tiles, tiles_per_seq, sample_len):
    i = pl.program_id(0)
    j = pl.program_id(1)

    @pl.when(j == 0)
    def _():
        x = jnp.where(i < n_prompt_tiles, xp_ref[...], xs_ref[...])
        hn_sc[...] = _rmsnorm(x, g_ref[...]).astype(BF16)

    def project():
        hn = hn_sc[...]
        bg = _dot(hn, wb_ref[...].astype(BF16))
        u = _dot(hn, wc_ref[...].astype(BF16)) * _dot(hn, wv_ref[...].astype(BF16))
        return bg, u

    @pl.when(i < n_prompt_tiles)
    def _():
        bg, u = project()
        hist_sc[:HISTORY_ROWS, :] = jnp.where(i % tiles_per_seq == 0, 0.0, carry_sc[j])
        hist_sc[HISTORY_ROWS:, :] = u
        gated_ref[...] = (bg * _conv(hist_sc, taps_ref[...])).astype(BF16)
        carry_sc[j] = u[u.shape[0] - HISTORY_ROWS:, :]
        tail_ref[...] = u[u.shape[0] - V7X_SUBLANES:, :]

    @pl.when(i >= n_prompt_tiles)
    def _():
        taps = taps_ref[...]
        _state_correction(buf_ref, taps, stage_sc, sample_len)
        bg, u = project()
        hist_sc[:HISTORY_ROWS, :] = jnp.zeros_like(carry_sc[j])
        hist_sc[HISTORY_ROWS:, :] = u
        conv = _conv(hist_sc, taps, sample_len) + _staged(stage_sc)
        gated_ref[...] = (bg * conv).astype(BF16)
        tail_ref[...] = u[u.shape[0] - V7X_SUBLANES:, :]
        _emit_sample_state(u, stage_sc, nstate_ref, sample_len)


def _short_conv_in(xp, xs, g, w_in, layer, taps, buf, *, seq_len_p, seq_len_s):
    mp, d = xp.shape
    ms = xs.shape[0]
    m = mp + ms
    k = taps.shape[1]
    tm, tn = ROW_TILE, NARROW_COL_TILE
    assert mp % tm == 0 and ms == tm and seq_len_p % tm == 0 and tm % seq_len_s == 0
    assert d % tn == 0 and tn % V7X_LANES == 0 and k - 1 <= min(seq_len_s, V7X_SUBLANES)
    npt, nj = mp // tm, d // tn
    n_tiles = npt + 1
    n_seq_s = ms // seq_len_s
    samp_col = lambda i, j: jnp.where(i >= npt, j, 0)
    kern = functools.partial(_sc_in_kernel, n_prompt_tiles=npt, tiles_per_seq=seq_len_p // tm,
                             sample_len=seq_len_s)
    return pl.pallas_call(
        kern,
        grid=(n_tiles, nj),
        in_specs=[
            pl.BlockSpec((tm, d), lambda i, j: (jnp.minimum(i, npt - 1), 0)),
            pl.BlockSpec((tm, d), lambda i, j: (0, 0), pipeline_mode=pl.Buffered(1)),
            pl.BlockSpec((1, d), lambda i, j: (0, 0)),
            _layer_spec((d, tn), layer, lambda i, j: (0, j)),
            _layer_spec((d, tn), layer, lambda i, j: (0, nj + j)),
            _layer_spec((d, tn), layer, lambda i, j: (0, 2 * nj + j)),
            _layer_spec((k, tn), layer, lambda i, j: (0, j)),
            _layer_spec((k - 1, n_seq_s, tn), layer, lambda i, j: (0, 0, samp_col(i, j))),
        ],
        out_specs=[
            pl.BlockSpec((tm, tn), lambda i, j: (i, j)),
            pl.BlockSpec((V7X_SUBLANES, tn), lambda i, j: (i, j)),
            pl.BlockSpec((k - 1, n_seq_s, tn), lambda i, j: (0, 0, samp_col(i, j))),
        ],
        out_shape=[
            jax.ShapeDtypeStruct((m, d), BF16),
            jax.ShapeDtypeStruct((n_tiles * V7X_SUBLANES, d), F32),
            jax.ShapeDtypeStruct((k - 1, n_seq_s, d), F32),
        ],
        scratch_shapes=[pltpu.VMEM((tm, d), BF16), pltpu.VMEM((nj, HISTORY_ROWS, tn), F32),
                        pltpu.VMEM((HISTORY_ROWS + tm, tn), F32),
                        pltpu.VMEM((tn // V7X_LANES, tm, V7X_LANES), F32)],
        compiler_params=_params("arbitrary", "arbitrary"),
        name="short_conv_in",
    )(xp, xs, g, w_in, w_in, w_in, taps, buf)


def _res_kernel(*refs, n_prompt_tiles, res_pair, a_pair):
    refs = list(refs)
    res_refs = [refs.pop(0) for _ in range(2 if res_pair else 1)]
    a_refs = [refs.pop(0) for _ in range(2 if a_pair else 1)]
    w_ref, out_ref, wb_sc = refs
    i = pl.program_id(1)

    @pl.when(i == 0)
    def _():
        wb_sc[...] = w_ref[...].astype(BF16)

    def body(which):
        out_ref[...] = res_refs[which * res_pair][...] + _dot(a_refs[which * a_pair][...], wb_sc[...])

    if not (res_pair or a_pair):
        body(0)
        return
    pl.when(i < n_prompt_tiles)(lambda: body(0))
    pl.when(i >= n_prompt_tiles)(lambda: body(1))


def _matmul_residual(res, a, w, layer, *, n_prompt_rows):
    res_pair, a_pair = isinstance(res, tuple), isinstance(a, tuple)
    _, k, n = w.shape
    m = sum(r.shape[0] for r in res) if res_pair else res.shape[0]
    tm = HALF_ROW_TILE
    tn = min(n, WEIGHT_SLAB_BYTES // (4 * k))
    assert n_prompt_rows % tm == 0 and m % tm == 0 and n % tn == 0 and tn % V7X_LANES == 0
    npt = n_prompt_rows // tm

    def row_specs(pair, block, col_of):
        if not pair:
            return [pl.BlockSpec(block, lambda j, i: (i, col_of(j)))]
        return [pl.BlockSpec(block, lambda j, i: (jnp.minimum(i, npt - 1), col_of(j))),
                pl.BlockSpec(block, lambda j, i: (jnp.maximum(i - npt, 0), col_of(j)))]

    return pl.pallas_call(
        functools.partial(_res_kernel, n_prompt_tiles=npt, res_pair=res_pair, a_pair=a_pair),
        grid=(n // tn, m // tm),
        in_specs=row_specs(res_pair, (tm, tn), lambda j: j) + row_specs(a_pair, (tm, k), lambda j: 0)
        + [pl.BlockSpec((None, k, tn), lambda j, i: (layer, 0, j), pipeline_mode=pl.Buffered(1))],
        out_specs=pl.BlockSpec((tm, tn), lambda j, i: (i, j)),
        out_shape=jax.ShapeDtypeStruct((m, n), F32),
        scratch_shapes=[pltpu.VMEM((k, tn), BF16)],
        compiler_params=_params("arbitrary", "arbitrary"),
        name="matmul_residual",
    )(*(res if res_pair else (res,)), *(a if a_pair else (a,)), w)


def _ffn_kernel(h_ref, g_ref, wg_ref, wu_ref, wd_ref, out_ref, *rest):
    *cast_refs, hn_sc = rest

    @pl.when(pl.program_id(1) == 0)
    def _():
        h = h_ref[...]
        hn_sc[...] = _rmsnorm(h, g_ref[...]).astype(BF16)
        out_ref[...] = h

    wg, wu, wd = wg_ref[...], wu_ref[...], wd_ref[...]
    if cast_refs:
        wg, wu, wd = wg.astype(BF16), wu.astype(BF16), wd.astype(BF16)
        for ref, w in zip(cast_refs, (wg, wu, wd)):
            ref[...] = w
    hn = hn_sc[...]
    gate = _dot(hn, wg)
    act = (_silu(gate) * _dot(hn, wu)).astype(BF16)
    out_ref[...] += _dot(act, wd)


def _ffn(h, g, w_gate, w_up, w_down, layer):
    m, d = h.shape
    f = w_gate.shape[2]
    tm = ROW_TILE
    assert m % tm == 0 and m > tm and f % COL_TILE == 0 and f % NARROW_COL_TILE == 0

    def call(row0_tile, n_row_tiles, tf, weights, weight_specs, cast_outputs, name):
        out_specs = [pl.BlockSpec((tm, d), lambda i, j: (i, 0))]
        out_shape = [jax.ShapeDtypeStruct((n_row_tiles * tm, d), F32)]
        if cast_outputs:
            out_specs += [pl.BlockSpec((d, tf), lambda i, j: (0, j)), pl.BlockSpec((d, tf), lambda i, j: (0, j)),
                          pl.BlockSpec((tf, d), lambda i, j: (j, 0))]
            out_shape += [jax.ShapeDtypeStruct((d, f), BF16), jax.ShapeDtypeStruct((d, f), BF16),
                          jax.ShapeDtypeStruct((f, d), BF16)]
        h_mode = dict(pipeline_mode=pl.Buffered(1)) if n_row_tiles == 1 else {}
        return pl.pallas_call(
            _ffn_kernel,
            grid=(n_row_tiles, f // tf),
            in_specs=[pl.BlockSpec((tm, d), lambda i, j: (i + row0_tile, 0), **h_mode),
                      pl.BlockSpec((1, d), lambda i, j: (0, 0))] + weight_specs(tf),
            out_specs=out_specs,
            out_shape=out_shape,
            scratch_shapes=[pltpu.VMEM((tm, d), BF16)],
            compiler_params=_params("arbitrary", "arbitrary"),
            name=name,
        )(h, g, *weights)

    f32_specs = lambda tf: [_layer_spec((d, tf), layer, lambda i, j: (0, j)),
                            _layer_spec((d, tf), layer, lambda i, j: (0, j)),
                            _layer_spec((tf, d), layer, lambda i, j: (j, 0))]
    bf16_specs = lambda tf: [pl.BlockSpec((d, tf), lambda i, j: (0, j)), pl.BlockSpec((d, tf), lambda i, j: (0, j)),
                             pl.BlockSpec((tf, d), lambda i, j: (j, 0))]
    first, wg_b, wu_b, wd_b = call(0, 1, NARROW_COL_TILE, (w_gate, w_up, w_down), f32_specs, True,
                                   "swiglu_ffn_first")
    rest, = call(1, m // tm - 1, COL_TILE, (wg_b, wu_b, wd_b), bf16_specs, False, "swiglu_ffn_rest")
    return first, rest


def _ple_kernel(ha_ref, hb_ref, pp_ref, ps_ref, g_ref, wg_ref, wp_ref, *rest,
                n_first_tiles, n_prompt_tiles, final):
    i = pl.program_id(0)
    *rest, wgb_sc, wpb_sc = rest

    @pl.when(i == 0)
    def _():
        wgb_sc[...] = wg_ref[...].astype(BF16)
        wpb_sc[...] = wp_ref[...].astype(BF16)

    h = jnp.where(i < n_first_tiles, ha_ref[...], hb_ref[...])
    gate = jax.nn.sigmoid(_dot(_rmsnorm(h, g_ref[...]).astype(BF16), wgb_sc[...]))
    p = jnp.where(i < n_prompt_tiles, pp_ref[...], ps_ref[...]).astype(BF16)
    out = h + _dot(p, wpb_sc[...]) * gate
    if not final:
        rest[0][...] = out
        return
    gf_ref, yp_ref, ys_ref = rest
    y = _rmsnorm(out, gf_ref[...])

    @pl.when(i < n_prompt_tiles)
    def _():
        yp_ref[...] = y

    @pl.when(i >= n_prompt_tiles)
    def _():
        ys_ref[...] = y


def _ple(h, pp, ps, g, w_gate, w_proj, layer, g_final=None):
    ha, hb = h
    d = ha.shape[1]
    m = ha.shape[0] + hb.shape[0]
    _, mp, pdim = pp.shape
    tm = PLE_ROW_TILE
    assert m % tm == 0 and mp % tm == 0 and ha.shape[0] % tm == 0
    npt = mp // tm
    nft = ha.shape[0] // tm
    final = g_final is not None
    const = lambda i: (0, 0)
    resident = dict(pipeline_mode=pl.Buffered(1))
    in_specs = [
        pl.BlockSpec((tm, d), lambda i: (jnp.minimum(i, nft - 1), 0)),
        pl.BlockSpec((tm, d), lambda i: (jnp.maximum(i - nft, 0), 0)),
        _layer_spec((tm, pdim), layer, lambda i: (jnp.minimum(i, npt - 1), 0)),
        _layer_spec((tm, pdim), layer, lambda i: (jnp.maximum(i - npt, 0), 0)),
        pl.BlockSpec((1, d), const),
        pl.BlockSpec((None, d, d), lambda i: (layer, 0, 0), **resident),
        pl.BlockSpec((None, pdim, d), lambda i: (layer, 0, 0), **resident),
    ]
    args = [ha, hb, pp, ps, g, w_gate, w_proj]
    if final:
        in_specs.append(pl.BlockSpec((1, d), const))
        args.append(g_final)
        out_specs = [pl.BlockSpec((tm, d), lambda i: (jnp.minimum(i, npt - 1), 0)),
                     pl.BlockSpec((tm, d), lambda i: (jnp.maximum(i - npt, 0), 0))]
        out_shape = [jax.ShapeDtypeStruct((mp, d), F32), jax.ShapeDtypeStruct((m - mp, d), F32)]
    else:
        out_specs = pl.BlockSpec((tm, d), lambda i: (i, 0))
        out_shape = jax.ShapeDtypeStruct((m, d), F32)
    return pl.pallas_call(
        functools.partial(_ple_kernel, n_first_tiles=nft, n_prompt_tiles=npt, final=final),
        grid=(m // tm,),
        in_specs=in_specs,
        out_specs=out_specs,
        out_shape=out_shape,
        scratch_shapes=[pltpu.VMEM((d, d), BF16), pltpu.VMEM((pdim, d), BF16)],
        compiler_params=_params("arbitrary"),
        name="ple_final" if final else "ple",
    )(*args)


def _ssd_in_kernel(h_ref, g_ref, wt_ref, wdtT_ref, dtb_ref, taps_ref, cb_ref, buf_ref,
                   zs_ref, xbc_ref, tail_ref, nstate_ref, dt_ref, dtT_ref,
                   hn_sc, carry_sc, hist_sc, stage_sc,
                   *, n_prompt_tiles, tiles_per_seq, sample_len, n_z_tiles):
    i = pl.program_id(0)
    j = pl.program_id(1)
    jc = j - n_z_tiles
    tm = hn_sc.shape[0]
    rows_per_part = tm // ROW_PARTS

    @pl.when(j == 0)
    def _():
        hn = _rmsnorm(h_ref[...], g_ref[...]).astype(BF16)
        hn_sc[...] = hn
        dt = _softplus(_dot_nt(hn, wdtT_ref[...]) + dtb_ref[...])
        dt_ref[...] = dt
        dtT_ref[...] = dt.T

    def finish(conv):
        return _silu(conv + cb_ref[...]).astype(BF16)

    @pl.when(j < n_z_tiles)
    def _():
        wb = wt_ref[...].astype(BF16)
        for r in range(ROW_PARTS):
            rows = pl.ds(r * rows_per_part, rows_per_part)
            zs_ref[rows, :] = _silu(_dot_nt(hn_sc[rows, :], wb)).astype(BF16)

    def project_into_hist():
        wb = wt_ref[...].astype(BF16)
        for r in range(ROW_PARTS):
            rows = pl.ds(r * rows_per_part, rows_per_part)
            hist_sc[pl.ds(HISTORY_ROWS + r * rows_per_part, rows_per_part), :] = _dot_nt(hn_sc[rows, :], wb)

    @pl.when(jnp.logical_and(j >= n_z_tiles, i < n_prompt_tiles))
    def _():
        hist_sc[:HISTORY_ROWS, :] = jnp.where(i % tiles_per_seq == 0, 0.0, carry_sc[jc])
        project_into_hist()
        taps = taps_ref[...]
        for r in range(ROW_PARTS):
            xbc_ref[pl.ds(r * rows_per_part, rows_per_part), :] = finish(
                _conv(hist_sc, taps, None, r * rows_per_part, rows_per_part))
        carry_sc[jc] = hist_sc[tm:, :]
        tail_ref[...] = hist_sc[HISTORY_ROWS + tm - V7X_SUBLANES:, :]

    @pl.when(jnp.logical_and(j >= n_z_tiles, i >= n_prompt_tiles))
    def _():
        taps = taps_ref[...]
        _state_correction(buf_ref, taps, stage_sc, sample_len)
        hist_sc[:HISTORY_ROWS, :] = jnp.zeros_like(carry_sc[jc])
        project_into_hist()
        xbc_ref[...] = finish(_conv(hist_sc, taps, sample_len) + _staged(stage_sc))
        u = hist_sc[HISTORY_ROWS:, :]
        tail_ref[...] = u[tm - V7X_SUBLANES:, :]
        _emit_sample_state(u, stage_sc, nstate_ref, sample_len)


def _ssd_in(h, g, w_in_t, layer, w_dt_t, dt_b, taps, conv_b, buf,
            *, n_prompt_rows, seq_len_p, seq_len_s, d_inner):
    m, d = h.shape
    _, k, conv_dim = taps.shape
    hp = w_dt_t.shape[0]
    tm, tn = ROW_TILE, COL_TILE
    ms = m - n_prompt_rows
    assert n_prompt_rows % tm == 0 and ms == tm and seq_len_p % tm == 0 and tm % seq_len_s == 0
    assert d_inner % tn == 0 and conv_dim % tn == 0 and k - 1 <= min(seq_len_s, V7X_SUBLANES)
    npt = n_prompt_rows // tm
    n_tiles = npt + 1
    n_seq_s = ms // seq_len_s
    nz, nc = d_inner // tn, conv_dim // tn
    assert tm % (ROW_PARTS * seq_len_s) == 0
    ce = lambda i, j: jnp.maximum(j - nz, 0)
    samp_col = lambda i, j: jnp.where(i >= npt, ce(i, j), 0)
    const = lambda i, j: (0, 0)
    kern = functools.partial(_ssd_in_kernel, n_prompt_tiles=npt, tiles_per_seq=seq_len_p // tm,
                             sample_len=seq_len_s, n_z_tiles=nz)
    return pl.pallas_call(
        kern,
        grid=(n_tiles, nz + nc),
        in_specs=[
            pl.BlockSpec((tm, d), lambda i, j: (i, 0)),
            pl.BlockSpec((1, d), const),
            _layer_spec((tn, d), layer, lambda i, j: (j, 0)),
            pl.BlockSpec((hp, d), const),
            pl.BlockSpec((1, hp), const),
            _layer_spec((k, tn), layer, lambda i, j: (0, ce(i, j))),
            _layer_spec((1, tn), layer, lambda i, j: (0, ce(i, j))),
            _layer_spec((k - 1, n_seq_s, tn), layer, lambda i, j: (0, 0, samp_col(i, j))),
        ],
        out_specs=[
            pl.BlockSpec((tm, tn), lambda i, j: (i, jnp.minimum(j, nz - 1))),
            pl.BlockSpec((tm, tn), lambda i, j: (i, ce(i, j))),
            pl.BlockSpec((V7X_SUBLANES, tn), lambda i, j: (i, ce(i, j))),
            pl.BlockSpec((k - 1, n_seq_s, tn), lambda i, j: (0, 0, samp_col(i, j))),
            pl.BlockSpec((tm, hp), lambda i, j: (i, 0)),
            pl.BlockSpec((hp, tm), lambda i, j: (0, i)),
        ],
        out_shape=[
            jax.ShapeDtypeStruct((m, d_inner), BF16),
            jax.ShapeDtypeStruct((m, conv_dim), BF16),
            jax.ShapeDtypeStruct((n_tiles * V7X_SUBLANES, conv_dim), F32),
            jax.ShapeDtypeStruct((k - 1, n_seq_s, conv_dim), F32),
            jax.ShapeDtypeStruct((m, hp), F32),
            jax.ShapeDtypeStruct((hp, m), F32),
        ],
        scratch_shapes=[pltpu.VMEM((tm, d), BF16), pltpu.VMEM((nc, HISTORY_ROWS, tn), F32),
                        pltpu.VMEM((HISTORY_ROWS + tm, tn), F32),
                        pltpu.VMEM((tn // V7X_LANES, tm, V7X_LANES), F32)],
        compiler_params=_params("arbitrary", "arbitrary"),
        name="ssd_in",
    )(h, g, w_in_t, w_dt_t, dt_b, taps, conv_b, buf)


def _ssd_group_out(x, zs, cb, acum, acumT, dtT, mask, extra, ng, head0, heads_per_group, head_dim,
                   carried=None):
    heads_per_slab = V7X_LANES // head_dim
    parts = []
    for q in range(heads_per_group // heads_per_slab):
        cols = slice(q * V7X_LANES, (q + 1) * V7X_LANES)
        rhs = x[:, cols]
        if carried is not None:
            cg, st_t = carried
            rhs = jnp.concatenate([rhs, st_t[:, cols].astype(BF16)], axis=0)
        lanes = lax.broadcasted_iota(jnp.int32, rhs.shape, 1)
        acc = None
        for r in range(heads_per_slab):
            hd = head0 + q * heads_per_slab + r
            a_t = jnp.broadcast_to(acum[:, hd:hd + 1], cb.shape)
            seg = a_t - acumT[hd:hd + 1, :]
            lhs = (cb * jnp.exp(jnp.where(mask, seg, MASKED)) * dtT[hd:hd + 1, :]).astype(BF16)
            if carried is not None:
                lhs = jnp.concatenate([lhs, (cg * jnp.exp(a_t)).astype(BF16)], axis=1)
            in_head = jnp.logical_and(lanes >= r * head_dim, lanes < (r + 1) * head_dim)
            part = _dot(lhs, jnp.where(in_head, rhs, jnp.zeros_like(rhs)))
            acc = part if acc is None else acc + part
        parts.append(acc)
    y = jnp.concatenate(parts, axis=1) + extra
    gated = y * zs
    ms = jnp.mean(gated * gated, axis=-1, keepdims=True)
    return (gated * lax.rsqrt(ms + EPS) * ng).astype(BF16)


def _ssd_prompt_body(c, n_chunks, xs_ref, b_ref, c_ref, zs_ref, dt_ref, dtT_ref, alr_ref, alc_ref,
                     e_ref, d_ref, ng_ref, y_ref, state_ref, st_sc, *, n_groups, head_dim, d_state):
    @pl.when(c == 0)
    def _():
        st_sc[...] = jnp.zeros_like(st_sc)

    q_rows, d_inner = xs_ref.shape
    gw = d_inner // n_groups
    hpg = gw // head_dim
    row = lax.broadcasted_iota(jnp.int32, (q_rows, q_rows), 0)
    col = lax.broadcasted_iota(jnp.int32, (q_rows, q_rows), 1)
    causal = col <= row
    tril = jnp.where(causal, 1.0, 0.0).astype(BF16)
    triu = jnp.where(row <= col, 1.0, 0.0).astype(BF16)
    dt = dt_ref[...]
    dtT = dtT_ref[...]
    acum = _dot01_lhs(tril, dt * -jnp.exp(alr_ref[...]))
    acumT = _dot01_rhs(dtT * -jnp.exp(alc_ref[...]), triu)
    a_end = acum[q_rows - 1:q_rows, :]
    e = e_ref[...]
    x = xs_ref[...]
    xf = x.astype(F32)
    to_end = (xf * _dot((dt * jnp.exp(a_end - acum)).astype(BF16), e)).astype(BF16)
    decay = _dot01_rhs(jnp.broadcast_to(jnp.exp(a_end), (V7X_SUBLANES, a_end.shape[1])), e)[:1, :]
    skip = xf * d_ref[...]
    for g in range(n_groups):
        sl = slice(g * gw, (g + 1) * gw)
        ns = slice(g * d_state, (g + 1) * d_state)
        bg, cg = b_ref[:, ns], c_ref[:, ns]
        st = st_sc[:, sl]
        y_ref[:, sl] = _ssd_group_out(x[:, sl], zs_ref[:, sl].astype(F32), _dot_nt(cg, bg), acum,
                                      acumT, dtT, causal, skip[:, sl], ng_ref[:, sl],
                                      g * hpg, hpg, head_dim,
                                      carried=(cg.astype(F32), st))
        st_sc[:, sl] = decay[:, sl] * st + _dot_tn(bg, to_end[:, sl])

    @pl.when(c == n_chunks - 1)
    def _():
        for g in range(n_groups):
            state_ref[g * gw:(g + 1) * gw, :] = st_sc[:, g * gw:(g + 1) * gw].T


def _ssd_sample_body(xs_ref, b_ref, c_ref, zs_ref, dt_ref, dtT_ref, alr_ref, alc_ref, e_ref,
                     d_ref, ng_ref, st_ref, y_ref, nst_ref, *, seq_len, head_dim):
    q_rows, gw = xs_ref.shape
    n_seq = q_rows // seq_len
    row = lax.broadcasted_iota(jnp.int32, (q_rows, q_rows), 0)
    col = lax.broadcasted_iota(jnp.int32, (q_rows, q_rows), 1)
    same = (row // seq_len) == (col // seq_len)
    mask = jnp.logical_and(same, col <= row)
    tril = jnp.where(mask, 1.0, 0.0).astype(BF16)
    triu = jnp.where(jnp.logical_and(same, row <= col), 1.0, 0.0).astype(BF16)
    ends = jnp.where(col == (row // seq_len) * seq_len + (seq_len - 1), 1.0, 0.0).astype(BF16)
    dt = dt_ref[...]
    dtT = dtT_ref[...]
    acum = _dot01_lhs(tril, dt * -jnp.exp(alr_ref[...]))
    acumT = _dot01_rhs(dtT * -jnp.exp(alc_ref[...]), triu)
    a_end = _dot01_lhs(ends, acum)
    e = e_ref[...]
    x = xs_ref[...].astype(F32)
    to_endT = (x * _dot((dt * jnp.exp(a_end - acum)).astype(BF16), e)).T.astype(BF16)
    decayT = _dot01_rhs(jnp.exp(a_end), e).T
    from_start = _spread(jnp.exp(acum), e)
    bg = b_ref[...].astype(F32)
    cg = c_ref[...].astype(F32)
    seq_of_row = lax.broadcasted_iota(jnp.int32, bg.shape, 0) // seq_len
    inter = jnp.zeros((q_rows, gw), F32)
    for s in range(n_seq):
        st = st_ref[s]
        mine = seq_of_row == s
        inter = inter + _dot_nt(jnp.where(mine, cg, 0.0).astype(BF16), st.astype(BF16))
        bm = jnp.where(mine, bg, 0.0).astype(BF16)
        nst_ref[s] = decayT[:, s * seq_len:s * seq_len + 1] * st + _dot(to_endT, bm)
    inter = from_start * inter + x * d_ref[...]
    y_ref[...] = _ssd_group_out(xs_ref[...], zs_ref[...].astype(F32), _dot_nt(c_ref[...], b_ref[...]),
                                acum, acumT, dtT, mask, inter, ng_ref[...], 0, gw // head_dim, head_dim)


N_PROMPT_SCAN_INPUTS = 11
N_SAMPLE_SCAN_INPUTS = 12


def _ssd_scan_kernel(*refs, n_chunks, n_groups, seq_len_s, head_dim, d_state):
    a, b = N_PROMPT_SCAN_INPUTS, N_PROMPT_SCAN_INPUTS + N_SAMPLE_SCAN_INPUTS
    prompt_in, sample_in = refs[:a], refs[a:b]
    y_p, state_p, y_s, state_s, st_sc = refs[b:]
    _ssd_prompt_body(pl.program_id(0) % n_chunks, n_chunks, *prompt_in, y_p, state_p, st_sc,
                     n_groups=n_groups, head_dim=head_dim, d_state=d_state)
    _ssd_sample_body(*sample_in, y_s, state_s, seq_len=seq_len_s, head_dim=head_dim)


def _ssd_scan(xbc, zs, dt, dtT, alog_row, alog_col, expand, d_x, ng,
              dt_g, dtT_g, alog_row_g, alog_col_g, state,
              *, n_seq_p, seq_len_p, n_rows_s, seq_len_s, d_inner, n_groups, head_dim, d_state):
    q = SSD_CHUNK
    row0 = n_seq_p * seq_len_p
    assert seq_len_p % q == 0 and q == d_state
    assert n_rows_s % q == 0 and q % seq_len_s == 0 and row0 % q == 0
    nc = seq_len_p // q
    nb = n_rows_s // q
    assert n_seq_p * nc == nb * n_groups
    spb = q // seq_len_s
    gw = d_inner // n_groups
    hp = dt.shape[1]
    hpg_rows = dtT_g.shape[1]
    gn = n_groups * d_state
    assert d_inner % gn == 0
    rb0 = row0 // q
    b_col0 = d_inner // d_state
    const = lambda t: (0, 0)
    sb = lambda t: t // n_groups
    g = lambda t: t % n_groups
    prompt_specs = [
        pl.BlockSpec((q, d_inner), lambda t: (t, 0)),
        pl.BlockSpec((q, gn), lambda t: (t, d_inner // gn)),
        pl.BlockSpec((q, gn), lambda t: (t, d_inner // gn + 1)),
        pl.BlockSpec((q, d_inner), lambda t: (t, 0)),
        pl.BlockSpec((q, hp), lambda t: (t, 0)),
        pl.BlockSpec((hp, q), lambda t: (0, t)),
        pl.BlockSpec((1, hp), const),
        pl.BlockSpec((hp, 1), const),
        pl.BlockSpec((hp, d_inner), const),
        pl.BlockSpec((1, d_inner), const),
        pl.BlockSpec((1, d_inner), const),
    ]
    sample_specs = [
        pl.BlockSpec((q, gw), lambda t: (rb0 + sb(t), g(t))),
        pl.BlockSpec((q, d_state), lambda t: (rb0 + sb(t), b_col0 + g(t))),
        pl.BlockSpec((q, d_state), lambda t: (rb0 + sb(t), b_col0 + n_groups + g(t))),
        pl.BlockSpec((q, gw), lambda t: (rb0 + sb(t), g(t))),
        pl.BlockSpec((None, q, hp), lambda t: (g(t), sb(t), 0)),
        pl.BlockSpec((None, hpg_rows, q), lambda t: (g(t), 0, sb(t))),
        pl.BlockSpec((None, 1, hp), lambda t: (g(t), 0, 0)),
        pl.BlockSpec((None, hpg_rows, 1), lambda t: (g(t), 0, 0)),
        pl.BlockSpec((hp, gw), const),
        pl.BlockSpec((1, gw), lambda t: (0, g(t))),
        pl.BlockSpec((1, gw), lambda t: (0, g(t))),
        pl.BlockSpec((spb, None, gw, d_state), lambda t: (sb(t), g(t), 0, 0)),
    ]
    assert len(prompt_specs) == N_PROMPT_SCAN_INPUTS and len(sample_specs) == N_SAMPLE_SCAN_INPUTS
    kern = functools.partial(_ssd_scan_kernel, n_chunks=nc, n_groups=n_groups, seq_len_s=seq_len_s,
                             head_dim=head_dim, d_state=d_state)
    return pl.pallas_call(
        kern,
        grid=(n_seq_p * nc,),
        in_specs=prompt_specs + sample_specs,
        out_specs=[
            pl.BlockSpec((q, d_inner), lambda t: (t, 0)),
            pl.BlockSpec((d_inner, d_state), lambda t: (t // nc, 0)),
            pl.BlockSpec((q, gw), lambda t: (sb(t), g(t))),
            pl.BlockSpec((spb, None, gw, d_state), lambda t: (sb(t), g(t), 0, 0)),
        ],
        out_shape=[
            jax.ShapeDtypeStruct((row0, d_inner), BF16),
            jax.ShapeDtypeStruct((n_seq_p * d_inner, d_state), F32),
            jax.ShapeDtypeStruct((n_rows_s, d_inner), BF16),
            jax.ShapeDtypeStruct(state.shape, F32),
        ],
        scratch_shapes=[pltpu.VMEM((d_state, d_inner), F32)],
        compiler_params=_params("arbitrary"),
        name="ssd_scan",
    )(xbc, xbc, xbc, zs, dt, dtT, alog_row, alog_col, expand, d_x, ng,
      xbc, xbc, xbc, zs, dt_g, dtT_g, alog_row_g, alog_col_g, expand[:, :gw], d_x, ng, state)


def _prompt_conv_state(tail, *, n_prompt_tiles, tiles_per_seq, km1):
    t = tail.reshape(-1, V7X_SUBLANES, tail.shape[1])[:n_prompt_tiles]
    return t[tiles_per_seq - 1::tiles_per_seq, V7X_SUBLANES - km1:, :]


def kernel(x_prompt, x_sample, p_prompt, p_sample, state_sc_conv, state_ssd_conv, state_ssd, g_mix, g_ffn, g_ple, g_final, sc_w_in, sc_w_conv, sc_w_out, ssd_w_in, ssd_conv_w, ssd_conv_b, ssd_dt_bias, ssd_a_log, ssd_d, ssd_norm_g, ssd_w_out, ffn_w_gate, ffn_w_up, ffn_w_down, ple_w_proj, ple_w_gate):
    bp, lp, d = x_prompt.shape
    bs, ls, _ = x_sample.shape
    depth = g_mix.shape[0]
    mp, ms = bp * lp, bs * ls
    pdim = p_prompt.shape[-1]
    n_heads, head_dim, d_state = state_ssd.shape[2:]
    d_inner = n_heads * head_dim
    conv_dim = ssd_conv_w.shape[-1]
    n_groups = (conv_dim - d_inner) // (2 * d_state)
    hpg = n_heads // n_groups
    assert n_heads <= V7X_LANES and V7X_LANES % head_dim == 0 and d_state == V7X_LANES
    npt = mp // ROW_TILE
    tps = lp // ROW_TILE
    row = lambda v: v.reshape(1, -1)
    pp = p_prompt.reshape(depth, mp, pdim)
    ps = p_sample.reshape(depth, ms, pdim)

    h = (x_prompt.reshape(mp, d), x_sample.reshape(ms, d))
    sc_p, sc_s, cv_p, cv_s, st_p, st_s = [], [], [], [], [], []
    y_out = None
    for i in range(depth):
        j = i // 2
        if i % 2 == 0:
            km1 = sc_w_conv.shape[1] - 1
            if not isinstance(h, tuple):
                h = (h[:mp], h[mp:])
            gated, tail, nstate = _short_conv_in(
                *h, row(g_mix[i]), sc_w_in, j, sc_w_conv, jnp.swapaxes(state_sc_conv, 1, 2),
                seq_len_p=lp, seq_len_s=ls)
            sc_p.append(_prompt_conv_state(tail, n_prompt_tiles=npt, tiles_per_seq=tps, km1=km1))
            sc_s.append(jnp.swapaxes(nstate, 0, 1))
            h = _matmul_residual(h, gated, sc_w_out, j, n_prompt_rows=mp)
        else:
            if isinstance(h, tuple):
                h = jnp.concatenate(h)
            km1 = ssd_conv_w.shape[1] - 1
            zx = d_inner + conv_dim
            pad_h = V7X_LANES - n_heads
            w_in_t = jnp.swapaxes(ssd_w_in, 1, 2)
            w_dt_t = jnp.pad(w_in_t[j, zx:, :], ((0, pad_h), (0, 0))).astype(BF16)
            dt_b = jnp.pad(ssd_dt_bias[j], (0, pad_h))
            alog = jnp.pad(ssd_a_log[j], (0, pad_h))
            zs, xbc, tail, nstate, dt, dtT = _ssd_in(
                h, row(g_mix[i]), w_in_t, j, w_dt_t, row(dt_b),
                ssd_conv_w, ssd_conv_b.reshape(ssd_conv_b.shape[0], 1, conv_dim),
                jnp.swapaxes(state_ssd_conv, 1, 2),
                n_prompt_rows=mp, seq_len_p=lp, seq_len_s=ls, d_inner=d_inner)
            cv_p.append(_prompt_conv_state(tail, n_prompt_tiles=npt, tiles_per_seq=tps, km1=km1))
            cv_s.append(jnp.swapaxes(nstate, 0, 1))
            head_of_lane = jnp.arange(d_inner, dtype=jnp.int32) // head_dim
            expand = (jnp.arange(V7X_LANES, dtype=jnp.int32)[:, None] == head_of_lane[None, :]).astype(BF16)
            d_x = row(jnp.repeat(ssd_d[j], head_dim))
            ng = row(ssd_norm_g[j])
            dt_s = dt[mp:]
            dt_g = jnp.stack([jnp.roll(dt_s, -g * hpg, axis=1) for g in range(n_groups)])
            alog_g = jnp.stack([jnp.roll(alog, -g * hpg) for g in range(n_groups)])
            dtT_g = dtT[:n_heads, mp:].reshape(n_groups, hpg, ms)
            y_p, new_p, y_s, new_s = _ssd_scan(
                xbc, zs, dt, dtT, row(alog), alog.reshape(-1, 1), expand, d_x, ng,
                dt_g, dtT_g, alog_g.reshape(n_groups, 1, -1), ssd_a_log[j].reshape(n_groups, hpg, 1),
                state_ssd[j].reshape(bs, n_groups, hpg * head_dim, d_state),
                n_seq_p=bp, seq_len_p=lp, n_rows_s=ms, seq_len_s=ls,
                d_inner=d_inner, n_groups=n_groups, head_dim=head_dim, d_state=d_state)
            st_p.append(new_p.reshape(bp, n_heads, head_dim, d_state))
            st_s.append(new_s.reshape(bs, n_heads, head_dim, d_state))
            h = _matmul_residual(h, (y_p, y_s), ssd_w_out, j, n_prompt_rows=mp)
        h = _ffn(h, row(g_ffn[i]), ffn_w_gate, ffn_w_up, ffn_w_down, i)
        ple_args = (h, pp, ps, row(g_ple[i]), ple_w_gate, ple_w_proj, i)
        if i == depth - 1:
            y_out = _ple(*ple_args, g_final=row(g_final))
        else:
            h = _ple(*ple_args)
    y_p, y_s = y_out
    return (y_p.reshape(bp, lp, d), y_s.reshape(bs, ls, d), jnp.stack(sc_p), jnp.stack(sc_s),
            jnp.stack(cv_p), jnp.stack(cv_s), jnp.stack(st_p), jnp.stack(st_s))
```

```python
import functools

import jax
import jax.numpy as jnp
from jax import lax
from jax.experimental import pallas as pl
from jax.experimental.pallas import tpu as pltpu

F32 = jnp.float32
BF16 = jnp.bfloat16
EPS = 1e-6
MASKED = -1e30
V7X_LANES = 128
V7X_SUBLANES = 8
HISTORY_ROWS = 16
V7X_VMEM_LIMIT = 56 * 1024 * 1024

ROW_TILE = 1024
HALF_ROW_TILE = 512
PLE_ROW_TILE = 256
COL_TILE = 512
NARROW_COL_TILE = 256
SSD_CHUNK = 128
WEIGHT_SLAB_BYTES = 16 * 1024 * 1024


def _params(*sem):
    return pltpu.CompilerParams(dimension_semantics=sem, vmem_limit_bytes=V7X_VMEM_LIMIT)


def _dot(a, b):
    return jnp.dot(a, b, preferred_element_type=F32)


def _dot_nt(a, b):
    return lax.dot_general(a, b, (((1,), (1,)), ((), ())), preferred_element_type=F32)


def _dot_tn(a, b):
    return lax.dot_general(a, b, (((0,), (0,)), ((), ())), preferred_element_type=F32)


def _split3(a):
    a1 = a.astype(BF16)
    r1 = a - a1.astype(F32)
    a2 = r1.astype(BF16)
    a3 = (r1 - a2.astype(F32)).astype(BF16)
    return a3, a2, a1


def _dot01_rhs(a, e):
    p3, p2, p1 = _split3(a)
    return (_dot(p3, e) + _dot(p2, e)) + _dot(p1, e)


def _spread(a, e):
    hi = a.astype(BF16)
    lo = (a - hi.astype(F32)).astype(BF16)
    return _dot(lo, e) + _dot(hi, e)


def _dot01_lhs(t, a):
    p3, p2, p1 = _split3(a)
    return (_dot(t, p3) + _dot(t, p2)) + _dot(t, p1)


def _rmsnorm(x, g):
    ms = jnp.mean(x * x, axis=-1, keepdims=True)
    return x * lax.rsqrt(ms + EPS) * g


def _softplus(x):
    return jnp.maximum(x, 0.0) + jnp.log1p(jnp.exp(-jnp.abs(x)))


def _silu(x):
    return x * jax.nn.sigmoid(x)


def _layer_spec(block, layer, imap):
    return pl.BlockSpec((None,) + tuple(block), lambda *a: (layer,) + tuple(imap(*a)))


def _conv(hist_ref, taps, seq_len=None, row0=0, n_rows=None):
    k = taps.shape[0]
    if n_rows is None:
        n_rows = hist_ref.shape[0] - HISTORY_ROWS
    out = taps[k - 1:k, :] * hist_ref[pl.ds(HISTORY_ROWS + row0, n_rows), :]
    if seq_len is not None:
        assert row0 % seq_len == 0
        t = lax.broadcasted_iota(jnp.int32, out.shape, 0) % seq_len
    for d in range(1, k):
        sh = hist_ref[pl.ds(HISTORY_ROWS + row0 - d, n_rows), :]
        if seq_len is not None:
            sh = jnp.where(t >= d, sh, 0.0)
        out = out + taps[k - 1 - d:k - d, :] * sh
    return out


def _state_correction(buf_ref, taps, stage_sc, seq_len):
    k = taps.shape[0]
    km1 = k - 1
    n_seq = stage_sc.shape[1] // seq_len
    stage_sc[...] = jnp.zeros_like(stage_sc)
    rows = [buf_ref[r] for r in range(km1)]
    for t in range(km1):
        acc = None
        for d in range(t + 1, k):
            term = taps[k - 1 - d:k - d, :] * rows[km1 + t - d]
            acc = term if acc is None else acc + term
        for c in range(stage_sc.shape[0]):
            stage_sc[c, pl.ds(t, n_seq, stride=seq_len), :] = acc[:, c * V7X_LANES:(c + 1) * V7X_LANES]


def _staged(stage_sc):
    return jnp.concatenate([stage_sc[c] for c in range(stage_sc.shape[0])], axis=1)


def _emit_sample_state(u, stage_sc, nstate_ref, seq_len):
    km1 = nstate_ref.shape[0]
    n_chunks = stage_sc.shape[0]
    n_seq = stage_sc.shape[1] // seq_len
    for c in range(n_chunks):
        stage_sc[c] = u[:, c * V7X_LANES:(c + 1) * V7X_LANES]
    for r in range(km1):
        nstate_ref[r] = jnp.concatenate(
            [stage_sc[c, pl.ds(seq_len - km1 + r, n_seq, stride=seq_len), :] for c in range(n_chunks)],
            axis=1)


def _sc_in_sample_kernel(x_ref, g_ref, wb_ref, wc_ref, wv_ref, taps_ref, buf_ref,
                         gated_ref, nstate_ref, wbb_ref, wcb_ref, wvb_ref, hn_sc, hist_sc, stage_sc,
                         *, sample_len):
    @pl.when(pl.program_id(0) == 0)
    def _():
        hn_sc[...] = _rmsnorm(x_ref[...], g_ref[...]).astype(BF16)

    wb, wc, wv = wb_ref[...].astype(BF16), wc_ref[...].astype(BF16), wv_ref[...].astype(BF16)
    wbb_ref[...], wcb_ref[...], wvb_ref[...] = wb, wc, wv
    taps = taps_ref[...]
    _state_correction(buf_ref, taps, stage_sc, sample_len)
    hn = hn_sc[...]
    bg = _dot(hn, wb)
    u = _dot(hn, wc) * _dot(hn, wv)
    hist_sc[:HISTORY_ROWS, :] = jnp.zeros((HISTORY_ROWS, u.shape[1]), F32)
    hist_sc[HISTORY_ROWS:, :] = u
    gated_ref[...] = (bg * (_conv(hist_sc, taps, sample_len) + _staged(stage_sc))).astype(BF16)
    _emit_sample_state(u, stage_sc, nstate_ref, sample_len)


def _sc_in_prompt_kernel(x_ref, g_ref, wb_ref, wc_ref, wv_ref, taps_ref,
                         gated_ref, tail_ref, hn_sc, carry_sc, hist_sc, *, tiles_per_seq):
    i = pl.program_id(0)
    j = pl.program_id(1)

    @pl.when(j == 0)
    def _():
        hn_sc[...] = _rmsnorm(x_ref[...], g_ref[...]).astype(BF16)

    hn = hn_sc[...]
    bg = _dot(hn, wb_ref[...])
    u = _dot(hn, wc_ref[...]) * _dot(hn, wv_ref[...])
    hist_sc[:HISTORY_ROWS, :] = jnp.where(i % tiles_per_seq == 0, 0.0, carry_sc[j])
    hist_sc[HISTORY_ROWS:, :] = u
    gated_ref[...] = (bg * _conv(hist_sc, taps_ref[...])).astype(BF16)
    carry_sc[j] = u[u.shape[0] - HISTORY_ROWS:, :]
    tail_ref[...] = u[u.shape[0] - V7X_SUBLANES:, :]


def _short_conv_in(xp, xs, g, w_in, layer, taps, buf, *, seq_len_p, seq_len_s):
    mp, d = xp.shape
    ms = xs.shape[0]
    k = taps.shape[1]
    tm = ROW_TILE
    assert mp % tm == 0 and ms == tm and seq_len_p % tm == 0 and tm % seq_len_s == 0
    assert k - 1 <= min(seq_len_s, V7X_SUBLANES)
    n_seq_s = ms // seq_len_s

    tn = NARROW_COL_TILE
    nj = d // tn
    assert d % tn == 0 and tn % V7X_LANES == 0
    gated_s, nstate, wb, wc, wv = pl.pallas_call(
        functools.partial(_sc_in_sample_kernel, sample_len=seq_len_s),
        grid=(nj,),
        in_specs=[
            pl.BlockSpec((tm, d), lambda j: (0, 0)),
            pl.BlockSpec((1, d), lambda j: (0, 0)),
            _layer_spec((d, tn), layer, lambda j: (0, j)),
            _layer_spec((d, tn), layer, lambda j: (0, nj + j)),
            _layer_spec((d, tn), layer, lambda j: (0, 2 * nj + j)),
            _layer_spec((k, tn), layer, lambda j: (0, j)),
            _layer_spec((k - 1, n_seq_s, tn), layer, lambda j: (0, 0, j)),
        ],
        out_specs=[pl.BlockSpec((tm, tn), lambda j: (0, j)),
                   pl.BlockSpec((k - 1, n_seq_s, tn), lambda j: (0, 0, j))]
        + [pl.BlockSpec((d, tn), lambda j: (0, j))] * 3,
        out_shape=[jax.ShapeDtypeStruct((ms, d), BF16), jax.ShapeDtypeStruct((k - 1, n_seq_s, d), F32)]
        + [jax.ShapeDtypeStruct((d, d), BF16)] * 3,
        scratch_shapes=[pltpu.VMEM((tm, d), BF16), pltpu.VMEM((HISTORY_ROWS + tm, tn), F32),
                        pltpu.VMEM((tn // V7X_LANES, tm, V7X_LANES), F32)],
        compiler_params=_params("arbitrary"),
        name="short_conv_in_sample",
    )(xs, g, w_in, w_in, w_in, taps, buf)

    tn = COL_TILE
    nj = d // tn
    assert d % tn == 0
    npt = mp // tm
    gated_p, tail = pl.pallas_call(
        functools.partial(_sc_in_prompt_kernel, tiles_per_seq=seq_len_p // tm),
        grid=(npt, nj),
        in_specs=[
            pl.BlockSpec((tm, d), lambda i, j: (i, 0)),
            pl.BlockSpec((1, d), lambda i, j: (0, 0)),
            pl.BlockSpec((d, tn), lambda i, j: (0, j)),
            pl.BlockSpec((d, tn), lambda i, j: (0, j)),
            pl.BlockSpec((d, tn), lambda i, j: (0, j)),
            _layer_spec((k, tn), layer, lambda i, j: (0, j)),
        ],
        out_specs=[pl.BlockSpec((tm, tn), lambda i, j: (i, j)),
                   pl.BlockSpec((V7X_SUBLANES, tn), lambda i, j: (i, j))],
        out_shape=[jax.ShapeDtypeStruct((mp, d), BF16),
                   jax.ShapeDtypeStruct((npt * V7X_SUBLANES, d), F32)],
        scratch_shapes=[pltpu.VMEM((tm, d), BF16), pltpu.VMEM((nj, HISTORY_ROWS, tn), F32),
                        pltpu.VMEM((HISTORY_ROWS + tm, tn), F32)],
        compiler_params=_params("arbitrary", "arbitrary"),
        name="short_conv_in_prompt",
    )(xp, g, wb, wc, wv, taps)
    return (gated_p, gated_s), tail, nstate


def _res_kernel(*refs, n_prompt_tiles, res_pair, a_pair):
    refs = list(refs)
    res_refs = [refs.pop(0) for _ in range(2 if res_pair else 1)]
    a_refs = [refs.pop(0) for _ in range(2 if a_pair else 1)]
    w_ref, out_ref, wb_sc = refs
    i = pl.program_id(1)

    @pl.when(i == 0)
    def _():
        wb_sc[...] = w_ref[...].astype(BF16)

    def body(which):
        out_ref[...] = res_refs[which * res_pair][...] + _dot(a_refs[which * a_pair][...], wb_sc[...])

    if not (res_pair or a_pair):
        body(0)
        return
    pl.when(i < n_prompt_tiles)(lambda: body(0))
    pl.when(i >= n_prompt_tiles)(lambda: body(1))


def _matmul_residual(res, a, w, layer, *, n_prompt_rows):
    res_pair, a_pair = isinstance(res, tuple), isinstance(a, tuple)
    _, k, n = w.shape
    m = sum(r.shape[0] for r in res) if res_pair else res.shape[0]
    tm = HALF_ROW_TILE
    tn = min(n, WEIGHT_SLAB_BYTES // (4 * k))
    assert n_prompt_rows % tm == 0 and m % tm == 0 and n % tn == 0 and tn % V7X_LANES == 0
    npt = n_prompt_rows // tm

    def row_specs(pair, block, col_of):
        if not pair:
            return [pl.BlockSpec(block, lambda j, i: (i, col_of(j)))]
        return [pl.BlockSpec(block, lambda j, i: (jnp.minimum(i, npt - 1), col_of(j))),
                pl.BlockSpec(block, lambda j, i: (jnp.maximum(i - npt, 0), col_of(j)),
                             pipeline_mode=pl.Buffered(1))]

    return pl.pallas_call(
        functools.partial(_res_kernel, n_prompt_tiles=npt, res_pair=res_pair, a_pair=a_pair),
        grid=(n // tn, m // tm),
        in_specs=row_specs(res_pair, (tm, tn), lambda j: j) + row_specs(a_pair, (tm, k), lambda j: 0)
        + [pl.BlockSpec((None, k, tn), lambda j, i: (layer, 0, j), pipeline_mode=pl.Buffered(1))],
        out_specs=pl.BlockSpec((tm, tn), lambda j, i: (i, j)),
        out_shape=jax.ShapeDtypeStruct((m, n), F32),
        scratch_shapes=[pltpu.VMEM((k, tn), BF16)],
        compiler_params=_params("arbitrary", "arbitrary"),
        name="matmul_residual",
    )(*(res if res_pair else (res,)), *(a if a_pair else (a,)), w)


def _ffn_kernel(h_ref, g_ref, wg_ref, wu_ref, wd_ref, out_ref, *rest):
    *cast_refs, hn_sc = rest

    @pl.when(pl.program_id(1) == 0)
    def _():
        h = h_ref[...]
        hn_sc[...] = _rmsnorm(h, g_ref[...]).astype(BF16)
        out_ref[...] = h

    wg, wu, wd = wg_ref[...], wu_ref[...], wd_ref[...]
    if cast_refs:
        wg, wu, wd = wg.astype(BF16), wu.astype(BF16), wd.astype(BF16)
        for ref, w in zip(cast_refs, (wg, wu, wd)):
            ref[...] = w
    hn = hn_sc[...]
    gate = _dot(hn, wg)
    act = (_silu(gate) * _dot(hn, wu)).astype(BF16)
    out_ref[...] += _dot(act, wd)


def _ffn(h, g, w_gate, w_up, w_down, layer):
    m, d = h.shape
    f = w_gate.shape[2]
    tm = ROW_TILE
    assert m % tm == 0 and m > tm and f % COL_TILE == 0 and f % NARROW_COL_TILE == 0

    def call(row0_tile, n_row_tiles, tf, weights, weight_specs, cast_outputs, name):
        out_specs = [pl.BlockSpec((tm, d), lambda i, j: (i, 0))]
        out_shape = [jax.ShapeDtypeStruct((n_row_tiles * tm, d), F32)]
        if cast_outputs:
            out_specs += [pl.BlockSpec((d, tf), lambda i, j: (0, j)), pl.BlockSpec((d, tf), lambda i, j: (0, j)),
                          pl.BlockSpec((tf, d), lambda i, j: (j, 0))]
            out_shape += [jax.ShapeDtypeStruct((d, f), BF16), jax.ShapeDtypeStruct((d, f), BF16),
                          jax.ShapeDtypeStruct((f, d), BF16)]
        h_mode = dict(pipeline_mode=pl.Buffered(1)) if n_row_tiles == 1 else {}
        return pl.pallas_call(
            _ffn_kernel,
            grid=(n_row_tiles, f // tf),
            in_specs=[pl.BlockSpec((tm, d), lambda i, j: (i + row0_tile, 0), **h_mode),
                      pl.BlockSpec((1, d), lambda i, j: (0, 0))] + weight_specs(tf),
            out_specs=out_specs,
            out_shape=out_shape,
            scratch_shapes=[pltpu.VMEM((tm, d), BF16)],
            compiler_params=_params("arbitrary", "arbitrary"),
            name=name,
        )(h, g, *weights)

    f32_specs = lambda tf: [_layer_spec((d, tf), layer, lambda i, j: (0, j)),
                            _layer_spec((d, tf), layer, lambda i, j: (0, j)),
                            _layer_spec((tf, d), layer, lambda i, j: (j, 0))]
    bf16_specs = lambda tf: [pl.BlockSpec((d, tf), lambda i, j: (0, j)), pl.BlockSpec((d, tf), lambda i, j: (0, j)),
                             pl.BlockSpec((tf, d), lambda i, j: (j, 0))]
    first, wg_b, wu_b, wd_b = call(0, 1, NARROW_COL_TILE, (w_gate, w_up, w_down), f32_specs, True,
                                   "swiglu_ffn_first")
    rest, = call(1, m // tm - 1, COL_TILE, (wg_b, wu_b, wd_b), bf16_specs, False, "swiglu_ffn_rest")
    return first, rest


def _ple_kernel(ha_ref, hb_ref, pp_ref, ps_ref, g_ref, wg_ref, wp_ref, *rest,
                n_first_tiles, n_prompt_tiles, final):
    i = pl.program_id(0)
    *rest, wgb_sc, wpb_sc = rest

    @pl.when(i == 0)
    def _():
        wgb_sc[...] = wg_ref[...].astype(BF16)
        wpb_sc[...] = wp_ref[...].astype(BF16)

    h = jnp.where(i < n_first_tiles, ha_ref[...], hb_ref[...])
    gate = jax.nn.sigmoid(_dot(_rmsnorm(h, g_ref[...]).astype(BF16), wgb_sc[...]))
    p = jnp.where(i < n_prompt_tiles, pp_ref[...], ps_ref[...]).astype(BF16)
    out = h + _dot(p, wpb_sc[...]) * gate
    if not final:
        rest[0][...] = out
        return
    gf_ref, yp_ref, ys_ref = rest
    y = _rmsnorm(out, gf_ref[...])

    @pl.when(i < n_prompt_tiles)
    def _():
        yp_ref[...] = y

    @pl.when(i >= n_prompt_tiles)
    def _():
        ys_ref[...] = y


def _ple(h, pp, ps, g, w_gate, w_proj, layer, g_final=None):
    ha, hb = h
    d = ha.shape[1]
    m = ha.shape[0] + hb.shape[0]
    _, mp, pdim = pp.shape
    tm = PLE_ROW_TILE
    assert m % tm == 0 and mp % tm == 0 and ha.shape[0] % tm == 0
    npt = mp // tm
    nft = ha.shape[0] // tm
    final = g_final is not None
    const = lambda i: (0, 0)
    resident = dict(pipeline_mode=pl.Buffered(1))
    in_specs = [
        pl.BlockSpec((tm, d), lambda i: (jnp.minimum(i, nft - 1), 0)),
        pl.BlockSpec((tm, d), lambda i: (jnp.maximum(i - nft, 0), 0)),
        _layer_spec((tm, pdim), layer, lambda i: (jnp.minimum(i, npt - 1), 0)),
        _layer_spec((tm, pdim), layer, lambda i: (jnp.maximum(i - npt, 0), 0)),
        pl.BlockSpec((1, d), const),
        pl.BlockSpec((None, d, d), lambda i: (layer, 0, 0), **resident),
        pl.BlockSpec((None, pdim, d), lambda i: (layer, 0, 0), **resident),
    ]
    args = [ha, hb, pp, ps, g, w_gate, w_proj]
    if final:
        in_specs.append(pl.BlockSpec((1, d), const))
        args.append(g_final)
        out_specs = [pl.BlockSpec((tm, d), lambda i: (jnp.minimum(i, npt - 1), 0)),
                     pl.BlockSpec((tm, d), lambda i: (jnp.maximum(i - npt, 0), 0))]
        out_shape = [jax.ShapeDtypeStruct((mp, d), F32), jax.ShapeDtypeStruct((m - mp, d), F32)]
    else:
        out_specs = pl.BlockSpec((tm, d), lambda i: (i, 0))
        out_shape = jax.ShapeDtypeStruct((m, d), F32)
    return pl.pallas_call(
        functools.partial(_ple_kernel, n_first_tiles=nft, n_prompt_tiles=npt, final=final),
        grid=(m // tm,),
        in_specs=in_specs,
        out_specs=out_specs,
        out_shape=out_shape,
        scratch_shapes=[pltpu.VMEM((d, d), BF16), pltpu.VMEM((pdim, d), BF16)],
        compiler_params=_params("arbitrary"),
        name="ple_final" if final else "ple",
    )(*args)


def _ssd_in_kernel(h_ref, g_ref, wt_ref, wdtT_ref, dtb_ref, taps_ref, cb_ref, *rest,
                   sample_len, tiles_per_seq, n_z_tiles):
    if sample_len is not None:
        (buf_ref, zs_ref, xbc_ref, nstate_ref, dt_ref, dtT_ref, wtb_ref,
         hn_sc, hist_sc, stage_sc) = rest
        i, j = 0, pl.program_id(0)
    else:
        zs_ref, xbc_ref, tail_ref, dt_ref, dtT_ref, hn_sc, carry_sc, hist_sc = rest
        i, j = pl.program_id(0), pl.program_id(1)
    jc = j - n_z_tiles
    tm = hn_sc.shape[0]

    @pl.when(j == 0)
    def _():
        hn = _rmsnorm(h_ref[...], g_ref[...]).astype(BF16)
        hn_sc[...] = hn
        dt = _softplus(_dot_nt(hn, wdtT_ref[...]) + dtb_ref[...])
        dt_ref[...] = dt
        dtT_ref[...] = dt.T

    def weight_tile():
        if sample_len is None:
            return wt_ref[...]
        wb = wt_ref[...].astype(BF16)
        wtb_ref[...] = wb
        return wb

    def finish(conv):
        return _silu(conv + cb_ref[...]).astype(BF16)

    @pl.when(j < n_z_tiles)
    def _():
        zs_ref[...] = _silu(_dot_nt(hn_sc[...], weight_tile())).astype(BF16)

    @pl.when(j >= n_z_tiles)
    def _():
        hist_sc[HISTORY_ROWS:, :] = _dot_nt(hn_sc[...], weight_tile())
        taps = taps_ref[...]
        if sample_len is None:
            hist_sc[:HISTORY_ROWS, :] = jnp.where(i % tiles_per_seq == 0, 0.0, carry_sc[jc])
            xbc_ref[...] = finish(_conv(hist_sc, taps))
            carry_sc[jc] = hist_sc[tm:, :]
            tail_ref[...] = hist_sc[HISTORY_ROWS + tm - V7X_SUBLANES:, :]
        else:
            _state_correction(buf_ref, taps, stage_sc, sample_len)
            hist_sc[:HISTORY_ROWS, :] = jnp.zeros((HISTORY_ROWS, hist_sc.shape[1]), F32)
            xbc_ref[...] = finish(_conv(hist_sc, taps, sample_len) + _staged(stage_sc))
            _emit_sample_state(hist_sc[HISTORY_ROWS:, :], stage_sc, nstate_ref, sample_len)


def _ssd_in(h, g, w_in_t, layer, w_dt_t, dt_b, taps, conv_b, buf,
            *, n_prompt_rows, seq_len_p, seq_len_s, d_inner):
    m, d = h.shape
    _, k, conv_dim = taps.shape
    hp = w_dt_t.shape[0]
    tm, tn = ROW_TILE, COL_TILE
    ms = m - n_prompt_rows
    assert n_prompt_rows % tm == 0 and ms == tm and seq_len_p % tm == 0 and tm % seq_len_s == 0
    assert d_inner % tn == 0 and conv_dim % tn == 0 and k - 1 <= min(seq_len_s, V7X_SUBLANES)
    npt = n_prompt_rows // tm
    n_seq_s = ms // seq_len_s
    nz, nc = d_inner // tn, conv_dim // tn
    zc = lambda j: jnp.minimum(j, nz - 1)
    cc = lambda j: jnp.maximum(j - nz, 0)

    zs_s, xbc_s, nstate, dt_s, dtT_s, w_b = pl.pallas_call(
        functools.partial(_ssd_in_kernel, sample_len=seq_len_s, tiles_per_seq=None, n_z_tiles=nz),
        grid=(nz + nc,),
        in_specs=[
            pl.BlockSpec((tm, d), lambda j: (npt, 0)),
            pl.BlockSpec((1, d), lambda j: (0, 0)),
            _layer_spec((tn, d), layer, lambda j: (j, 0)),
            pl.BlockSpec((hp, d), lambda j: (0, 0)),
            pl.BlockSpec((1, hp), lambda j: (0, 0)),
            _layer_spec((k, tn), layer, lambda j: (0, cc(j))),
            _layer_spec((1, tn), layer, lambda j: (0, cc(j))),
            _layer_spec((k - 1, n_seq_s, tn), layer, lambda j: (0, 0, cc(j))),
        ],
        out_specs=[
            pl.BlockSpec((tm, tn), lambda j: (0, zc(j))),
            pl.BlockSpec((tm, tn), lambda j: (0, cc(j))),
            pl.BlockSpec((k - 1, n_seq_s, tn), lambda j: (0, 0, cc(j))),
            pl.BlockSpec((tm, hp), lambda j: (0, 0)),
            pl.BlockSpec((hp, tm), lambda j: (0, 0)),
            pl.BlockSpec((tn, d), lambda j: (j, 0)),
        ],
        out_shape=[
            jax.ShapeDtypeStruct((ms, d_inner), BF16),
            jax.ShapeDtypeStruct((ms, conv_dim), BF16),
            jax.ShapeDtypeStruct((k - 1, n_seq_s, conv_dim), F32),
            jax.ShapeDtypeStruct((ms, hp), F32),
            jax.ShapeDtypeStruct((hp, ms), F32),
            jax.ShapeDtypeStruct(((nz + nc) * tn, d), BF16),
        ],
        scratch_shapes=[pltpu.VMEM((tm, d), BF16), pltpu.VMEM((HISTORY_ROWS + tm, tn), F32),
                        pltpu.VMEM((tn // V7X_LANES, tm, V7X_LANES), F32)],
        compiler_params=_params("arbitrary"),
        name="ssd_in_sample",
    )(h, g, w_in_t, w_dt_t, dt_b, taps, conv_b, buf)

    const = lambda i, j: (0, 0)
    zs_p, xbc_p, tail, dt_p, dtT_p = pl.pallas_call(
        functools.partial(_ssd_in_kernel, sample_len=None, tiles_per_seq=seq_len_p // tm, n_z_tiles=nz),
        grid=(npt, nz + nc),
        in_specs=[
            pl.BlockSpec((tm, d), lambda i, j: (i, 0)),
            pl.BlockSpec((1, d), const),
            pl.BlockSpec((tn, d), lambda i, j: (j, 0)),
            pl.BlockSpec((hp, d), const),
            pl.BlockSpec((1, hp), const),
            _layer_spec((k, tn), layer, lambda i, j: (0, cc(j))),
            _layer_spec((1, tn), layer, lambda i, j: (0, cc(j))),
        ],
        out_specs=[
            pl.BlockSpec((tm, tn), lambda i, j: (i, zc(j))),
            pl.BlockSpec((tm, tn), lambda i, j: (i, cc(j))),
            pl.BlockSpec((V7X_SUBLANES, tn), lambda i, j: (i, cc(j))),
            pl.BlockSpec((tm, hp), lambda i, j: (i, 0)),
            pl.BlockSpec((hp, tm), lambda i, j: (0, i)),
        ],
        out_shape=[
            jax.ShapeDtypeStruct((n_prompt_rows, d_inner), BF16),
            jax.ShapeDtypeStruct((n_prompt_rows, conv_dim), BF16),
            jax.ShapeDtypeStruct((npt * V7X_SUBLANES, conv_dim), F32),
            jax.ShapeDtypeStruct((n_prompt_rows, hp), F32),
            jax.ShapeDtypeStruct((hp, n_prompt_rows), F32),
        ],
        scratch_shapes=[pltpu.VMEM((tm, d), BF16), pltpu.VMEM((nc, HISTORY_ROWS, tn), F32),
                        pltpu.VMEM((HISTORY_ROWS + tm, tn), F32)],
        compiler_params=_params("arbitrary", "arbitrary"),
        name="ssd_in_prompt",
    )(h, g, w_b, w_dt_t, dt_b, taps, conv_b)
    return ((zs_p, zs_s), (xbc_p, xbc_s), (dt_p, dt_s), (dtT_p, dtT_s)), tail, nstate


def _ssd_group_out(x, zs, cb, acum, acumT, dtT, mask, extra, ng, head0, heads_per_group, head_dim,
                   carried=None):
    heads_per_slab = V7X_LANES // head_dim
    parts = []
    for q in range(heads_per_group // heads_per_slab):
        cols = slice(q * V7X_LANES, (q + 1) * V7X_LANES)
        rhs = x[:, cols]
        if carried is not None:
            cg, st_t = carried
            rhs = jnp.concatenate([rhs, st_t[:, cols].astype(BF16)], axis=0)
        lanes = lax.broadcasted_iota(jnp.int32, rhs.shape, 1)
        acc = None
        for r in range(heads_per_slab):
            hd = head0 + q * heads_per_slab + r
            a_t = jnp.broadcast_to(acum[:, hd:hd + 1], cb.shape)
            seg = a_t - acumT[hd:hd + 1, :]
            lhs = (cb * jnp.exp(jnp.where(mask, seg, MASKED)) * dtT[hd:hd + 1, :]).astype(BF16)
            if carried is not None:
                lhs = jnp.concatenate([lhs, (cg * jnp.exp(a_t)).astype(BF16)], axis=1)
            in_head = jnp.logical_and(lanes >= r * head_dim, lanes < (r + 1) * head_dim)
            part = _dot(lhs, jnp.where(in_head, rhs, jnp.zeros_like(rhs)))
            acc = part if acc is None else acc + part
        parts.append(acc)
    y = jnp.concatenate(parts, axis=1) + extra
    gated = y * zs
    ms = jnp.mean(gated * gated, axis=-1, keepdims=True)
    return (gated * lax.rsqrt(ms + EPS) * ng).astype(BF16)


def _ssd_prompt_body(c, n_chunks, xs_ref, b_ref, c_ref, zs_ref, dt_ref, dtT_ref, alr_ref, alc_ref,
                     e_ref, d_ref, ng_ref, y_ref, state_ref, st_sc, *, n_groups, head_dim, d_state):
    @pl.when(c == 0)
    def _():
        st_sc[...] = jnp.zeros_like(st_sc)

    q_rows, d_inner = xs_ref.shape
    gw = d_inner // n_groups
    hpg = gw // head_dim
    row = lax.broadcasted_iota(jnp.int32, (q_rows, q_rows), 0)
    col = lax.broadcasted_iota(jnp.int32, (q_rows, q_rows), 1)
    causal = col <= row
    tril = jnp.where(causal, 1.0, 0.0).astype(BF16)
    triu = jnp.where(row <= col, 1.0, 0.0).astype(BF16)
    dt = dt_ref[...]
    dtT = dtT_ref[...]
    acum = _dot01_lhs(tril, dt * -jnp.exp(alr_ref[...]))
    acumT = _dot01_rhs(dtT * -jnp.exp(alc_ref[...]), triu)
    a_end = acum[q_rows - 1:q_rows, :]
    e = e_ref[...]
    x = xs_ref[...]
    xf = x.astype(F32)
    to_end = (xf * _dot((dt * jnp.exp(a_end - acum)).astype(BF16), e)).astype(BF16)
    decay = _dot01_rhs(jnp.broadcast_to(jnp.exp(a_end), (V7X_SUBLANES, a_end.shape[1])), e)[:1, :]
    skip = xf * d_ref[...]
    for g in range(n_groups):
        sl = slice(g * gw, (g + 1) * gw)
        ns = slice(g * d_state, (g + 1) * d_state)
        bg, cg = b_ref[:, ns], c_ref[:, ns]
        st = st_sc[:, sl]
        y_ref[:, sl] = _ssd_group_out(x[:, sl], zs_ref[:, sl].astype(F32), _dot_nt(cg, bg), acum,
                                      acumT, dtT, causal, skip[:, sl], ng_ref[:, sl],
                                      g * hpg, hpg, head_dim,
                                      carried=(cg.astype(F32), st))
        st_sc[:, sl] = decay[:, sl] * st + _dot_tn(bg, to_end[:, sl])

    @pl.when(c == n_chunks - 1)
    def _():
        for g in range(n_groups):
            state_ref[g * gw:(g + 1) * gw, :] = st_sc[:, g * gw:(g + 1) * gw].T


def _ssd_sample_body(xs_ref, b_ref, c_ref, zs_ref, dt_ref, dtT_ref, alr_ref, alc_ref, e_ref,
                     d_ref, ng_ref, st_ref, y_ref, nst_ref, *, seq_len, head_dim):
    q_rows, gw = xs_ref.shape
    n_seq = q_rows // seq_len
    row = lax.broadcasted_iota(jnp.int32, (q_rows, q_rows), 0)
    col = lax.broadcasted_iota(jnp.int32, (q_rows, q_rows), 1)
    same = (row // seq_len) == (col // seq_len)
    mask = jnp.logical_and(same, col <= row)
    tril = jnp.where(mask, 1.0, 0.0).astype(BF16)
    triu = jnp.where(jnp.logical_and(same, row <= col), 1.0, 0.0).astype(BF16)
    ends = jnp.where(col == (row // seq_len) * seq_len + (seq_len - 1), 1.0, 0.0).astype(BF16)
    dt = dt_ref[...]
    dtT = dtT_ref[...]
    acum = _dot01_lhs(tril, dt * -jnp.exp(alr_ref[...]))
    acumT = _dot01_rhs(dtT * -jnp.exp(alc_ref[...]), triu)
    a_end = _dot01_lhs(ends, acum)
    e = e_ref[...]
    x = xs_ref[...].astype(F32)
    to_endT = (x * _dot((dt * jnp.exp(a_end - acum)).astype(BF16), e)).T.astype(BF16)
    decayT = _dot01_rhs(jnp.exp(a_end), e).T
    from_start = _spread(jnp.exp(acum), e)
    bg = b_ref[...].astype(F32)
    cg = c_ref[...].astype(F32)
    seq_of_row = lax.broadcasted_iota(jnp.int32, bg.shape, 0) // seq_len
    inter = jnp.zeros((q_rows, gw), F32)
    for s in range(n_seq):
        st = st_ref[s]
        mine = seq_of_row == s
        inter = inter + _dot_nt(jnp.where(mine, cg, 0.0).astype(BF16), st.astype(BF16))
        bm = jnp.where(mine, bg, 0.0).astype(BF16)
        nst_ref[s] = decayT[:, s * seq_len:s * seq_len + 1] * st + _dot(to_endT, bm)
    inter = from_start * inter + x * d_ref[...]
    y_ref[...] = _ssd_group_out(xs_ref[...], zs_ref[...].astype(F32), _dot_nt(c_ref[...], b_ref[...]),
                                acum, acumT, dtT, mask, inter, ng_ref[...], 0, gw // head_dim, head_dim)


N_PROMPT_SCAN_INPUTS = 11
N_SAMPLE_SCAN_INPUTS = 12


def _ssd_scan_kernel(*refs, n_chunks, n_groups, seq_len_s, head_dim, d_state):
    a, b = N_PROMPT_SCAN_INPUTS, N_PROMPT_SCAN_INPUTS + N_SAMPLE_SCAN_INPUTS
    prompt_in, sample_in = refs[:a], refs[a:b]
    y_p, state_p, y_s, state_s, st_sc = refs[b:]
    _ssd_prompt_body(pl.program_id(0) % n_chunks, n_chunks, *prompt_in, y_p, state_p, st_sc,
                     n_groups=n_groups, head_dim=head_dim, d_state=d_state)
    _ssd_sample_body(*sample_in, y_s, state_s, seq_len=seq_len_s, head_dim=head_dim)


def _ssd_scan(xbc_p, zs_p, dt, dtT, alog_row, alog_col, expand, d_x, ng,
              xbc_s, zs_s, dt_g, dtT_g, alog_row_g, alog_col_g, state,
              *, n_seq_p, seq_len_p, seq_len_s, d_inner, n_groups, head_dim, d_state):
    q = SSD_CHUNK
    n_rows_s = xbc_s.shape[0]
    assert seq_len_p % q == 0 and q == d_state
    assert n_rows_s % q == 0 and q % seq_len_s == 0
    nc = seq_len_p // q
    nb = n_rows_s // q
    assert n_seq_p * nc == nb * n_groups
    spb = q // seq_len_s
    gw = d_inner // n_groups
    hp = dt.shape[1]
    hpg_rows = dtT_g.shape[1]
    gn = n_groups * d_state
    assert d_inner % gn == 0
    b_col0 = d_inner // d_state
    const = lambda t: (0, 0)
    sb = lambda t: t // n_groups
    g = lambda t: t % n_groups
    prompt_specs = [
        pl.BlockSpec((q, d_inner), lambda t: (t, 0)),
        pl.BlockSpec((q, gn), lambda t: (t, d_inner // gn)),
        pl.BlockSpec((q, gn), lambda t: (t, d_inner // gn + 1)),
        pl.BlockSpec((q, d_inner), lambda t: (t, 0)),
        pl.BlockSpec((q, hp), lambda t: (t, 0)),
        pl.BlockSpec((hp, q), lambda t: (0, t)),
        pl.BlockSpec((1, hp), const),
        pl.BlockSpec((hp, 1), const),
        pl.BlockSpec((hp, d_inner), const),
        pl.BlockSpec((1, d_inner), const),
        pl.BlockSpec((1, d_inner), const),
    ]
    sample_specs = [
        pl.BlockSpec((q, gw), lambda t: (sb(t), g(t))),
        pl.BlockSpec((q, d_state), lambda t: (sb(t), b_col0 + g(t))),
        pl.BlockSpec((q, d_state), lambda t: (sb(t), b_col0 + n_groups + g(t))),
        pl.BlockSpec((q, gw), lambda t: (sb(t), g(t))),
        pl.BlockSpec((None, q, hp), lambda t: (g(t), sb(t), 0)),
        pl.BlockSpec((None, hpg_rows, q), lambda t: (g(t), 0, sb(t))),
        pl.BlockSpec((None, 1, hp), lambda t: (g(t), 0, 0)),
        pl.BlockSpec((None, hpg_rows, 1), lambda t: (g(t), 0, 0)),
        pl.BlockSpec((hp, gw), const),
        pl.BlockSpec((1, gw), lambda t: (0, g(t))),
        pl.BlockSpec((1, gw), lambda t: (0, g(t))),
        pl.BlockSpec((spb, None, gw, d_state), lambda t: (sb(t), g(t), 0, 0)),
    ]
    assert len(prompt_specs) == N_PROMPT_SCAN_INPUTS and len(sample_specs) == N_SAMPLE_SCAN_INPUTS
    kern = functools.partial(_ssd_scan_kernel, n_chunks=nc, n_groups=n_groups, seq_len_s=seq_len_s,
                             head_dim=head_dim, d_state=d_state)
    return pl.pallas_call(
        kern,
        grid=(n_seq_p * nc,),
        in_specs=prompt_specs + sample_specs,
        out_specs=[
            pl.BlockSpec((q, d_inner), lambda t: (t, 0)),
            pl.BlockSpec((d_inner, d_state), lambda t: (t // nc, 0)),
            pl.BlockSpec((q, gw), lambda t: (sb(t), g(t))),
            pl.BlockSpec((spb, None, gw, d_state), lambda t: (sb(t), g(t), 0, 0)),
        ],
        out_shape=[
            jax.ShapeDtypeStruct((n_seq_p * seq_len_p, d_inner), BF16),
            jax.ShapeDtypeStruct((n_seq_p * d_inner, d_state), F32),
            jax.ShapeDtypeStruct((n_rows_s, d_inner), BF16),
            jax.ShapeDtypeStruct(state.shape, F32),
        ],
        scratch_shapes=[pltpu.VMEM((d_state, d_inner), F32)],
        compiler_params=_params("arbitrary"),
        name="ssd_scan",
    )(xbc_p, xbc_p, xbc_p, zs_p, dt, dtT, alog_row, alog_col, expand, d_x, ng,
      xbc_s, xbc_s, xbc_s, zs_s, dt_g, dtT_g, alog_row_g, alog_col_g, expand[:, :gw], d_x, ng, state)


def _prompt_conv_state(tail, *, n_prompt_tiles, tiles_per_seq, km1):
    t = tail.reshape(-1, V7X_SUBLANES, tail.shape[1])[:n_prompt_tiles]
    return t[tiles_per_seq - 1::tiles_per_seq, V7X_SUBLANES - km1:, :]


def kernel(x_prompt, x_sample, p_prompt, p_sample, state_sc_conv, state_ssd_conv, state_ssd, g_mix, g_ffn, g_ple, g_final, sc_w_in, sc_w_conv, sc_w_out, ssd_w_in, ssd_conv_w, ssd_conv_b, ssd_dt_bias, ssd_a_log, ssd_d, ssd_norm_g, ssd_w_out, ffn_w_gate, ffn_w_up, ffn_w_down, ple_w_proj, ple_w_gate):
    bp, lp, d = x_prompt.shape
    bs, ls, _ = x_sample.shape
    depth = g_mix.shape[0]
    mp, ms = bp * lp, bs * ls
    pdim = p_prompt.shape[-1]
    n_heads, head_dim, d_state = state_ssd.shape[2:]
    d_inner = n_heads * head_dim
    conv_dim = ssd_conv_w.shape[-1]
    n_groups = (conv_dim - d_inner) // (2 * d_state)
    hpg = n_heads // n_groups
    assert n_heads <= V7X_LANES and V7X_LANES % head_dim == 0 and d_state == V7X_LANES
    npt = mp // ROW_TILE
    tps = lp // ROW_TILE
    row = lambda v: v.reshape(1, -1)
    pp = p_prompt.reshape(depth, mp, pdim)
    ps = p_sample.reshape(depth, ms, pdim)

    h = (x_prompt.reshape(mp, d), x_sample.reshape(ms, d))
    sc_p, sc_s, cv_p, cv_s, st_p, st_s = [], [], [], [], [], []
    y_out = None
    for i in range(depth):
        j = i // 2
        if i % 2 == 0:
            km1 = sc_w_conv.shape[1] - 1
            if not isinstance(h, tuple):
                h = (h[:mp], h[mp:])
            gated, tail, nstate = _short_conv_in(
                *h, row(g_mix[i]), sc_w_in, j, sc_w_conv, jnp.swapaxes(state_sc_conv, 1, 2),
                seq_len_p=lp, seq_len_s=ls)
            sc_p.append(_prompt_conv_state(tail, n_prompt_tiles=npt, tiles_per_seq=tps, km1=km1))
            sc_s.append(jnp.swapaxes(nstate, 0, 1))
            h = _matmul_residual(h, gated, sc_w_out, j, n_prompt_rows=mp)
        else:
            if isinstance(h, tuple):
                h = jnp.concatenate(h)
            km1 = ssd_conv_w.shape[1] - 1
            zx = d_inner + conv_dim
            pad_h = V7X_LANES - n_heads
            w_in_t = jnp.swapaxes(ssd_w_in, 1, 2)
            w_dt_t = jnp.pad(w_in_t[j, zx:, :], ((0, pad_h), (0, 0))).astype(BF16)
            dt_b = jnp.pad(ssd_dt_bias[j], (0, pad_h))
            alog = jnp.pad(ssd_a_log[j], (0, pad_h))
            (zs, xbc, dt, dtT), tail, nstate = _ssd_in(
                h, row(g_mix[i]), w_in_t, j, w_dt_t, row(dt_b),
                ssd_conv_w, ssd_conv_b.reshape(ssd_conv_b.shape[0], 1, conv_dim),
                jnp.swapaxes(state_ssd_conv, 1, 2),
                n_prompt_rows=mp, seq_len_p=lp, seq_len_s=ls, d_inner=d_inner)
            cv_p.append(_prompt_conv_state(tail, n_prompt_tiles=npt, tiles_per_seq=tps, km1=km1))
            cv_s.append(jnp.swapaxes(nstate, 0, 1))
            head_of_lane = jnp.arange(d_inner, dtype=jnp.int32) // head_dim
            expand = (jnp.arange(V7X_LANES, dtype=jnp.int32)[:, None] == head_of_lane[None, :]).astype(BF16)
            d_x = row(jnp.repeat(ssd_d[j], head_dim))
            ng = row(ssd_norm_g[j])
            dt_g = jnp.stack([jnp.roll(dt[1], -g * hpg, axis=1) for g in range(n_groups)])
            alog_g = jnp.stack([jnp.roll(alog, -g * hpg) for g in range(n_groups)])
            dtT_g = dtT[1][:n_heads].reshape(n_groups, hpg, ms)
            y_p, new_p, y_s, new_s = _ssd_scan(
                xbc[0], zs[0], dt[0], dtT[0], row(alog), alog.reshape(-1, 1), expand, d_x, ng,
                xbc[1], zs[1], dt_g, dtT_g, alog_g.reshape(n_groups, 1, -1),
                ssd_a_log[j].reshape(n_groups, hpg, 1),
                state_ssd[j].reshape(bs, n_groups, hpg * head_dim, d_state),
                n_seq_p=bp, seq_len_p=lp, seq_len_s=ls,
                d_inner=d_inner, n_groups=n_groups, head_dim=head_dim, d_state=d_state)
            st_p.append(new_p.reshape(bp, n_heads, head_dim, d_state))
            st_s.append(new_s.reshape(bs, n_heads, head_dim, d_state))
            h = _matmul_residual(h, (y_p, y_s), ssd_w_out, j, n_prompt_rows=mp)
        h = _ffn(h, row(g_ffn[i]), ffn_w_gate, ffn_w_up, ffn_w_down, i)
        ple_args = (h, pp, ps, row(g_ple[i]), ple_w_gate, ple_w_proj, i)
        if i == depth - 1:
            y_out = _ple(*ple_args, g_final=row(g_final))
        else:
            h = _ple(*ple_args)
    y_p, y_s = y_out
    return (y_p.reshape(bp, lp, d), y_s.reshape(bs, ls, d), jnp.stack(sc_p), jnp.stack(sc_s),
            jnp.stack(cv_p), jnp.stack(cv_s), jnp.stack(st_p), jnp.stack(st_s))
```

```python
import functools

import jax
import jax.numpy as jnp
from jax import lax
from jax.experimental import pallas as pl
from jax.experimental.pallas import tpu as pltpu

F32 = jnp.float32
BF16 = jnp.bfloat16
EPS = 1e-6
MASKED = -1e30
V7X_LANES = 128
V7X_SUBLANES = 8
HISTORY_ROWS = 16
V7X_VMEM_LIMIT = 56 * 1024 * 1024

ROW_TILE = 1024
HALF_ROW_TILE = 512
PLE_ROW_TILE = 256
COL_TILE = 512
NARROW_COL_TILE = 256
SSD_CHUNK = 128
WEIGHT_SLAB_BYTES = 16 * 1024 * 1024


def _params(*sem):
    return pltpu.CompilerParams(dimension_semantics=sem, vmem_limit_bytes=V7X_VMEM_LIMIT)


def _dot(a, b):
    return jnp.dot(a, b, preferred_element_type=F32)


def _dot_nt(a, b):
    return lax.dot_general(a, b, (((1,), (1,)), ((), ())), preferred_element_type=F32)


def _dot_tn(a, b):
    return lax.dot_general(a, b, (((0,), (0,)), ((), ())), preferred_element_type=F32)


def _split3(a):
    a1 = a.astype(BF16)
    r1 = a - a1.astype(F32)
    a2 = r1.astype(BF16)
    a3 = (r1 - a2.astype(F32)).astype(BF16)
    return a3, a2, a1


def _dot01_rhs(a, e):
    p3, p2, p1 = _split3(a)
    return (_dot(p3, e) + _dot(p2, e)) + _dot(p1, e)


def _spread(a, e):
    hi = a.astype(BF16)
    lo = (a - hi.astype(F32)).astype(BF16)
    return _dot(lo, e) + _dot(hi, e)


def _dot01_lhs(t, a):
    p3, p2, p1 = _split3(a)
    return (_dot(t, p3) + _dot(t, p2)) + _dot(t, p1)


def _rmsnorm(x, g):
    ms = jnp.mean(x * x, axis=-1, keepdims=True)
    return x * lax.rsqrt(ms + EPS) * g


def _softplus(x):
    return jnp.maximum(x, 0.0) + jnp.log1p(jnp.exp(-jnp.abs(x)))


def _silu(x):
    return x * jax.nn.sigmoid(x)


def _layer_spec(block, layer, imap):
    return pl.BlockSpec((None,) + tuple(block), lambda *a: (layer,) + tuple(imap(*a)))


def _conv(hist_ref, taps, seq_len=None, row0=0, n_rows=None):
    k = taps.shape[0]
    if n_rows is None:
        n_rows = hist_ref.shape[0] - HISTORY_ROWS
    out = taps[k - 1:k, :] * hist_ref[pl.ds(HISTORY_ROWS + row0, n_rows), :]
    if seq_len is not None:
        assert row0 % seq_len == 0
        t = lax.broadcasted_iota(jnp.int32, out.shape, 0) % seq_len
    for d in range(1, k):
        sh = hist_ref[pl.ds(HISTORY_ROWS + row0 - d, n_rows), :]
        if seq_len is not None:
            sh = jnp.where(t >= d, sh, 0.0)
        out = out + taps[k - 1 - d:k - d, :] * sh
    return out


def _state_correction(buf_ref, taps, stage_sc, seq_len):
    k = taps.shape[0]
    km1 = k - 1
    n_seq = stage_sc.shape[1] // seq_len
    stage_sc[...] = jnp.zeros_like(stage_sc)
    rows = [buf_ref[r] for r in range(km1)]
    for t in range(km1):
        acc = None
        for d in range(t + 1, k):
            term = taps[k - 1 - d:k - d, :] * rows[km1 + t - d]
            acc = term if acc is None else acc + term
        for c in range(stage_sc.shape[0]):
            stage_sc[c, pl.ds(t, n_seq, stride=seq_len), :] = acc[:, c * V7X_LANES:(c + 1) * V7X_LANES]


def _staged(stage_sc):
    return jnp.concatenate([stage_sc[c] for c in range(stage_sc.shape[0])], axis=1)


def _emit_sample_state(u, stage_sc, nstate_ref, seq_len):
    km1 = nstate_ref.shape[0]
    n_chunks = stage_sc.shape[0]
    n_seq = stage_sc.shape[1] // seq_len
    for c in range(n_chunks):
        stage_sc[c] = u[:, c * V7X_LANES:(c + 1) * V7X_LANES]
    for r in range(km1):
        nstate_ref[r] = jnp.concatenate(
            [stage_sc[c, pl.ds(seq_len - km1 + r, n_seq, stride=seq_len), :] for c in range(n_chunks)],
            axis=1)


def _sc_in_sample_kernel(x_ref, g_ref, wb_ref, wc_ref, wv_ref, taps_ref, buf_ref,
                         gated_ref, nstate_ref, wbb_ref, wcb_ref, wvb_ref, hn_sc, hist_sc, stage_sc,
                         *, sample_len):
    @pl.when(pl.program_id(0) == 0)
    def _():
        hn_sc[...] = _rmsnorm(x_ref[...], g_ref[...]).astype(BF16)

    wb, wc, wv = wb_ref[...].astype(BF16), wc_ref[...].astype(BF16), wv_ref[...].astype(BF16)
    wbb_ref[...], wcb_ref[...], wvb_ref[...] = wb, wc, wv
    taps = taps_ref[...]
    _state_correction(buf_ref, taps, stage_sc, sample_len)
    hn = hn_sc[...]
    bg = _dot(hn, wb)
    u = _dot(hn, wc) * _dot(hn, wv)
    hist_sc[:HISTORY_ROWS, :] = jnp.zeros((HISTORY_ROWS, u.shape[1]), F32)
    hist_sc[HISTORY_ROWS:, :] = u
    gated_ref[...] = (bg * (_conv(hist_sc, taps, sample_len) + _staged(stage_sc))).astype(BF16)
    _emit_sample_state(u, stage_sc, nstate_ref, sample_len)


def _sc_in_prompt_kernel(x_ref, g_ref, wb_ref, wc_ref, wv_ref, taps_ref,
                         gated_ref, tail_ref, hn_sc, carry_sc, hist_sc, *, tiles_per_seq):
    i = pl.program_id(0)
    j = pl.program_id(1)

    @pl.when(j == 0)
    def _():
        hn_sc[...] = _rmsnorm(x_ref[...], g_ref[...]).astype(BF16)

    hn = hn_sc[...]
    bg = _dot(hn, wb_ref[...])
    u = _dot(hn, wc_ref[...]) * _dot(hn, wv_ref[...])
    hist_sc[:HISTORY_ROWS, :] = jnp.where(i % tiles_per_seq == 0, 0.0, carry_sc[j])
    hist_sc[HISTORY_ROWS:, :] = u
    gated_ref[...] = (bg * _conv(hist_sc, taps_ref[...])).astype(BF16)
    carry_sc[j] = u[u.shape[0] - HISTORY_ROWS:, :]
    tail_ref[...] = u[u.shape[0] - V7X_SUBLANES:, :]


def _short_conv_in(xp, xs, g, w_in, layer, taps, buf, *, seq_len_p, seq_len_s):
    mp, d = xp.shape
    ms = xs.shape[0]
    k = taps.shape[1]
    tm = ROW_TILE
    assert mp % tm == 0 and ms == tm and seq_len_p % tm == 0 and tm % seq_len_s == 0
    assert k - 1 <= min(seq_len_s, V7X_SUBLANES)
    n_seq_s = ms // seq_len_s

    tn = NARROW_COL_TILE
    nj = d // tn
    assert d % tn == 0 and tn % V7X_LANES == 0
    gated_s, nstate, wb, wc, wv = pl.pallas_call(
        functools.partial(_sc_in_sample_kernel, sample_len=seq_len_s),
        grid=(nj,),
        in_specs=[
            pl.BlockSpec((tm, d), lambda j: (0, 0)),
            pl.BlockSpec((1, d), lambda j: (0, 0)),
            _layer_spec((d, tn), layer, lambda j: (0, j)),
            _layer_spec((d, tn), layer, lambda j: (0, nj + j)),
            _layer_spec((d, tn), layer, lambda j: (0, 2 * nj + j)),
            _layer_spec((k, tn), layer, lambda j: (0, j)),
            _layer_spec((k - 1, n_seq_s, tn), layer, lambda j: (0, 0, j)),
        ],
        out_specs=[pl.BlockSpec((tm, tn), lambda j: (0, j)),
                   pl.BlockSpec((k - 1, n_seq_s, tn), lambda j: (0, 0, j))]
        + [pl.BlockSpec((d, tn), lambda j: (0, j))] * 3,
        out_shape=[jax.ShapeDtypeStruct((ms, d), BF16), jax.ShapeDtypeStruct((k - 1, n_seq_s, d), F32)]
        + [jax.ShapeDtypeStruct((d, d), BF16)] * 3,
        scratch_shapes=[pltpu.VMEM((tm, d), BF16), pltpu.VMEM((HISTORY_ROWS + tm, tn), F32),
                        pltpu.VMEM((tn // V7X_LANES, tm, V7X_LANES), F32)],
        compiler_params=_params("arbitrary"),
        name="short_conv_in_sample",
    )(xs, g, w_in, w_in, w_in, taps, buf)

    tn = COL_TILE
    nj = d // tn
    assert d % tn == 0
    npt = mp // tm
    gated_p, tail = pl.pallas_call(
        functools.partial(_sc_in_prompt_kernel, tiles_per_seq=seq_len_p // tm),
        grid=(npt, nj),
        in_specs=[
            pl.BlockSpec((tm, d), lambda i, j: (i, 0)),
            pl.BlockSpec((1, d), lambda i, j: (0, 0)),
            pl.BlockSpec((d, tn), lambda i, j: (0, j)),
            pl.BlockSpec((d, tn), lambda i, j: (0, j)),
            pl.BlockSpec((d, tn), lambda i, j: (0, j)),
            _layer_spec((k, tn), layer, lambda i, j: (0, j)),
        ],
        out_specs=[pl.BlockSpec((tm, tn), lambda i, j: (i, j)),
                   pl.BlockSpec((V7X_SUBLANES, tn), lambda i, j: (i, j))],
        out_shape=[jax.ShapeDtypeStruct((mp, d), BF16),
                   jax.ShapeDtypeStruct((npt * V7X_SUBLANES, d), F32)],
        scratch_shapes=[pltpu.VMEM((tm, d), BF16), pltpu.VMEM((nj, HISTORY_ROWS, tn), F32),
                        pltpu.VMEM((HISTORY_ROWS + tm, tn), F32)],
        compiler_params=_params("arbitrary", "arbitrary"),
        name="short_conv_in_prompt",
    )(xp, g, wb, wc, wv, taps)
    return (gated_p, gated_s), tail, nstate


def _res_kernel(*refs, n_prompt_tiles, res_pair, a_pair):
    refs = list(refs)
    res_refs = [refs.pop(0) for _ in range(2 if res_pair else 1)]
    a_refs = [refs.pop(0) for _ in range(2 if a_pair else 1)]
    w_ref, out_ref, wb_sc = refs
    i = pl.program_id(1)

    @pl.when(i == 0)
    def _():
        wb_sc[...] = w_ref[...].astype(BF16)

    def body(which):
        out_ref[...] = res_refs[which * res_pair][...] + _dot(a_refs[which * a_pair][...], wb_sc[...])

    if not (res_pair or a_pair):
        body(0)
        return
    pl.when(i < n_prompt_tiles)(lambda: body(0))
    pl.when(i >= n_prompt_tiles)(lambda: body(1))


def _matmul_residual(res, a, w, layer, *, n_prompt_rows):
    res_pair, a_pair = isinstance(res, tuple), isinstance(a, tuple)
    _, k, n = w.shape
    m = sum(r.shape[0] for r in res) if res_pair else res.shape[0]
    tm = HALF_ROW_TILE
    tn = min(n, WEIGHT_SLAB_BYTES // (4 * k))
    assert n_prompt_rows % tm == 0 and m % tm == 0 and n % tn == 0 and tn % V7X_LANES == 0
    npt = n_prompt_rows // tm

    def row_specs(pair, block, col_of):
        if not pair:
            return [pl.BlockSpec(block, lambda j, i: (i, col_of(j)))]
        return [pl.BlockSpec(block, lambda j, i: (jnp.minimum(i, npt - 1), col_of(j))),
                pl.BlockSpec(block, lambda j, i: (jnp.maximum(i - npt, 0), col_of(j)),
                             pipeline_mode=pl.Buffered(1))]

    return pl.pallas_call(
        functools.partial(_res_kernel, n_prompt_tiles=npt, res_pair=res_pair, a_pair=a_pair),
        grid=(n // tn, m // tm),
        in_specs=row_specs(res_pair, (tm, tn), lambda j: j) + row_specs(a_pair, (tm, k), lambda j: 0)
        + [pl.BlockSpec((None, k, tn), lambda j, i: (layer, 0, j), pipeline_mode=pl.Buffered(1))],
        out_specs=pl.BlockSpec((tm, tn), lambda j, i: (i, j)),
        out_shape=jax.ShapeDtypeStruct((m, n), F32),
        scratch_shapes=[pltpu.VMEM((k, tn), BF16)],
        compiler_params=_params("arbitrary", "arbitrary"),
        name="matmul_residual",
    )(*(res if res_pair else (res,)), *(a if a_pair else (a,)), w)


def _ffn_kernel(h_ref, g_ref, wg_ref, wu_ref, wd_ref, out_ref, *rest):
    *cast_refs, hn_sc = rest

    @pl.when(pl.program_id(1) == 0)
    def _():
        h = h_ref[...]
        hn_sc[...] = _rmsnorm(h, g_ref[...]).astype(BF16)
        out_ref[...] = h

    wg, wu, wd = wg_ref[...], wu_ref[...], wd_ref[...]
    if cast_refs:
        wg, wu, wd = wg.astype(BF16), wu.astype(BF16), wd.astype(BF16)
        for ref, w in zip(cast_refs, (wg, wu, wd)):
            ref[...] = w
    hn = hn_sc[...]
    gate = _dot(hn, wg)
    act = (_silu(gate) * _dot(hn, wu)).astype(BF16)
    out_ref[...] += _dot(act, wd)


def _ffn(h, g, w_gate, w_up, w_down, layer, *, n_prompt_rows):
    m, d = h.shape
    f = w_gate.shape[2]
    tm = ROW_TILE
    assert n_prompt_rows % tm == 0 and m - n_prompt_rows == tm
    assert f % COL_TILE == 0 and f % NARROW_COL_TILE == 0
    npt = n_prompt_rows // tm

    def call(row0_tile, n_row_tiles, tf, weights, weight_specs, cast_outputs, name):
        out_specs = [pl.BlockSpec((tm, d), lambda i, j: (i, 0))]
        out_shape = [jax.ShapeDtypeStruct((n_row_tiles * tm, d), F32)]
        if cast_outputs:
            out_specs += [pl.BlockSpec((d, tf), lambda i, j: (0, j)), pl.BlockSpec((d, tf), lambda i, j: (0, j)),
                          pl.BlockSpec((tf, d), lambda i, j: (j, 0))]
            out_shape += [jax.ShapeDtypeStruct((d, f), BF16), jax.ShapeDtypeStruct((d, f), BF16),
                          jax.ShapeDtypeStruct((f, d), BF16)]
        h_mode = dict(pipeline_mode=pl.Buffered(1)) if n_row_tiles == 1 else {}
        return pl.pallas_call(
            _ffn_kernel,
            grid=(n_row_tiles, f // tf),
            in_specs=[pl.BlockSpec((tm, d), lambda i, j: (i + row0_tile, 0), **h_mode),
                      pl.BlockSpec((1, d), lambda i, j: (0, 0))] + weight_specs(tf),
            out_specs=out_specs,
            out_shape=out_shape,
            scratch_shapes=[pltpu.VMEM((tm, d), BF16)],
            compiler_params=_params("arbitrary", "arbitrary"),
            name=name,
        )(h, g, *weights)

    f32_specs = lambda tf: [_layer_spec((d, tf), layer, lambda i, j: (0, j)),
                            _layer_spec((d, tf), layer, lambda i, j: (0, j)),
                            _layer_spec((tf, d), layer, lambda i, j: (j, 0))]
    bf16_specs = lambda tf: [pl.BlockSpec((d, tf), lambda i, j: (0, j)), pl.BlockSpec((d, tf), lambda i, j: (0, j)),
                             pl.BlockSpec((tf, d), lambda i, j: (j, 0))]
    sample, wg_b, wu_b, wd_b = call(npt, 1, NARROW_COL_TILE, (w_gate, w_up, w_down), f32_specs, True,
                                    "swiglu_ffn_sample")
    prompt, = call(0, npt, COL_TILE, (wg_b, wu_b, wd_b), bf16_specs, False, "swiglu_ffn_prompt")
    return prompt, sample


def _ple_kernel(h_ref, p_ref, g_ref, wg_ref, wp_ref, *rest, cast, final):
    if cast:
        *rest, wgb_ref, wpb_ref = rest

        @pl.when(pl.program_id(0) == 0)
        def _():
            wgb_ref[...] = wg_ref[...].astype(BF16)
            wpb_ref[...] = wp_ref[...].astype(BF16)
    else:
        wgb_ref, wpb_ref = wg_ref, wp_ref
    h = h_ref[...]
    gate = jax.nn.sigmoid(_dot(_rmsnorm(h, g_ref[...]).astype(BF16), wgb_ref[...]))
    out = h + _dot(p_ref[...].astype(BF16), wpb_ref[...]) * gate
    if final:
        gf_ref, out_ref = rest
        out_ref[...] = _rmsnorm(out, gf_ref[...])
    else:
        rest[0][...] = out


def _ple(h, pp, ps, g, w_gate, w_proj, layer, g_final=None):
    hp, hs = h
    d = hp.shape[1]
    pdim = pp.shape[2]
    final = g_final is not None
    const = lambda i: (0, 0)

    def call(rows, p, tm, weights, weight_specs, cast, name):
        m = rows.shape[0]
        assert m % tm == 0
        in_specs = [pl.BlockSpec((tm, d), lambda i: (i, 0)),
                    _layer_spec((tm, pdim), layer, lambda i: (i, 0)),
                    pl.BlockSpec((1, d), const)] + weight_specs
        args = [rows, p, g, *weights]
        if final:
            in_specs.append(pl.BlockSpec((1, d), const))
            args.append(g_final)
        out_specs = [pl.BlockSpec((tm, d), lambda i: (i, 0))]
        out_shape = [jax.ShapeDtypeStruct((m, d), F32)]
        if cast:
            out_specs += [pl.BlockSpec((d, d), const), pl.BlockSpec((pdim, d), const)]
            out_shape += [jax.ShapeDtypeStruct((d, d), BF16), jax.ShapeDtypeStruct((pdim, d), BF16)]
        return pl.pallas_call(
            functools.partial(_ple_kernel, cast=cast, final=final),
            grid=(m // tm,),
            in_specs=in_specs,
            out_specs=out_specs,
            out_shape=out_shape,
            compiler_params=_params("arbitrary"),
            name=name,
        )(*args)

    resident = dict(pipeline_mode=pl.Buffered(1))
    f32_specs = [pl.BlockSpec((None, d, d), lambda i: (layer, 0, 0), **resident),
                 pl.BlockSpec((None, pdim, d), lambda i: (layer, 0, 0), **resident)]
    bf16_specs = [pl.BlockSpec((d, d), const, **resident), pl.BlockSpec((pdim, d), const, **resident)]
    tag = "ple_final" if final else "ple"
    out_s, wg_b, wp_b = call(hs, ps, PLE_ROW_TILE, (w_gate, w_proj), f32_specs, True, tag + "_sample")
    out_p, = call(hp, pp, HALF_ROW_TILE, (wg_b, wp_b), bf16_specs, False, tag + "_prompt")
    return out_p, out_s


def _ssd_in_kernel(h_ref, g_ref, wt_ref, wdtT_ref, dtb_ref, taps_ref, cb_ref, *rest,
                   sample_len, tiles_per_seq, n_z_tiles):
    if sample_len is not None:
        (buf_ref, zs_ref, xbc_ref, nstate_ref, dt_ref, dtT_ref, wtb_ref,
         hn_sc, hist_sc, stage_sc) = rest
        i, j = 0, pl.program_id(0)
    else:
        zs_ref, xbc_ref, tail_ref, dt_ref, dtT_ref, hn_sc, carry_sc, hist_sc = rest
        i, j = pl.program_id(0), pl.program_id(1)
    jc = j - n_z_tiles
    tm = hn_sc.shape[0]

    @pl.when(j == 0)
    def _():
        hn = _rmsnorm(h_ref[...], g_ref[...]).astype(BF16)
        hn_sc[...] = hn
        dt = _softplus(_dot_nt(hn, wdtT_ref[...]) + dtb_ref[...])
        dt_ref[...] = dt
        dtT_ref[...] = dt.T

    def weight_tile():
        if sample_len is None:
            return wt_ref[...]
        wb = wt_ref[...].astype(BF16)
        wtb_ref[...] = wb
        return wb

    def finish(conv):
        return _silu(conv + cb_ref[...]).astype(BF16)

    @pl.when(j < n_z_tiles)
    def _():
        zs_ref[...] = _silu(_dot_nt(hn_sc[...], weight_tile())).astype(BF16)

    @pl.when(j >= n_z_tiles)
    def _():
        hist_sc[HISTORY_ROWS:, :] = _dot_nt(hn_sc[...], weight_tile())
        taps = taps_ref[...]
        if sample_len is None:
            hist_sc[:HISTORY_ROWS, :] = jnp.where(i % tiles_per_seq == 0, 0.0, carry_sc[jc])
            xbc_ref[...] = finish(_conv(hist_sc, taps))
            carry_sc[jc] = hist_sc[tm:, :]
            tail_ref[...] = hist_sc[HISTORY_ROWS + tm - V7X_SUBLANES:, :]
        else:
            _state_correction(buf_ref, taps, stage_sc, sample_len)
            hist_sc[:HISTORY_ROWS, :] = jnp.zeros((HISTORY_ROWS, hist_sc.shape[1]), F32)
            xbc_ref[...] = finish(_conv(hist_sc, taps, sample_len) + _staged(stage_sc))
            _emit_sample_state(hist_sc[HISTORY_ROWS:, :], stage_sc, nstate_ref, sample_len)


def _ssd_in(h_p, h_s, g, w_in_t, layer, w_dt_t, dt_b, taps, conv_b, buf, *, seq_len_p, seq_len_s, d_inner):
    n_prompt_rows, d = h_p.shape
    ms = h_s.shape[0]
    _, k, conv_dim = taps.shape
    hp = w_dt_t.shape[0]
    tm, tn = ROW_TILE, COL_TILE
    assert n_prompt_rows % tm == 0 and ms == tm and seq_len_p % tm == 0 and tm % seq_len_s == 0
    assert d_inner % tn == 0 and conv_dim % tn == 0 and k - 1 <= min(seq_len_s, V7X_SUBLANES)
    npt = n_prompt_rows // tm
    n_seq_s = ms // seq_len_s
    nz, nc = d_inner // tn, conv_dim // tn
    zc = lambda j: jnp.minimum(j, nz - 1)
    cc = lambda j: jnp.maximum(j - nz, 0)

    zs_s, xbc_s, nstate, dt_s, dtT_s, w_b = pl.pallas_call(
        functools.partial(_ssd_in_kernel, sample_len=seq_len_s, tiles_per_seq=None, n_z_tiles=nz),
        grid=(nz + nc,),
        in_specs=[
            pl.BlockSpec((tm, d), lambda j: (0, 0)),
            pl.BlockSpec((1, d), lambda j: (0, 0)),
            _layer_spec((tn, d), layer, lambda j: (j, 0)),
            pl.BlockSpec((hp, d), lambda j: (0, 0)),
            pl.BlockSpec((1, hp), lambda j: (0, 0)),
            _layer_spec((k, tn), layer, lambda j: (0, cc(j))),
            _layer_spec((1, tn), layer, lambda j: (0, cc(j))),
            _layer_spec((k - 1, n_seq_s, tn), layer, lambda j: (0, 0, cc(j))),
        ],
        out_specs=[
            pl.BlockSpec((tm, tn), lambda j: (0, zc(j))),
            pl.BlockSpec((tm, tn), lambda j: (0, cc(j))),
            pl.BlockSpec((k - 1, n_seq_s, tn), lambda j: (0, 0, cc(j))),
            pl.BlockSpec((tm, hp), lambda j: (0, 0)),
            pl.BlockSpec((hp, tm), lambda j: (0, 0)),
            pl.BlockSpec((tn, d), lambda j: (j, 0)),
        ],
        out_shape=[
            jax.ShapeDtypeStruct((ms, d_inner), BF16),
            jax.ShapeDtypeStruct((ms, conv_dim), BF16),
            jax.ShapeDtypeStruct((k - 1, n_seq_s, conv_dim), F32),
            jax.ShapeDtypeStruct((ms, hp), F32),
            jax.ShapeDtypeStruct((hp, ms), F32),
            jax.ShapeDtypeStruct(((nz + nc) * tn, d), BF16),
        ],
        scratch_shapes=[pltpu.VMEM((tm, d), BF16), pltpu.VMEM((HISTORY_ROWS + tm, tn), F32),
                        pltpu.VMEM((tn // V7X_LANES, tm, V7X_LANES), F32)],
        compiler_params=_params("arbitrary"),
        name="ssd_in_sample",
    )(h_s, g, w_in_t, w_dt_t, dt_b, taps, conv_b, buf)

    const = lambda i, j: (0, 0)
    zs_p, xbc_p, tail, dt_p, dtT_p = pl.pallas_call(
        functools.partial(_ssd_in_kernel, sample_len=None, tiles_per_seq=seq_len_p // tm, n_z_tiles=nz),
        grid=(npt, nz + nc),
        in_specs=[
            pl.BlockSpec((tm, d), lambda i, j: (i, 0)),
            pl.BlockSpec((1, d), const),
            pl.BlockSpec((tn, d), lambda i, j: (j, 0)),
            pl.BlockSpec((hp, d), const),
            pl.BlockSpec((1, hp), const),
            _layer_spec((k, tn), layer, lambda i, j: (0, cc(j))),
            _layer_spec((1, tn), layer, lambda i, j: (0, cc(j))),
        ],
        out_specs=[
            pl.BlockSpec((tm, tn), lambda i, j: (i, zc(j))),
            pl.BlockSpec((tm, tn), lambda i, j: (i, cc(j))),
            pl.BlockSpec((V7X_SUBLANES, tn), lambda i, j: (i, cc(j))),
            pl.BlockSpec((tm, hp), lambda i, j: (i, 0)),
            pl.BlockSpec((hp, tm), lambda i, j: (0, i)),
        ],
        out_shape=[
            jax.ShapeDtypeStruct((n_prompt_rows, d_inner), BF16),
            jax.ShapeDtypeStruct((n_prompt_rows, conv_dim), BF16),
            jax.ShapeDtypeStruct((npt * V7X_SUBLANES, conv_dim), F32),
            jax.ShapeDtypeStruct((n_prompt_rows, hp), F32),
            jax.ShapeDtypeStruct((hp, n_prompt_rows), F32),
        ],
        scratch_shapes=[pltpu.VMEM((tm, d), BF16), pltpu.VMEM((nc, HISTORY_ROWS, tn), F32),
                        pltpu.VMEM((HISTORY_ROWS + tm, tn), F32)],
        compiler_params=_params("arbitrary", "arbitrary"),
        name="ssd_in_prompt",
    )(h_p, g, w_b, w_dt_t, dt_b, taps, conv_b)
    return ((zs_p, zs_s), (xbc_p, xbc_s), (dt_p, dt_s), (dtT_p, dtT_s)), tail, nstate


def _ssd_group_out(x, zs, cb, acum, acumT, dtT, mask, extra, ng, head0, heads_per_group, head_dim,
                   carried=None):
    heads_per_slab = V7X_LANES // head_dim
    parts = []
    for q in range(heads_per_group // heads_per_slab):
        cols = slice(q * V7X_LANES, (q + 1) * V7X_LANES)
        rhs = x[:, cols]
        if carried is not None:
            cg, st_t = carried
            rhs = jnp.concatenate([rhs, st_t[:, cols].astype(BF16)], axis=0)
        lanes = lax.broadcasted_iota(jnp.int32, rhs.shape, 1)
        acc = None
        for r in range(heads_per_slab):
            hd = head0 + q * heads_per_slab + r
            a_t = jnp.broadcast_to(acum[:, hd:hd + 1], cb.shape)
            seg = a_t - acumT[hd:hd + 1, :]
            lhs = (cb * jnp.exp(jnp.where(mask, seg, MASKED)) * dtT[hd:hd + 1, :]).astype(BF16)
            if carried is not None:
                lhs = jnp.concatenate([lhs, (cg * jnp.exp(a_t)).astype(BF16)], axis=1)
            in_head = jnp.logical_and(lanes >= r * head_dim, lanes < (r + 1) * head_dim)
            part = _dot(lhs, jnp.where(in_head, rhs, jnp.zeros_like(rhs)))
            acc = part if acc is None else acc + part
        parts.append(acc)
    y = jnp.concatenate(parts, axis=1) + extra
    gated = y * zs
    ms = jnp.mean(gated * gated, axis=-1, keepdims=True)
    return (gated * lax.rsqrt(ms + EPS) * ng).astype(BF16)


def _ssd_prompt_body(c, n_chunks, xs_ref, b_ref, c_ref, zs_ref, dt_ref, dtT_ref, alr_ref, alc_ref,
                     e_ref, d_ref, ng_ref, y_ref, state_ref, st_sc, *, n_groups, head_dim, d_state):
    @pl.when(c == 0)
    def _():
        st_sc[...] = jnp.zeros_like(st_sc)

    q_rows, d_inner = xs_ref.shape
    gw = d_inner // n_groups
    hpg = gw // head_dim
    row = lax.broadcasted_iota(jnp.int32, (q_rows, q_rows), 0)
    col = lax.broadcasted_iota(jnp.int32, (q_rows, q_rows), 1)
    causal = col <= row
    tril = jnp.where(causal, 1.0, 0.0).astype(BF16)
    triu = jnp.where(row <= col, 1.0, 0.0).astype(BF16)
    dt = dt_ref[...]
    dtT = dtT_ref[...]
    acum = _dot01_lhs(tril, dt * -jnp.exp(alr_ref[...]))
    acumT = _dot01_rhs(dtT * -jnp.exp(alc_ref[...]), triu)
    a_end = acum[q_rows - 1:q_rows, :]
    e = e_ref[...]
    x = xs_ref[...]
    xf = x.astype(F32)
    to_end = (xf * _dot((dt * jnp.exp(a_end - acum)).astype(BF16), e)).astype(BF16)
    decay = _dot01_rhs(jnp.broadcast_to(jnp.exp(a_end), (V7X_SUBLANES, a_end.shape[1])), e)[:1, :]
    skip = xf * d_ref[...]
    for g in range(n_groups):
        sl = slice(g * gw, (g + 1) * gw)
        ns = slice(g * d_state, (g + 1) * d_state)
        bg, cg = b_ref[:, ns], c_ref[:, ns]
        st = st_sc[:, sl]
        y_ref[:, sl] = _ssd_group_out(x[:, sl], zs_ref[:, sl].astype(F32), _dot_nt(cg, bg), acum,
                                      acumT, dtT, causal, skip[:, sl], ng_ref[:, sl],
                                      g * hpg, hpg, head_dim,
                                      carried=(cg.astype(F32), st))
        st_sc[:, sl] = decay[:, sl] * st + _dot_tn(bg, to_end[:, sl])

    @pl.when(c == n_chunks - 1)
    def _():
        for g in range(n_groups):
            state_ref[g * gw:(g + 1) * gw, :] = st_sc[:, g * gw:(g + 1) * gw].T


def _ssd_sample_body(xs_ref, b_ref, c_ref, zs_ref, dt_ref, dtT_ref, alr_ref, alc_ref, e_ref,
                     d_ref, ng_ref, st_ref, y_ref, nst_ref, *, seq_len, head_dim):
    q_rows, gw = xs_ref.shape
    n_seq = q_rows // seq_len
    row = lax.broadcasted_iota(jnp.int32, (q_rows, q_rows), 0)
    col = lax.broadcasted_iota(jnp.int32, (q_rows, q_rows), 1)
    same = (row // seq_len) == (col // seq_len)
    mask = jnp.logical_and(same, col <= row)
    tril = jnp.where(mask, 1.0, 0.0).astype(BF16)
    triu = jnp.where(jnp.logical_and(same, row <= col), 1.0, 0.0).astype(BF16)
    ends = jnp.where(col == (row // seq_len) * seq_len + (seq_len - 1), 1.0, 0.0).astype(BF16)
    dt = dt_ref[...]
    dtT = dtT_ref[...]
    acum = _dot01_lhs(tril, dt * -jnp.exp(alr_ref[...]))
    acumT = _dot01_rhs(dtT * -jnp.exp(alc_ref[...]), triu)
    a_end = _dot01_lhs(ends, acum)
    e = e_ref[...]
    x = xs_ref[...].astype(F32)
    to_endT = (x * _dot((dt * jnp.exp(a_end - acum)).astype(BF16), e)).T.astype(BF16)
    decayT = _dot01_rhs(jnp.exp(a_end), e).T
    from_start = _spread(jnp.exp(acum), e)
    bg = b_ref[...].astype(F32)
    cg = c_ref[...].astype(F32)
    seq_of_row = lax.broadcasted_iota(jnp.int32, bg.shape, 0) // seq_len
    inter = jnp.zeros((q_rows, gw), F32)
    for s in range(n_seq):
        st = st_ref[s]
        mine = seq_of_row == s
        inter = inter + _dot_nt(jnp.where(mine, cg, 0.0).astype(BF16), st.astype(BF16))
        bm = jnp.where(mine, bg, 0.0).astype(BF16)
        nst_ref[s] = decayT[:, s * seq_len:s * seq_len + 1] * st + _dot(to_endT, bm)
    inter = from_start * inter + x * d_ref[...]
    y_ref[...] = _ssd_group_out(xs_ref[...], zs_ref[...].astype(F32), _dot_nt(c_ref[...], b_ref[...]),
                                acum, acumT, dtT, mask, inter, ng_ref[...], 0, gw // head_dim, head_dim)


N_PROMPT_SCAN_INPUTS = 11
N_SAMPLE_SCAN_INPUTS = 12


def _ssd_scan_kernel(*refs, n_chunks, n_groups, seq_len_s, head_dim, d_state):
    a, b = N_PROMPT_SCAN_INPUTS, N_PROMPT_SCAN_INPUTS + N_SAMPLE_SCAN_INPUTS
    prompt_in, sample_in = refs[:a], refs[a:b]
    y_p, state_p, y_s, state_s, st_sc = refs[b:]
    _ssd_prompt_body(pl.program_id(0) % n_chunks, n_chunks, *prompt_in, y_p, state_p, st_sc,
                     n_groups=n_groups, head_dim=head_dim, d_state=d_state)
    _ssd_sample_body(*sample_in, y_s, state_s, seq_len=seq_len_s, head_dim=head_dim)


def _ssd_scan(xbc_p, zs_p, dt, dtT, alog_row, alog_col, expand, d_x, ng,
              xbc_s, zs_s, dt_g, dtT_g, alog_row_g, alog_col_g, state,
              *, n_seq_p, seq_len_p, seq_len_s, d_inner, n_groups, head_dim, d_state):
    q = SSD_CHUNK
    n_rows_s = xbc_s.shape[0]
    assert seq_len_p % q == 0 and q == d_state
    assert n_rows_s % q == 0 and q % seq_len_s == 0
    nc = seq_len_p // q
    nb = n_rows_s // q
    assert n_seq_p * nc == nb * n_groups
    spb = q // seq_len_s
    gw = d_inner // n_groups
    hp = dt.shape[1]
    hpg_rows = dtT_g.shape[1]
    gn = n_groups * d_state
    assert d_inner % gn == 0
    b_col0 = d_inner // d_state
    const = lambda t: (0, 0)
    sb = lambda t: t // n_groups
    g = lambda t: t % n_groups
    prompt_specs = [
        pl.BlockSpec((q, d_inner), lambda t: (t, 0)),
        pl.BlockSpec((q, gn), lambda t: (t, d_inner // gn)),
        pl.BlockSpec((q, gn), lambda t: (t, d_inner // gn + 1)),
        pl.BlockSpec((q, d_inner), lambda t: (t, 0)),
        pl.BlockSpec((q, hp), lambda t: (t, 0)),
        pl.BlockSpec((hp, q), lambda t: (0, t)),
        pl.BlockSpec((1, hp), const),
        pl.BlockSpec((hp, 1), const),
        pl.BlockSpec((hp, d_inner), const),
        pl.BlockSpec((1, d_inner), const),
        pl.BlockSpec((1, d_inner), const),
    ]
    sample_specs = [
        pl.BlockSpec((q, gw), lambda t: (sb(t), g(t))),
        pl.BlockSpec((q, d_state), lambda t: (sb(t), b_col0 + g(t))),
        pl.BlockSpec((q, d_state), lambda t: (sb(t), b_col0 + n_groups + g(t))),
        pl.BlockSpec((q, gw), lambda t: (sb(t), g(t))),
        pl.BlockSpec((None, q, hp), lambda t: (g(t), sb(t), 0)),
        pl.BlockSpec((None, hpg_rows, q), lambda t: (g(t), 0, sb(t))),
        pl.BlockSpec((None, 1, hp), lambda t: (g(t), 0, 0)),
        pl.BlockSpec((None, hpg_rows, 1), lambda t: (g(t), 0, 0)),
        pl.BlockSpec((hp, gw), const),
        pl.BlockSpec((1, gw), lambda t: (0, g(t))),
        pl.BlockSpec((1, gw), lambda t: (0, g(t))),
        pl.BlockSpec((spb, None, gw, d_state), lambda t: (sb(t), g(t), 0, 0)),
    ]
    assert len(prompt_specs) == N_PROMPT_SCAN_INPUTS and len(sample_specs) == N_SAMPLE_SCAN_INPUTS
    kern = functools.partial(_ssd_scan_kernel, n_chunks=nc, n_groups=n_groups, seq_len_s=seq_len_s,
                             head_dim=head_dim, d_state=d_state)
    return pl.pallas_call(
        kern,
        grid=(n_seq_p * nc,),
        in_specs=prompt_specs + sample_specs,
        out_specs=[
            pl.BlockSpec((q, d_inner), lambda t: (t, 0)),
            pl.BlockSpec((d_inner, d_state), lambda t: (t // nc, 0)),
            pl.BlockSpec((q, gw), lambda t: (sb(t), g(t))),
            pl.BlockSpec((spb, None, gw, d_state), lambda t: (sb(t), g(t), 0, 0)),
        ],
        out_shape=[
            jax.ShapeDtypeStruct((n_seq_p * seq_len_p, d_inner), BF16),
            jax.ShapeDtypeStruct((n_seq_p * d_inner, d_state), F32),
            jax.ShapeDtypeStruct((n_rows_s, d_inner), BF16),
            jax.ShapeDtypeStruct(state.shape, F32),
        ],
        scratch_shapes=[pltpu.VMEM((d_state, d_inner), F32)],
        compiler_params=_params("arbitrary"),
        name="ssd_scan",
    )(xbc_p, xbc_p, xbc_p, zs_p, dt, dtT, alog_row, alog_col, expand, d_x, ng,
      xbc_s, xbc_s, xbc_s, zs_s, dt_g, dtT_g, alog_row_g, alog_col_g, expand[:, :gw], d_x, ng, state)


def _prompt_conv_state(tail, *, n_prompt_tiles, tiles_per_seq, km1):
    t = tail.reshape(-1, V7X_SUBLANES, tail.shape[1])[:n_prompt_tiles]
    return t[tiles_per_seq - 1::tiles_per_seq, V7X_SUBLANES - km1:, :]


def kernel(x_prompt, x_sample, p_prompt, p_sample, state_sc_conv, state_ssd_conv, state_ssd, g_mix, g_ffn, g_ple, g_final, sc_w_in, sc_w_conv, sc_w_out, ssd_w_in, ssd_conv_w, ssd_conv_b, ssd_dt_bias, ssd_a_log, ssd_d, ssd_norm_g, ssd_w_out, ffn_w_gate, ffn_w_up, ffn_w_down, ple_w_proj, ple_w_gate):
    bp, lp, d = x_prompt.shape
    bs, ls, _ = x_sample.shape
    depth = g_mix.shape[0]
    mp, ms = bp * lp, bs * ls
    pdim = p_prompt.shape[-1]
    n_heads, head_dim, d_state = state_ssd.shape[2:]
    d_inner = n_heads * head_dim
    conv_dim = ssd_conv_w.shape[-1]
    n_groups = (conv_dim - d_inner) // (2 * d_state)
    hpg = n_heads // n_groups
    assert n_heads <= V7X_LANES and V7X_LANES % head_dim == 0 and d_state == V7X_LANES
    npt = mp // ROW_TILE
    tps = lp // ROW_TILE
    row = lambda v: v.reshape(1, -1)
    pp = p_prompt.reshape(depth, mp, pdim)
    ps = p_sample.reshape(depth, ms, pdim)

    h = (x_prompt.reshape(mp, d), x_sample.reshape(ms, d))
    sc_p, sc_s, cv_p, cv_s, st_p, st_s = [], [], [], [], [], []
    for i in range(depth):
        j = i // 2
        if i % 2 == 0:
            km1 = sc_w_conv.shape[1] - 1
            gated, tail, nstate = _short_conv_in(
                *h, row(g_mix[i]), sc_w_in, j, sc_w_conv, jnp.swapaxes(state_sc_conv, 1, 2),
                seq_len_p=lp, seq_len_s=ls)
            sc_p.append(_prompt_conv_state(tail, n_prompt_tiles=npt, tiles_per_seq=tps, km1=km1))
            sc_s.append(jnp.swapaxes(nstate, 0, 1))
            h = _matmul_residual(h, gated, sc_w_out, j, n_prompt_rows=mp)
        else:
            km1 = ssd_conv_w.shape[1] - 1
            zx = d_inner + conv_dim
            pad_h = V7X_LANES - n_heads
            w_in_t = jnp.swapaxes(ssd_w_in, 1, 2)
            w_dt_t = jnp.pad(w_in_t[j, zx:, :], ((0, pad_h), (0, 0))).astype(BF16)
            dt_b = jnp.pad(ssd_dt_bias[j], (0, pad_h))
            alog = jnp.pad(ssd_a_log[j], (0, pad_h))
            (zs, xbc, dt, dtT), tail, nstate = _ssd_in(
                *h, row(g_mix[i]), w_in_t, j, w_dt_t, row(dt_b),
                ssd_conv_w, ssd_conv_b.reshape(ssd_conv_b.shape[0], 1, conv_dim),
                jnp.swapaxes(state_ssd_conv, 1, 2),
                seq_len_p=lp, seq_len_s=ls, d_inner=d_inner)
            cv_p.append(_prompt_conv_state(tail, n_prompt_tiles=npt, tiles_per_seq=tps, km1=km1))
            cv_s.append(jnp.swapaxes(nstate, 0, 1))
            head_of_lane = jnp.arange(d_inner, dtype=jnp.int32) // head_dim
            expand = (jnp.arange(V7X_LANES, dtype=jnp.int32)[:, None] == head_of_lane[None, :]).astype(BF16)
            d_x = row(jnp.repeat(ssd_d[j], head_dim))
            ng = row(ssd_norm_g[j])
            dt_g = jnp.stack([jnp.roll(dt[1], -g * hpg, axis=1) for g in range(n_groups)])
            alog_g = jnp.stack([jnp.roll(alog, -g * hpg) for g in range(n_groups)])
            dtT_g = dtT[1][:n_heads].reshape(n_groups, hpg, ms)
            y_p, new_p, y_s, new_s = _ssd_scan(
                xbc[0], zs[0], dt[0], dtT[0], row(alog), alog.reshape(-1, 1), expand, d_x, ng,
                xbc[1], zs[1], dt_g, dtT_g, alog_g.reshape(n_groups, 1, -1),
                ssd_a_log[j].reshape(n_groups, hpg, 1),
                state_ssd[j].reshape(bs, n_groups, hpg * head_dim, d_state),
                n_seq_p=bp, seq_len_p=lp, seq_len_s=ls,
                d_inner=d_inner, n_groups=n_groups, head_dim=head_dim, d_state=d_state)
            st_p.append(new_p.reshape(bp, n_heads, head_dim, d_state))
            st_s.append(new_s.reshape(bs, n_heads, head_dim, d_state))
            h = _matmul_residual(h, (y_p, y_s), ssd_w_out, j, n_prompt_rows=mp)
        h = _ffn(h, row(g_ffn[i]), ffn_w_gate, ffn_w_up, ffn_w_down, i, n_prompt_rows=mp)
        h = _ple(h, pp, ps, row(g_ple[i]), ple_w_gate, ple_w_proj, i,
                 g_final=row(g_final) if i == depth - 1 else None)
    y_p, y_s = h
    return (y_p.reshape(bp, lp, d), y_s.reshape(bs, ls, d), jnp.stack(sc_p), jnp.stack(sc_s),
            jnp.stack(cv_p), jnp.stack(cv_s), jnp.stack(st_p), jnp.stack(st_s))
```

```python
import functools

import jax
import jax.numpy as jnp
from jax import lax
from jax.experimental import pallas as pl
from jax.experimental.pallas import tpu as pltpu

F32 = jnp.float32
BF16 = jnp.bfloat16
EPS = 1e-6
MASKED = -1e30
V7X_LANES = 128
V7X_SUBLANES = 8
HISTORY_ROWS = 16
V7X_VMEM_LIMIT = 56 * 1024 * 1024

ROW_TILE = 1024
HALF_ROW_TILE = 512
PLE_ROW_TILE = 256
COL_TILE = 512
NARROW_COL_TILE = 256
SSD_CHUNK = 128
WEIGHT_SLAB_BYTES = 16 * 1024 * 1024


def _params(*sem):
    return pltpu.CompilerParams(dimension_semantics=sem, vmem_limit_bytes=V7X_VMEM_LIMIT)


def _dot(a, b):
    return jnp.dot(a, b, preferred_element_type=F32)


def _dot_nt(a, b):
    return lax.dot_general(a, b, (((1,), (1,)), ((), ())), preferred_element_type=F32)


def _dot_tn(a, b):
    return lax.dot_general(a, b, (((0,), (0,)), ((), ())), preferred_element_type=F32)


def _split3(a):
    a1 = a.astype(BF16)
    r1 = a - a1.astype(F32)
    a2 = r1.astype(BF16)
    a3 = (r1 - a2.astype(F32)).astype(BF16)
    return a3, a2, a1


def _dot01_rhs(a, e):
    p3, p2, p1 = _split3(a)
    return (_dot(p3, e) + _dot(p2, e)) + _dot(p1, e)


def _spread(a, e):
    hi = a.astype(BF16)
    lo = (a - hi.astype(F32)).astype(BF16)
    return _dot(lo, e) + _dot(hi, e)


def _dot01_lhs(t, a):
    p3, p2, p1 = _split3(a)
    return (_dot(t, p3) + _dot(t, p2)) + _dot(t, p1)


def _rmsnorm(x, g):
    ms = jnp.mean(x * x, axis=-1, keepdims=True)
    return x * lax.rsqrt(ms + EPS) * g


def _softplus(x):
    return jnp.maximum(x, 0.0) + jnp.log1p(jnp.exp(-jnp.abs(x)))


def _silu(x):
    return x * jax.nn.sigmoid(x)


def _layer_spec(block, layer, imap):
    return pl.BlockSpec((None,) + tuple(block), lambda *a: (layer,) + tuple(imap(*a)))


def _conv(hist_ref, taps, seq_len=None, row0=0, n_rows=None):
    k = taps.shape[0]
    if n_rows is None:
        n_rows = hist_ref.shape[0] - HISTORY_ROWS
    out = taps[k - 1:k, :] * hist_ref[pl.ds(HISTORY_ROWS + row0, n_rows), :]
    if seq_len is not None:
        assert row0 % seq_len == 0
        t = lax.broadcasted_iota(jnp.int32, out.shape, 0) % seq_len
    for d in range(1, k):
        sh = hist_ref[pl.ds(HISTORY_ROWS + row0 - d, n_rows), :]
        if seq_len is not None:
            sh = jnp.where(t >= d, sh, 0.0)
        out = out + taps[k - 1 - d:k - d, :] * sh
    return out


def _state_correction(buf_ref, taps, stage_sc, seq_len):
    k = taps.shape[0]
    km1 = k - 1
    n_seq = stage_sc.shape[1] // seq_len
    stage_sc[...] = jnp.zeros_like(stage_sc)
    rows = [buf_ref[r] for r in range(km1)]
    for t in range(km1):
        acc = None
        for d in range(t + 1, k):
            term = taps[k - 1 - d:k - d, :] * rows[km1 + t - d]
            acc = term if acc is None else acc + term
        for c in range(stage_sc.shape[0]):
            stage_sc[c, pl.ds(t, n_seq, stride=seq_len), :] = acc[:, c * V7X_LANES:(c + 1) * V7X_LANES]


def _staged(stage_sc):
    return jnp.concatenate([stage_sc[c] for c in range(stage_sc.shape[0])], axis=1)


def _emit_sample_state(u, stage_sc, nstate_ref, seq_len):
    km1 = nstate_ref.shape[0]
    n_chunks = stage_sc.shape[0]
    n_seq = stage_sc.shape[1] // seq_len
    for c in range(n_chunks):
        stage_sc[c] = u[:, c * V7X_LANES:(c + 1) * V7X_LANES]
    for r in range(km1):
        nstate_ref[r] = jnp.concatenate(
            [stage_sc[c, pl.ds(seq_len - km1 + r, n_seq, stride=seq_len), :] for c in range(n_chunks)],
            axis=1)


def _sc_in_sample_kernel(x_ref, g_ref, wb_ref, wc_ref, wv_ref, taps_ref, buf_ref,
                         gated_ref, nstate_ref, wbb_ref, wcb_ref, wvb_ref, hn_sc, hist_sc, stage_sc,
                         *, sample_len):
    @pl.when(pl.program_id(0) == 0)
    def _():
        hn_sc[...] = _rmsnorm(x_ref[...], g_ref[...]).astype(BF16)

    wb, wc, wv = wb_ref[...].astype(BF16), wc_ref[...].astype(BF16), wv_ref[...].astype(BF16)
    wbb_ref[...], wcb_ref[...], wvb_ref[...] = wb, wc, wv
    taps = taps_ref[...]
    _state_correction(buf_ref, taps, stage_sc, sample_len)
    hn = hn_sc[...]
    bg = _dot(hn, wb)
    u = _dot(hn, wc) * _dot(hn, wv)
    hist_sc[:HISTORY_ROWS, :] = jnp.zeros((HISTORY_ROWS, u.shape[1]), F32)
    hist_sc[HISTORY_ROWS:, :] = u
    gated_ref[...] = (bg * (_conv(hist_sc, taps, sample_len) + _staged(stage_sc))).astype(BF16)
    _emit_sample_state(u, stage_sc, nstate_ref, sample_len)


def _sc_in_prompt_kernel(x_ref, g_ref, wb_ref, wc_ref, wv_ref, taps_ref,
                         gated_ref, tail_ref, hn_sc, carry_sc, hist_sc, *, tiles_per_seq):
    i = pl.program_id(0)
    j = pl.program_id(1)

    @pl.when(j == 0)
    def _():
        hn_sc[...] = _rmsnorm(x_ref[...], g_ref[...]).astype(BF16)

    hn = hn_sc[...]
    bg = _dot(hn, wb_ref[...])
    u = _dot(hn, wc_ref[...]) * _dot(hn, wv_ref[...])
    hist_sc[:HISTORY_ROWS, :] = jnp.where(i % tiles_per_seq == 0, 0.0, carry_sc[j])
    hist_sc[HISTORY_ROWS:, :] = u
    gated_ref[...] = (bg * _conv(hist_sc, taps_ref[...])).astype(BF16)
    carry_sc[j] = u[u.shape[0] - HISTORY_ROWS:, :]
    tail_ref[...] = u[u.shape[0] - V7X_SUBLANES:, :]


def _short_conv_in(xp, xs, g, w_in, layer, taps, buf, *, seq_len_p, seq_len_s):
    mp, d = xp.shape
    ms = xs.shape[0]
    k = taps.shape[1]
    tm = ROW_TILE
    assert mp % tm == 0 and ms == tm and seq_len_p % tm == 0 and tm % seq_len_s == 0
    assert k - 1 <= min(seq_len_s, V7X_SUBLANES)
    n_seq_s = ms // seq_len_s

    tn = NARROW_COL_TILE
    nj = d // tn
    assert d % tn == 0 and tn % V7X_LANES == 0
    gated_s, nstate, wb, wc, wv = pl.pallas_call(
        functools.partial(_sc_in_sample_kernel, sample_len=seq_len_s),
        grid=(nj,),
        in_specs=[
            pl.BlockSpec((tm, d), lambda j: (0, 0)),
            pl.BlockSpec((1, d), lambda j: (0, 0)),
            _layer_spec((d, tn), layer, lambda j: (0, j)),
            _layer_spec((d, tn), layer, lambda j: (0, nj + j)),
            _layer_spec((d, tn), layer, lambda j: (0, 2 * nj + j)),
            _layer_spec((k, tn), layer, lambda j: (0, j)),
            _layer_spec((k - 1, n_seq_s, tn), layer, lambda j: (0, 0, j)),
        ],
        out_specs=[pl.BlockSpec((tm, tn), lambda j: (0, j)),
                   pl.BlockSpec((k - 1, n_seq_s, tn), lambda j: (0, 0, j))]
        + [pl.BlockSpec((d, tn), lambda j: (0, j))] * 3,
        out_shape=[jax.ShapeDtypeStruct((ms, d), BF16), jax.ShapeDtypeStruct((k - 1, n_seq_s, d), F32)]
        + [jax.ShapeDtypeStruct((d, d), BF16)] * 3,
        scratch_shapes=[pltpu.VMEM((tm, d), BF16), pltpu.VMEM((HISTORY_ROWS + tm, tn), F32),
                        pltpu.VMEM((tn // V7X_LANES, tm, V7X_LANES), F32)],
        compiler_params=_params("arbitrary"),
        name="short_conv_in_sample",
    )(xs, g, w_in, w_in, w_in, taps, buf)

    tn = COL_TILE
    nj = d // tn
    assert d % tn == 0
    npt = mp // tm
    gated_p, tail = pl.pallas_call(
        functools.partial(_sc_in_prompt_kernel, tiles_per_seq=seq_len_p // tm),
        grid=(npt, nj),
        in_specs=[
            pl.BlockSpec((tm, d), lambda i, j: (i, 0)),
            pl.BlockSpec((1, d), lambda i, j: (0, 0)),
            pl.BlockSpec((d, tn), lambda i, j: (0, j)),
            pl.BlockSpec((d, tn), lambda i, j: (0, j)),
            pl.BlockSpec((d, tn), lambda i, j: (0, j)),
            _layer_spec((k, tn), layer, lambda i, j: (0, j)),
        ],
        out_specs=[pl.BlockSpec((tm, tn), lambda i, j: (i, j)),
                   pl.BlockSpec((V7X_SUBLANES, tn), lambda i, j: (i, j))],
        out_shape=[jax.ShapeDtypeStruct((mp, d), BF16),
                   jax.ShapeDtypeStruct((npt * V7X_SUBLANES, d), F32)],
        scratch_shapes=[pltpu.VMEM((tm, d), BF16), pltpu.VMEM((nj, HISTORY_ROWS, tn), F32),
                        pltpu.VMEM((HISTORY_ROWS + tm, tn), F32)],
        compiler_params=_params("arbitrary", "arbitrary"),
        name="short_conv_in_prompt",
    )(xp, g, wb, wc, wv, taps)
    return (gated_p, gated_s), tail, nstate


def _res_kernel(*refs, n_prompt_tiles, res_pair, a_pair):
    refs = list(refs)
    res_refs = [refs.pop(0) for _ in range(2 if res_pair else 1)]
    a_refs = [refs.pop(0) for _ in range(2 if a_pair else 1)]
    w_ref, out_ref, wb_sc = refs
    i = pl.program_id(1)

    @pl.when(i == 0)
    def _():
        wb_sc[...] = w_ref[...].astype(BF16)

    def body(which):
        out_ref[...] = res_refs[which * res_pair][...] + _dot(a_refs[which * a_pair][...], wb_sc[...])

    if not (res_pair or a_pair):
        body(0)
        return
    pl.when(i < n_prompt_tiles)(lambda: body(0))
    pl.when(i >= n_prompt_tiles)(lambda: body(1))


def _matmul_residual(res, a, w, layer, *, n_prompt_rows):
    res_pair, a_pair = isinstance(res, tuple), isinstance(a, tuple)
    _, k, n = w.shape
    m = sum(r.shape[0] for r in res) if res_pair else res.shape[0]
    tm = HALF_ROW_TILE
    tn = min(n, WEIGHT_SLAB_BYTES // (4 * k))
    assert n_prompt_rows % tm == 0 and m % tm == 0 and n % tn == 0 and tn % V7X_LANES == 0
    npt = n_prompt_rows // tm

    def row_specs(pair, block, col_of):
        if not pair:
            return [pl.BlockSpec(block, lambda j, i: (i, col_of(j)))]
        return [pl.BlockSpec(block, lambda j, i: (jnp.minimum(i, npt - 1), col_of(j))),
                pl.BlockSpec(block, lambda j, i: (jnp.maximum(i - npt, 0), col_of(j)),
                             pipeline_mode=pl.Buffered(1))]

    return pl.pallas_call(
        functools.partial(_res_kernel, n_prompt_tiles=npt, res_pair=res_pair, a_pair=a_pair),
        grid=(n // tn, m // tm),
        in_specs=row_specs(res_pair, (tm, tn), lambda j: j) + row_specs(a_pair, (tm, k), lambda j: 0)
        + [pl.BlockSpec((None, k, tn), lambda j, i: (layer, 0, j), pipeline_mode=pl.Buffered(1))],
        out_specs=pl.BlockSpec((tm, tn), lambda j, i: (i, j)),
        out_shape=jax.ShapeDtypeStruct((m, n), F32),
        scratch_shapes=[pltpu.VMEM((k, tn), BF16)],
        compiler_params=_params("arbitrary", "arbitrary"),
        name="matmul_residual",
    )(*(res if res_pair else (res,)), *(a if a_pair else (a,)), w)


def _ffn_kernel(h_ref, g_ref, wg_ref, wu_ref, wd_ref, out_ref, *rest):
    *cast_refs, hn_sc = rest

    @pl.when(pl.program_id(1) == 0)
    def _():
        h = h_ref[...]
        hn_sc[...] = _rmsnorm(h, g_ref[...]).astype(BF16)
        out_ref[...] = h

    wg, wu, wd = wg_ref[...], wu_ref[...], wd_ref[...]
    if cast_refs:
        wg, wu, wd = wg.astype(BF16), wu.astype(BF16), wd.astype(BF16)
        for ref, w in zip(cast_refs, (wg, wu, wd)):
            ref[...] = w
    hn = hn_sc[...]
    gate = _dot(hn, wg)
    act = (_silu(gate) * _dot(hn, wu)).astype(BF16)
    out_ref[...] += _dot(act, wd)


def _ffn(h, g, w_gate, w_up, w_down, layer, *, n_prompt_rows):
    m, d = h.shape
    f = w_gate.shape[2]
    tm = ROW_TILE
    assert n_prompt_rows % tm == 0 and m - n_prompt_rows == tm
    assert f % COL_TILE == 0 and f % NARROW_COL_TILE == 0
    npt = n_prompt_rows // tm

    def call(row0_tile, n_row_tiles, tf, weights, weight_specs, cast_outputs, name):
        out_specs = [pl.BlockSpec((tm, d), lambda i, j: (i, 0))]
        out_shape = [jax.ShapeDtypeStruct((n_row_tiles * tm, d), F32)]
        if cast_outputs:
            out_specs += [pl.BlockSpec((d, tf), lambda i, j: (0, j)), pl.BlockSpec((d, tf), lambda i, j: (0, j)),
                          pl.BlockSpec((tf, d), lambda i, j: (j, 0))]
            out_shape += [jax.ShapeDtypeStruct((d, f), BF16), jax.ShapeDtypeStruct((d, f), BF16),
                          jax.ShapeDtypeStruct((f, d), BF16)]
        h_mode = dict(pipeline_mode=pl.Buffered(1)) if n_row_tiles == 1 else {}
        return pl.pallas_call(
            _ffn_kernel,
            grid=(n_row_tiles, f // tf),
            in_specs=[pl.BlockSpec((tm, d), lambda i, j: (i + row0_tile, 0), **h_mode),
                      pl.BlockSpec((1, d), lambda i, j: (0, 0))] + weight_specs(tf),
            out_specs=out_specs,
            out_shape=out_shape,
            scratch_shapes=[pltpu.VMEM((tm, d), BF16)],
            compiler_params=_params("arbitrary", "arbitrary"),
            name=name,
        )(h, g, *weights)

    f32_specs = lambda tf: [_layer_spec((d, tf), layer, lambda i, j: (0, j)),
                            _layer_spec((d, tf), layer, lambda i, j: (0, j)),
                            _layer_spec((tf, d), layer, lambda i, j: (j, 0))]
    bf16_specs = lambda tf: [pl.BlockSpec((d, tf), lambda i, j: (0, j)), pl.BlockSpec((d, tf), lambda i, j: (0, j)),
                             pl.BlockSpec((tf, d), lambda i, j: (j, 0))]
    sample, wg_b, wu_b, wd_b = call(npt, 1, NARROW_COL_TILE, (w_gate, w_up, w_down), f32_specs, True,
                                    "swiglu_ffn_sample")
    prompt, = call(0, npt, COL_TILE, (wg_b, wu_b, wd_b), bf16_specs, False, "swiglu_ffn_prompt")
    return prompt, sample


def _ple_kernel(h_ref, p_ref, g_ref, wg_ref, wp_ref, *rest, cast, final):
    if cast:
        *rest, wgb_ref, wpb_ref = rest

        @pl.when(pl.program_id(0) == 0)
        def _():
            wgb_ref[...] = wg_ref[...].astype(BF16)
            wpb_ref[...] = wp_ref[...].astype(BF16)
    else:
        wgb_ref, wpb_ref = wg_ref, wp_ref
    h = h_ref[...]
    gate = jax.nn.sigmoid(_dot(_rmsnorm(h, g_ref[...]).astype(BF16), wgb_ref[...]))
    out = h + _dot(p_ref[...].astype(BF16), wpb_ref[...]) * gate
    if final:
        gf_ref, out_ref = rest
        out_ref[...] = _rmsnorm(out, gf_ref[...])
    else:
        rest[0][...] = out


def _ple(h, pp, ps, g, w_gate, w_proj, layer, g_final=None):
    hp, hs = h
    d = hp.shape[1]
    pdim = pp.shape[2]
    final = g_final is not None
    const = lambda i: (0, 0)

    def call(rows, p, tm, weights, weight_specs, cast, name):
        m = rows.shape[0]
        assert m % tm == 0
        in_specs = [pl.BlockSpec((tm, d), lambda i: (i, 0)),
                    _layer_spec((tm, pdim), layer, lambda i: (i, 0)),
                    pl.BlockSpec((1, d), const)] + weight_specs
        args = [rows, p, g, *weights]
        if final:
            in_specs.append(pl.BlockSpec((1, d), const))
            args.append(g_final)
        out_specs = [pl.BlockSpec((tm, d), lambda i: (i, 0))]
        out_shape = [jax.ShapeDtypeStruct((m, d), F32)]
        if cast:
            out_specs += [pl.BlockSpec((d, d), const), pl.BlockSpec((pdim, d), const)]
            out_shape += [jax.ShapeDtypeStruct((d, d), BF16), jax.ShapeDtypeStruct((pdim, d), BF16)]
        return pl.pallas_call(
            functools.partial(_ple_kernel, cast=cast, final=final),
            grid=(m // tm,),
            in_specs=in_specs,
            out_specs=out_specs,
            out_shape=out_shape,
            compiler_params=_params("arbitrary"),
            name=name,
        )(*args)

    resident = dict(pipeline_mode=pl.Buffered(1))
    f32_specs = [pl.BlockSpec((None, d, d), lambda i: (layer, 0, 0), **resident),
                 pl.BlockSpec((None, pdim, d), lambda i: (layer, 0, 0), **resident)]
    bf16_specs = [pl.BlockSpec((d, d), const, **resident), pl.BlockSpec((pdim, d), const, **resident)]
    tag = "ple_final" if final else "ple"
    out_s, wg_b, wp_b = call(hs, ps, PLE_ROW_TILE, (w_gate, w_proj), f32_specs, True, tag + "_sample")
    out_p, = call(hp, pp, HALF_ROW_TILE, (wg_b, wp_b), bf16_specs, False, tag + "_prompt")
    return out_p, out_s


def _ssd_in_kernel(h_ref, g_ref, wt_ref, wdtT_ref, dtb_ref, taps_ref, cb_ref, *rest,
                   sample_len, tiles_per_seq, n_z_tiles):
    if sample_len is not None:
        (buf_ref, zs_ref, xbc_ref, nstate_ref, dt_ref, dtT_ref, wtb_ref,
         hn_sc, hist_sc, stage_sc) = rest
        i, j = 0, pl.program_id(0)
    else:
        zs_ref, xbc_ref, tail_ref, dt_ref, dtT_ref, hn_sc, carry_sc, hist_sc = rest
        i, j = pl.program_id(0), pl.program_id(1)
    jc = j - n_z_tiles
    tm = hn_sc.shape[0]

    @pl.when(j == 0)
    def _():
        hn = _rmsnorm(h_ref[...], g_ref[...]).astype(BF16)
        hn_sc[...] = hn
        dt = _softplus(_dot_nt(hn, wdtT_ref[...]) + dtb_ref[...])
        dt_ref[...] = dt
        dtT_ref[...] = dt.T

    def weight_tile():
        if sample_len is None:
            return wt_ref[...]
        wb = wt_ref[...].astype(BF16)
        wtb_ref[...] = wb
        return wb

    def finish(conv):
        return _silu(conv + cb_ref[...]).astype(BF16)

    @pl.when(j < n_z_tiles)
    def _():
        zs_ref[...] = _silu(_dot_nt(hn_sc[...], weight_tile())).astype(BF16)

    @pl.when(j >= n_z_tiles)
    def _():
        hist_sc[HISTORY_ROWS:, :] = _dot_nt(hn_sc[...], weight_tile())
        taps = taps_ref[...]
        if sample_len is None:
            hist_sc[:HISTORY_ROWS, :] = jnp.where(i % tiles_per_seq == 0, 0.0, carry_sc[jc])
            xbc_ref[...] = finish(_conv(hist_sc, taps))
            carry_sc[jc] = hist_sc[tm:, :]
            tail_ref[...] = hist_sc[HISTORY_ROWS + tm - V7X_SUBLANES:, :]
        else:
            _state_correction(buf_ref, taps, stage_sc, sample_len)
            hist_sc[:HISTORY_ROWS, :] = jnp.zeros((HISTORY_ROWS, hist_sc.shape[1]), F32)
            xbc_ref[...] = finish(_conv(hist_sc, taps, sample_len) + _staged(stage_sc))
            _emit_sample_state(hist_sc[HISTORY_ROWS:, :], stage_sc, nstate_ref, sample_len)


def _ssd_in(h_p, h_s, g, w_in_t, layer, w_dt_t, dt_b, taps, conv_b, buf, *, seq_len_p, seq_len_s, d_inner):
    n_prompt_rows, d = h_p.shape
    ms = h_s.shape[0]
    _, k, conv_dim = taps.shape
    hp = w_dt_t.shape[0]
    tm, tn = ROW_TILE, COL_TILE
    assert n_prompt_rows % tm == 0 and ms == tm and seq_len_p % tm == 0 and tm % seq_len_s == 0
    assert d_inner % tn == 0 and conv_dim % tn == 0 and k - 1 <= min(seq_len_s, V7X_SUBLANES)
    npt = n_prompt_rows // tm
    n_seq_s = ms // seq_len_s
    nz, nc = d_inner // tn, conv_dim // tn
    zc = lambda j: jnp.minimum(j, nz - 1)
    cc = lambda j: jnp.maximum(j - nz, 0)

    zs_s, xbc_s, nstate, dt_s, dtT_s, w_b = pl.pallas_call(
        functools.partial(_ssd_in_kernel, sample_len=seq_len_s, tiles_per_seq=None, n_z_tiles=nz),
        grid=(nz + nc,),
        in_specs=[
            pl.BlockSpec((tm, d), lambda j: (0, 0)),
            pl.BlockSpec((1, d), lambda j: (0, 0)),
            _layer_spec((tn, d), layer, lambda j: (j, 0)),
            pl.BlockSpec((hp, d), lambda j: (0, 0)),
            pl.BlockSpec((1, hp), lambda j: (0, 0)),
            _layer_spec((k, tn), layer, lambda j: (0, cc(j))),
            _layer_spec((1, tn), layer, lambda j: (0, cc(j))),
            _layer_spec((k - 1, n_seq_s, tn), layer, lambda j: (0, 0, cc(j))),
        ],
        out_specs=[
            pl.BlockSpec((tm, tn), lambda j: (0, zc(j))),
            pl.BlockSpec((tm, tn), lambda j: (0, cc(j))),
            pl.BlockSpec((k - 1, n_seq_s, tn), lambda j: (0, 0, cc(j))),
            pl.BlockSpec((tm, hp), lambda j: (0, 0)),
            pl.BlockSpec((hp, tm), lambda j: (0, 0)),
            pl.BlockSpec((tn, d), lambda j: (j, 0)),
        ],
        out_shape=[
            jax.ShapeDtypeStruct((ms, d_inner), BF16),
            jax.ShapeDtypeStruct((ms, conv_dim), BF16),
            jax.ShapeDtypeStruct((k - 1, n_seq_s, conv_dim), F32),
            jax.ShapeDtypeStruct((ms, hp), F32),
            jax.ShapeDtypeStruct((hp, ms), F32),
            jax.ShapeDtypeStruct(((nz + nc) * tn, d), BF16),
        ],
        scratch_shapes=[pltpu.VMEM((tm, d), BF16), pltpu.VMEM((HISTORY_ROWS + tm, tn), F32),
                        pltpu.VMEM((tn // V7X_LANES, tm, V7X_LANES), F32)],
        compiler_params=_params("arbitrary"),
        name="ssd_in_sample",
    )(h_s, g, w_in_t, w_dt_t, dt_b, taps, conv_b, buf)

    const = lambda i, j: (0, 0)
    zs_p, xbc_p, tail, dt_p, dtT_p = pl.pallas_call(
        functools.partial(_ssd_in_kernel, sample_len=None, tiles_per_seq=seq_len_p // tm, n_z_tiles=nz),
        grid=(npt, nz + nc),
        in_specs=[
            pl.BlockSpec((tm, d), lambda i, j: (i, 0)),
            pl.BlockSpec((1, d), const),
            pl.BlockSpec((tn, d), lambda i, j: (j, 0)),
            pl.BlockSpec((hp, d), const),
            pl.BlockSpec((1, hp), const),
            _layer_spec((k, tn), layer, lambda i, j: (0, cc(j))),
            _layer_spec((1, tn), layer, lambda i, j: (0, cc(j))),
        ],
        out_specs=[
            pl.BlockSpec((tm, tn), lambda i, j: (i, zc(j))),
            pl.BlockSpec((tm, tn), lambda i, j: (i, cc(j))),
            pl.BlockSpec((V7X_SUBLANES, tn), lambda i, j: (i, cc(j))),
            pl.BlockSpec((tm, hp), lambda i, j: (i, 0)),
            pl.BlockSpec((hp, tm), lambda i, j: (0, i)),
        ],
        out_shape=[
            jax.ShapeDtypeStruct((n_prompt_rows, d_inner), BF16),
            jax.ShapeDtypeStruct((n_prompt_rows, conv_dim), BF16),
            jax.ShapeDtypeStruct((npt * V7X_SUBLANES, conv_dim), F32),
            jax.ShapeDtypeStruct((n_prompt_rows, hp), F32),
            jax.ShapeDtypeStruct((hp, n_prompt_rows), F32),
        ],
        scratch_shapes=[pltpu.VMEM((tm, d), BF16), pltpu.VMEM((nc, HISTORY_ROWS, tn), F32),
                        pltpu.VMEM((HISTORY_ROWS + tm, tn), F32)],
        compiler_params=_params("arbitrary", "arbitrary"),
        name="ssd_in_prompt",
    )(h_p, g, w_b, w_dt_t, dt_b, taps, conv_b)
    return ((zs_p, zs_s), (xbc_p, xbc_s), (dt_p, dt_s), (dtT_p, dtT_s)), tail, nstate


def _ssd_group_out(x, zs, cb, acum, acumT_less_logdt, mask, extra, ng, head0, heads_per_group, head_dim,
                   carried=None):
    heads_per_slab = V7X_LANES // head_dim
    cb = cb.astype(BF16)
    parts = []
    for q in range(heads_per_group // heads_per_slab):
        cols = slice(q * V7X_LANES, (q + 1) * V7X_LANES)
        rhs = x[:, cols]
        if carried is not None:
            cg, st_t = carried
            rhs = jnp.concatenate([rhs, st_t[:, cols].astype(BF16)], axis=0)
        lanes = lax.broadcasted_iota(jnp.int32, rhs.shape, 1)
        acc = None
        for r in range(heads_per_slab):
            hd = head0 + q * heads_per_slab + r
            a_t = jnp.broadcast_to(acum[:, hd:hd + 1], cb.shape)
            seg = a_t - acumT_less_logdt[hd:hd + 1, :]
            lhs = cb * jnp.exp(jnp.where(mask, seg, MASKED)).astype(BF16)
            if carried is not None:
                lhs = jnp.concatenate([lhs, cg * jnp.exp(a_t).astype(BF16)], axis=1)
            in_head = jnp.logical_and(lanes >= r * head_dim, lanes < (r + 1) * head_dim)
            part = _dot(lhs, jnp.where(in_head, rhs, jnp.zeros_like(rhs)))
            acc = part if acc is None else acc + part
        parts.append(acc)
    y = jnp.concatenate(parts, axis=1) + extra
    gated = y * zs
    ms = jnp.mean(gated * gated, axis=-1, keepdims=True)
    return (gated * lax.rsqrt(ms + EPS) * ng).astype(BF16)


def _ssd_prompt_body(c, n_chunks, xs_ref, b_ref, c_ref, zs_ref, dt_ref, dtT_ref, alr_ref, alc_ref,
                     e_ref, d_ref, ng_ref, y_ref, state_ref, st_sc, *, n_groups, head_dim, d_state):
    @pl.when(c == 0)
    def _():
        st_sc[...] = jnp.zeros_like(st_sc)

    q_rows, d_inner = xs_ref.shape
    gw = d_inner // n_groups
    hpg = gw // head_dim
    row = lax.broadcasted_iota(jnp.int32, (q_rows, q_rows), 0)
    col = lax.broadcasted_iota(jnp.int32, (q_rows, q_rows), 1)
    causal = col <= row
    tril = jnp.where(causal, 1.0, 0.0).astype(BF16)
    triu = jnp.where(row <= col, 1.0, 0.0).astype(BF16)
    dt = dt_ref[...]
    dtT = dtT_ref[...]
    acum = _dot01_lhs(tril, dt * -jnp.exp(alr_ref[...]))
    acumT = _dot01_rhs(dtT * -jnp.exp(alc_ref[...]), triu)
    acumT_less_logdt = acumT - jnp.log(dtT)
    a_end = acum[q_rows - 1:q_rows, :]
    w_end = (dt * jnp.exp(a_end - acum)).astype(BF16)
    decay_h = jnp.broadcast_to(jnp.exp(a_end), (V7X_SUBLANES, a_end.shape[1]))
    for g in range(n_groups):
        sl = slice(g * gw, (g + 1) * gw)
        ns = slice(g * d_state, (g + 1) * d_state)
        e = e_ref[:, sl]
        bg, cg = b_ref[:, ns], c_ref[:, ns]
        x = xs_ref[:, sl]
        xf = x.astype(F32)
        st = st_sc[:, sl]
        y_ref[:, sl] = _ssd_group_out(x, zs_ref[:, sl].astype(F32), _dot_nt(cg, bg), acum,
                                      acumT_less_logdt, causal, xf * d_ref[:, sl], ng_ref[:, sl],
                                      g * hpg, hpg, head_dim, carried=(cg, st))
        to_end = (xf * _dot(w_end, e)).astype(BF16)
        st_sc[:, sl] = _dot01_rhs(decay_h, e)[:1, :] * st + _dot_tn(bg, to_end)

    @pl.when(c == n_chunks - 1)
    def _():
        for g in range(n_groups):
            state_ref[g * gw:(g + 1) * gw, :] = st_sc[:, g * gw:(g + 1) * gw].T


def _ssd_sample_body(xs_ref, b_ref, c_ref, zs_ref, dt_ref, dtT_ref, alr_ref, alc_ref, e_ref,
                     d_ref, ng_ref, st_ref, y_ref, nst_ref, *, seq_len, head_dim):
    q_rows, gw = xs_ref.shape
    n_seq = q_rows // seq_len
    row = lax.broadcasted_iota(jnp.int32, (q_rows, q_rows), 0)
    col = lax.broadcasted_iota(jnp.int32, (q_rows, q_rows), 1)
    same = (row // seq_len) == (col // seq_len)
    mask = jnp.logical_and(same, col <= row)
    tril = jnp.where(mask, 1.0, 0.0).astype(BF16)
    triu = jnp.where(jnp.logical_and(same, row <= col), 1.0, 0.0).astype(BF16)
    ends = jnp.where(col == (row // seq_len) * seq_len + (seq_len - 1), 1.0, 0.0).astype(BF16)
    dt = dt_ref[...]
    dtT = dtT_ref[...]
    acum = _dot01_lhs(tril, dt * -jnp.exp(alr_ref[...]))
    acumT = _dot01_rhs(dtT * -jnp.exp(alc_ref[...]), triu)
    a_end = _dot01_lhs(ends, acum)
    e = e_ref[...]
    x = xs_ref[...].astype(F32)
    to_endT = (x * _dot((dt * jnp.exp(a_end - acum)).astype(BF16), e)).T.astype(BF16)
    decayT = _dot01_rhs(jnp.exp(a_end), e).T
    from_start = _spread(jnp.exp(acum), e)
    bg = b_ref[...].astype(F32)
    cg = c_ref[...].astype(F32)
    seq_of_row = lax.broadcasted_iota(jnp.int32, bg.shape, 0) // seq_len
    inter = jnp.zeros((q_rows, gw), F32)
    for s in range(n_seq):
        st = st_ref[s]
        mine = seq_of_row == s
        inter = inter + _dot_nt(jnp.where(mine, cg, 0.0).astype(BF16), st.astype(BF16))
        bm = jnp.where(mine, bg, 0.0).astype(BF16)
        nst_ref[s] = decayT[:, s * seq_len:s * seq_len + 1] * st + _dot(to_endT, bm)
    inter = from_start * inter + x * d_ref[...]
    y_ref[...] = _ssd_group_out(xs_ref[...], zs_ref[...].astype(F32), _dot_nt(c_ref[...], b_ref[...]),
                                acum, acumT - jnp.log(dtT), mask, inter, ng_ref[...],
                                0, gw // head_dim, head_dim)


N_PROMPT_SCAN_INPUTS = 11
N_SAMPLE_SCAN_INPUTS = 12


def _ssd_scan_kernel(*refs, n_chunks, n_groups, seq_len_s, head_dim, d_state):
    a, b = N_PROMPT_SCAN_INPUTS, N_PROMPT_SCAN_INPUTS + N_SAMPLE_SCAN_INPUTS
    prompt_in, sample_in = refs[:a], refs[a:b]
    y_p, state_p, y_s, state_s, st_sc = refs[b:]
    _ssd_prompt_body(pl.program_id(0) % n_chunks, n_chunks, *prompt_in, y_p, state_p, st_sc,
                     n_groups=n_groups, head_dim=head_dim, d_state=d_state)
    _ssd_sample_body(*sample_in, y_s, state_s, seq_len=seq_len_s, head_dim=head_dim)


def _ssd_scan(xbc_p, zs_p, dt, dtT, alog_row, alog_col, expand, d_x, ng,
              xbc_s, zs_s, dt_g, dtT_g, alog_row_g, alog_col_g, state,
              *, n_seq_p, seq_len_p, seq_len_s, d_inner, n_groups, head_dim, d_state):
    q = SSD_CHUNK
    n_rows_s = xbc_s.shape[0]
    assert seq_len_p % q == 0 and q == d_state
    assert n_rows_s % q == 0 and q % seq_len_s == 0
    nc = seq_len_p // q
    nb = n_rows_s // q
    assert n_seq_p * nc == nb * n_groups
    spb = q // seq_len_s
    gw = d_inner // n_groups
    hp = dt.shape[1]
    hpg_rows = dtT_g.shape[1]
    gn = n_groups * d_state
    assert d_inner % gn == 0
    b_col0 = d_inner // d_state
    const = lambda t: (0, 0)
    sb = lambda t: t // n_groups
    g = lambda t: t % n_groups
    prompt_specs = [
        pl.BlockSpec((q, d_inner), lambda t: (t, 0)),
        pl.BlockSpec((q, gn), lambda t: (t, d_inner // gn)),
        pl.BlockSpec((q, gn), lambda t: (t, d_inner // gn + 1)),
        pl.BlockSpec((q, d_inner), lambda t: (t, 0)),
        pl.BlockSpec((q, hp), lambda t: (t, 0)),
        pl.BlockSpec((hp, q), lambda t: (0, t)),
        pl.BlockSpec((1, hp), const),
        pl.BlockSpec((hp, 1), const),
        pl.BlockSpec((hp, d_inner), const),
        pl.BlockSpec((1, d_inner), const),
        pl.BlockSpec((1, d_inner), const),
    ]
    sample_specs = [
        pl.BlockSpec((q, gw), lambda t: (sb(t), g(t))),
        pl.BlockSpec((q, d_state), lambda t: (sb(t), b_col0 + g(t))),
        pl.BlockSpec((q, d_state), lambda t: (sb(t), b_col0 + n_groups + g(t))),
        pl.BlockSpec((q, gw), lambda t: (sb(t), g(t))),
        pl.BlockSpec((None, q, hp), lambda t: (g(t), sb(t), 0)),
        pl.BlockSpec((None, hpg_rows, q), lambda t: (g(t), 0, sb(t))),
        pl.BlockSpec((None, 1, hp), lambda t: (g(t), 0, 0)),
        pl.BlockSpec((None, hpg_rows, 1), lambda t: (g(t), 0, 0)),
        pl.BlockSpec((hp, gw), const),
        pl.BlockSpec((1, gw), lambda t: (0, g(t))),
        pl.BlockSpec((1, gw), lambda t: (0, g(t))),
        pl.BlockSpec((spb, None, gw, d_state), lambda t: (sb(t), g(t), 0, 0)),
    ]
    assert len(prompt_specs) == N_PROMPT_SCAN_INPUTS and len(sample_specs) == N_SAMPLE_SCAN_INPUTS
    kern = functools.partial(_ssd_scan_kernel, n_chunks=nc, n_groups=n_groups, seq_len_s=seq_len_s,
                             head_dim=head_dim, d_state=d_state)
    return pl.pallas_call(
        kern,
        grid=(n_seq_p * nc,),
        in_specs=prompt_specs + sample_specs,
        out_specs=[
            pl.BlockSpec((q, d_inner), lambda t: (t, 0)),
            pl.BlockSpec((d_inner, d_state), lambda t: (t // nc, 0)),
            pl.BlockSpec((q, gw), lambda t: (sb(t), g(t))),
            pl.BlockSpec((spb, None, gw, d_state), lambda t: (sb(t), g(t), 0, 0)),
        ],
        out_shape=[
            jax.ShapeDtypeStruct((n_seq_p * seq_len_p, d_inner), BF16),
            jax.ShapeDtypeStruct((n_seq_p * d_inner, d_state), F32),
            jax.ShapeDtypeStruct((n_rows_s, d_inner), BF16),
            jax.ShapeDtypeStruct(state.shape, F32),
        ],
        scratch_shapes=[pltpu.VMEM((d_state, d_inner), F32)],
        compiler_params=_params("arbitrary"),
        name="ssd_scan",
    )(xbc_p, xbc_p, xbc_p, zs_p, dt, dtT, alog_row, alog_col, expand, d_x, ng,
      xbc_s, xbc_s, xbc_s, zs_s, dt_g, dtT_g, alog_row_g, alog_col_g, expand[:, :gw], d_x, ng, state)


def _prompt_conv_state(tail, *, n_prompt_tiles, tiles_per_seq, km1):
    t = tail.reshape(-1, V7X_SUBLANES, tail.shape[1])[:n_prompt_tiles]
    return t[tiles_per_seq - 1::tiles_per_seq, V7X_SUBLANES - km1:, :]


def kernel(x_prompt, x_sample, p_prompt, p_sample, state_sc_conv, state_ssd_conv, state_ssd, g_mix, g_ffn, g_ple, g_final, sc_w_in, sc_w_conv, sc_w_out, ssd_w_in, ssd_conv_w, ssd_conv_b, ssd_dt_bias, ssd_a_log, ssd_d, ssd_norm_g, ssd_w_out, ffn_w_gate, ffn_w_up, ffn_w_down, ple_w_proj, ple_w_gate):
    bp, lp, d = x_prompt.shape
    bs, ls, _ = x_sample.shape
    depth = g_mix.shape[0]
    mp, ms = bp * lp, bs * ls
    pdim = p_prompt.shape[-1]
    n_heads, head_dim, d_state = state_ssd.shape[2:]
    d_inner = n_heads * head_dim
    conv_dim = ssd_conv_w.shape[-1]
    n_groups = (conv_dim - d_inner) // (2 * d_state)
    hpg = n_heads // n_groups
    assert n_heads <= V7X_LANES and V7X_LANES % head_dim == 0 and d_state == V7X_LANES
    npt = mp // ROW_TILE
    tps = lp // ROW_TILE
    row = lambda v: v.reshape(1, -1)
    pp = p_prompt.reshape(depth, mp, pdim)
    ps = p_sample.reshape(depth, ms, pdim)

    h = (x_prompt.reshape(mp, d), x_sample.reshape(ms, d))
    sc_p, sc_s, cv_p, cv_s, st_p, st_s = [], [], [], [], [], []
    for i in range(depth):
        j = i // 2
        if i % 2 == 0:
            km1 = sc_w_conv.shape[1] - 1
            gated, tail, nstate = _short_conv_in(
                *h, row(g_mix[i]), sc_w_in, j, sc_w_conv, jnp.swapaxes(state_sc_conv, 1, 2),
                seq_len_p=lp, seq_len_s=ls)
            sc_p.append(_prompt_conv_state(tail, n_prompt_tiles=npt, tiles_per_seq=tps, km1=km1))
            sc_s.append(jnp.swapaxes(nstate, 0, 1))
            h = _matmul_residual(h, gated, sc_w_out, j, n_prompt_rows=mp)
        else:
            km1 = ssd_conv_w.shape[1] - 1
            zx = d_inner + conv_dim
            pad_h = V7X_LANES - n_heads
            w_in_t = jnp.swapaxes(ssd_w_in, 1, 2)
            w_dt_t = jnp.pad(w_in_t[j, zx:, :], ((0, pad_h), (0, 0))).astype(BF16)
            dt_b = jnp.pad(ssd_dt_bias[j], (0, pad_h))
            alog = jnp.pad(ssd_a_log[j], (0, pad_h))
            (zs, xbc, dt, dtT), tail, nstate = _ssd_in(
                *h, row(g_mix[i]), w_in_t, j, w_dt_t, row(dt_b),
                ssd_conv_w, ssd_conv_b.reshape(ssd_conv_b.shape[0], 1, conv_dim),
                jnp.swapaxes(state_ssd_conv, 1, 2),
                seq_len_p=lp, seq_len_s=ls, d_inner=d_inner)
            cv_p.append(_prompt_conv_state(tail, n_prompt_tiles=npt, tiles_per_seq=tps, km1=km1))
            cv_s.append(jnp.swapaxes(nstate, 0, 1))
            head_of_lane = jnp.arange(d_inner, dtype=jnp.int32) // head_dim
            expand = (jnp.arange(V7X_LANES, dtype=jnp.int32)[:, None] == head_of_lane[None, :]).astype(BF16)
            d_x = row(jnp.repeat(ssd_d[j], head_dim))
            ng = row(ssd_norm_g[j])
            dt_g = jnp.stack([jnp.roll(dt[1], -g * hpg, axis=1) for g in range(n_groups)])
            alog_g = jnp.stack([jnp.roll(alog, -g * hpg) for g in range(n_groups)])
            dtT_g = dtT[1][:n_heads].reshape(n_groups, hpg, ms)
            y_p, new_p, y_s, new_s = _ssd_scan(
                xbc[0], zs[0], dt[0], dtT[0], row(alog), alog.reshape(-1, 1), expand, d_x, ng,
                xbc[1], zs[1], dt_g, dtT_g, alog_g.reshape(n_groups, 1, -1),
                ssd_a_log[j].reshape(n_groups, hpg, 1),
                state_ssd[j].reshape(bs, n_groups, hpg * head_dim, d_state),
                n_seq_p=bp, seq_len_p=lp, seq_len_s=ls,
                d_inner=d_inner, n_groups=n_groups, head_dim=head_dim, d_state=d_state)
            st_p.append(new_p.reshape(bp, n_heads, head_dim, d_state))
            st_s.append(new_s.reshape(bs, n_heads, head_dim, d_state))
            h = _matmul_residual(h, (y_p, y_s), ssd_w_out, j, n_prompt_rows=mp)
        h = _ffn(h, row(g_ffn[i]), ffn_w_gate, ffn_w_up, ffn_w_down, i, n_prompt_rows=mp)
        h = _ple(h, pp, ps, row(g_ple[i]), ple_w_gate, ple_w_proj, i,
                 g_final=row(g_final) if i == depth - 1 else None)
    y_p, y_s = h
    return (y_p.reshape(bp, lp, d), y_s.reshape(bs, ls, d), jnp.stack(sc_p), jnp.stack(sc_s),
            jnp.stack(cv_p), jnp.stack(cv_s), jnp.stack(st_p), jnp.stack(st_s))
```

```python
import functools

import jax
import jax.numpy as jnp
from jax import lax
from jax.experimental import pallas as pl
from jax.experimental.pallas import tpu as pltpu

F32 = jnp.float32
BF16 = jnp.bfloat16
EPS = 1e-6
MASKED = -1e30
V7X_LANES = 128
V7X_SUBLANES = 8
HISTORY_ROWS = 16
V7X_VMEM_LIMIT = 56 * 1024 * 1024

ROW_TILE = 1024
HALF_ROW_TILE = 512
PLE_ROW_TILE = 256
COL_TILE = 512
WIDE_COL_TILE = 1024
NARROW_COL_TILE = 256
SSD_CHUNK = 128
WEIGHT_SLAB_BYTES = 16 * 1024 * 1024


def _params(*sem):
    return pltpu.CompilerParams(dimension_semantics=sem, vmem_limit_bytes=V7X_VMEM_LIMIT)


def _dot(a, b):
    return jnp.dot(a, b, preferred_element_type=F32)


def _dot_nt(a, b):
    return lax.dot_general(a, b, (((1,), (1,)), ((), ())), preferred_element_type=F32)


def _dot_tn(a, b):
    return lax.dot_general(a, b, (((0,), (0,)), ((), ())), preferred_element_type=F32)


def _split3(a):
    a1 = a.astype(BF16)
    r1 = a - a1.astype(F32)
    a2 = r1.astype(BF16)
    a3 = (r1 - a2.astype(F32)).astype(BF16)
    return a3, a2, a1


def _dot01_rhs(a, e):
    p3, p2, p1 = _split3(a)
    return (_dot(p3, e) + _dot(p2, e)) + _dot(p1, e)


def _spread(a, e):
    hi = a.astype(BF16)
    lo = (a - hi.astype(F32)).astype(BF16)
    return _dot(lo, e) + _dot(hi, e)


def _dot01_lhs(t, a):
    p3, p2, p1 = _split3(a)
    return (_dot(t, p3) + _dot(t, p2)) + _dot(t, p1)


def _rmsnorm(x, g):
    ms = jnp.mean(x * x, axis=-1, keepdims=True)
    return x * lax.rsqrt(ms + EPS) * g


def _softplus(x):
    return jnp.maximum(x, 0.0) + jnp.log1p(jnp.exp(-jnp.abs(x)))


def _silu(x):
    return x * jax.nn.sigmoid(x)


def _layer_spec(block, layer, imap):
    return pl.BlockSpec((None,) + tuple(block), lambda *a: (layer,) + tuple(imap(*a)))


def _conv(hist_ref, taps, seq_len=None, row0=0, n_rows=None):
    k = taps.shape[0]
    if n_rows is None:
        n_rows = hist_ref.shape[0] - HISTORY_ROWS
    out = taps[k - 1:k, :] * hist_ref[pl.ds(HISTORY_ROWS + row0, n_rows), :]
    if seq_len is not None:
        assert row0 % seq_len == 0
        t = lax.broadcasted_iota(jnp.int32, out.shape, 0) % seq_len
    for d in range(1, k):
        sh = hist_ref[pl.ds(HISTORY_ROWS + row0 - d, n_rows), :]
        if seq_len is not None:
            sh = jnp.where(t >= d, sh, 0.0)
        out = out + taps[k - 1 - d:k - d, :] * sh
    return out


def _state_correction(buf_ref, taps, stage_sc, seq_len):
    k = taps.shape[0]
    km1 = k - 1
    n_seq = stage_sc.shape[1] // seq_len
    stage_sc[...] = jnp.zeros_like(stage_sc)
    rows = [buf_ref[r] for r in range(km1)]
    for t in range(km1):
        acc = None
        for d in range(t + 1, k):
            term = taps[k - 1 - d:k - d, :] * rows[km1 + t - d]
            acc = term if acc is None else acc + term
        for c in range(stage_sc.shape[0]):
            stage_sc[c, pl.ds(t, n_seq, stride=seq_len), :] = acc[:, c * V7X_LANES:(c + 1) * V7X_LANES]


def _staged(stage_sc):
    return jnp.concatenate([stage_sc[c] for c in range(stage_sc.shape[0])], axis=1)


def _emit_sample_state(u, stage_sc, nstate_ref, seq_len):
    km1 = nstate_ref.shape[0]
    n_chunks = stage_sc.shape[0]
    n_seq = stage_sc.shape[1] // seq_len
    for c in range(n_chunks):
        stage_sc[c] = u[:, c * V7X_LANES:(c + 1) * V7X_LANES]
    for r in range(km1):
        nstate_ref[r] = jnp.concatenate(
            [stage_sc[c, pl.ds(seq_len - km1 + r, n_seq, stride=seq_len), :] for c in range(n_chunks)],
            axis=1)


def _sc_in_sample_kernel(x_ref, g_ref, wb_ref, wc_ref, wv_ref, taps_ref, buf_ref,
                         gated_ref, nstate_ref, wbb_ref, wcb_ref, wvb_ref, hn_sc, hist_sc, stage_sc,
                         *, sample_len):
    @pl.when(pl.program_id(0) == 0)
    def _():
        hn_sc[...] = _rmsnorm(x_ref[...], g_ref[...]).astype(BF16)

    wb, wc, wv = wb_ref[...].astype(BF16), wc_ref[...].astype(BF16), wv_ref[...].astype(BF16)
    wbb_ref[...], wcb_ref[...], wvb_ref[...] = wb, wc, wv
    taps = taps_ref[...]
    _state_correction(buf_ref, taps, stage_sc, sample_len)
    hn = hn_sc[...]
    bg = _dot(hn, wb)
    u = _dot(hn, wc) * _dot(hn, wv)
    hist_sc[:HISTORY_ROWS, :] = jnp.zeros((HISTORY_ROWS, u.shape[1]), F32)
    hist_sc[HISTORY_ROWS:, :] = u
    gated_ref[...] = (bg * (_conv(hist_sc, taps, sample_len) + _staged(stage_sc))).astype(BF16)
    _emit_sample_state(u, stage_sc, nstate_ref, sample_len)


def _sc_in_prompt_kernel(x_ref, g_ref, wb_ref, wc_ref, wv_ref, taps_ref,
                         gated_ref, tail_ref, hn_sc, carry_sc, hist_sc, *, tiles_per_seq):
    i = pl.program_id(0)
    j = pl.program_id(1)

    @pl.when(j == 0)
    def _():
        hn_sc[...] = _rmsnorm(x_ref[...], g_ref[...]).astype(BF16)

    hn = hn_sc[...]
    bg = _dot(hn, wb_ref[...])
    u = _dot(hn, wc_ref[...]) * _dot(hn, wv_ref[...])
    hist_sc[:HISTORY_ROWS, :] = jnp.where(i % tiles_per_seq == 0, 0.0, carry_sc[j])
    hist_sc[HISTORY_ROWS:, :] = u
    gated_ref[...] = (bg * _conv(hist_sc, taps_ref[...])).astype(BF16)
    carry_sc[j] = u[u.shape[0] - HISTORY_ROWS:, :]
    tail_ref[...] = u[u.shape[0] - V7X_SUBLANES:, :]


def _short_conv_in(xp, xs, g, w_in, layer, taps, buf, *, seq_len_p, seq_len_s):
    mp, d = xp.shape
    ms = xs.shape[0]
    k = taps.shape[1]
    tm = ROW_TILE
    assert mp % tm == 0 and ms == tm and seq_len_p % tm == 0 and tm % seq_len_s == 0
    assert k - 1 <= min(seq_len_s, V7X_SUBLANES)
    n_seq_s = ms // seq_len_s

    tn = NARROW_COL_TILE
    nj = d // tn
    assert d % tn == 0 and tn % V7X_LANES == 0
    gated_s, nstate, wb, wc, wv = pl.pallas_call(
        functools.partial(_sc_in_sample_kernel, sample_len=seq_len_s),
        grid=(nj,),
        in_specs=[
            pl.BlockSpec((tm, d), lambda j: (0, 0)),
            pl.BlockSpec((1, d), lambda j: (0, 0)),
            _layer_spec((d, tn), layer, lambda j: (0, j)),
            _layer_spec((d, tn), layer, lambda j: (0, nj + j)),
            _layer_spec((d, tn), layer, lambda j: (0, 2 * nj + j)),
            _layer_spec((k, tn), layer, lambda j: (0, j)),
            _layer_spec((k - 1, n_seq_s, tn), layer, lambda j: (0, 0, j)),
        ],
        out_specs=[pl.BlockSpec((tm, tn), lambda j: (0, j)),
                   pl.BlockSpec((k - 1, n_seq_s, tn), lambda j: (0, 0, j))]
        + [pl.BlockSpec((d, tn), lambda j: (0, j))] * 3,
        out_shape=[jax.ShapeDtypeStruct((ms, d), BF16), jax.ShapeDtypeStruct((k - 1, n_seq_s, d), F32)]
        + [jax.ShapeDtypeStruct((d, d), BF16)] * 3,
        scratch_shapes=[pltpu.VMEM((tm, d), BF16), pltpu.VMEM((HISTORY_ROWS + tm, tn), F32),
                        pltpu.VMEM((tn // V7X_LANES, tm, V7X_LANES), F32)],
        compiler_params=_params("arbitrary"),
        name="short_conv_in_sample",
    )(xs, g, w_in, w_in, w_in, taps, buf)

    tn = COL_TILE
    nj = d // tn
    assert d % tn == 0
    npt = mp // tm
    gated_p, tail = pl.pallas_call(
        functools.partial(_sc_in_prompt_kernel, tiles_per_seq=seq_len_p // tm),
        grid=(npt, nj),
        in_specs=[
            pl.BlockSpec((tm, d), lambda i, j: (i, 0)),
            pl.BlockSpec((1, d), lambda i, j: (0, 0)),
            pl.BlockSpec((d, tn), lambda i, j: (0, j)),
            pl.BlockSpec((d, tn), lambda i, j: (0, j)),
            pl.BlockSpec((d, tn), lambda i, j: (0, j)),
            _layer_spec((k, tn), layer, lambda i, j: (0, j)),
        ],
        out_specs=[pl.BlockSpec((tm, tn), lambda i, j: (i, j)),
                   pl.BlockSpec((V7X_SUBLANES, tn), lambda i, j: (i, j))],
        out_shape=[jax.ShapeDtypeStruct((mp, d), BF16),
                   jax.ShapeDtypeStruct((npt * V7X_SUBLANES, d), F32)],
        scratch_shapes=[pltpu.VMEM((tm, d), BF16), pltpu.VMEM((nj, HISTORY_ROWS, tn), F32),
                        pltpu.VMEM((HISTORY_ROWS + tm, tn), F32)],
        compiler_params=_params("arbitrary", "arbitrary"),
        name="short_conv_in_prompt",
    )(xp, g, wb, wc, wv, taps)
    return (gated_p, gated_s), tail, nstate


def _res_kernel(*refs, n_prompt_tiles, res_pair, a_pair):
    refs = list(refs)
    res_refs = [refs.pop(0) for _ in range(2 if res_pair else 1)]
    a_refs = [refs.pop(0) for _ in range(2 if a_pair else 1)]
    w_ref, out_ref, wb_sc = refs
    i = pl.program_id(1)

    @pl.when(i == 0)
    def _():
        wb_sc[...] = w_ref[...].astype(BF16)

    def body(which):
        out_ref[...] = res_refs[which * res_pair][...] + _dot(a_refs[which * a_pair][...], wb_sc[...])

    if not (res_pair or a_pair):
        body(0)
        return
    pl.when(i < n_prompt_tiles)(lambda: body(0))
    pl.when(i >= n_prompt_tiles)(lambda: body(1))


def _matmul_residual(res, a, w, layer, *, n_prompt_rows):
    res_pair, a_pair = isinstance(res, tuple), isinstance(a, tuple)
    _, k, n = w.shape
    m = sum(r.shape[0] for r in res) if res_pair else res.shape[0]
    tm = HALF_ROW_TILE
    tn = min(n, WEIGHT_SLAB_BYTES // (4 * k))
    assert n_prompt_rows % tm == 0 and m % tm == 0 and n % tn == 0 and tn % V7X_LANES == 0
    npt = n_prompt_rows // tm

    def row_specs(pair, block, col_of):
        if not pair:
            return [pl.BlockSpec(block, lambda j, i: (i, col_of(j)))]
        return [pl.BlockSpec(block, lambda j, i: (jnp.minimum(i, npt - 1), col_of(j))),
                pl.BlockSpec(block, lambda j, i: (jnp.maximum(i - npt, 0), col_of(j)),
                             pipeline_mode=pl.Buffered(1))]

    return pl.pallas_call(
        functools.partial(_res_kernel, n_prompt_tiles=npt, res_pair=res_pair, a_pair=a_pair),
        grid=(n // tn, m // tm),
        in_specs=row_specs(res_pair, (tm, tn), lambda j: j) + row_specs(a_pair, (tm, k), lambda j: 0)
        + [pl.BlockSpec((None, k, tn), lambda j, i: (layer, 0, j), pipeline_mode=pl.Buffered(1))],
        out_specs=pl.BlockSpec((tm, tn), lambda j, i: (i, j)),
        out_shape=jax.ShapeDtypeStruct((m, n), F32),
        scratch_shapes=[pltpu.VMEM((k, tn), BF16)],
        compiler_params=_params("arbitrary", "arbitrary"),
        name="matmul_residual",
    )(*(res if res_pair else (res,)), *(a if a_pair else (a,)), w)


def _ffn_kernel(h_ref, g_ref, wg_ref, wu_ref, wd_ref, out_ref, *rest):
    *cast_refs, hn_sc = rest

    @pl.when(pl.program_id(1) == 0)
    def _():
        h = h_ref[...]
        hn_sc[...] = _rmsnorm(h, g_ref[...]).astype(BF16)
        out_ref[...] = h

    wg, wu, wd = wg_ref[...], wu_ref[...], wd_ref[...]
    if cast_refs:
        wg, wu, wd = wg.astype(BF16), wu.astype(BF16), wd.astype(BF16)
        for ref, w in zip(cast_refs, (wg, wu, wd)):
            ref[...] = w
    hn = hn_sc[...]
    gate = _dot(hn, wg)
    act = (_silu(gate) * _dot(hn, wu)).astype(BF16)
    out_ref[...] += _dot(act, wd)


def _ffn(h, g, w_gate, w_up, w_down, layer, *, n_prompt_rows):
    m, d = h.shape
    f = w_gate.shape[2]
    tm = ROW_TILE
    assert n_prompt_rows % tm == 0 and m - n_prompt_rows == tm
    assert f % COL_TILE == 0 and f % NARROW_COL_TILE == 0
    npt = n_prompt_rows // tm

    def call(row0_tile, n_row_tiles, tf, weights, weight_specs, cast_outputs, name):
        out_specs = [pl.BlockSpec((tm, d), lambda i, j: (i, 0))]
        out_shape = [jax.ShapeDtypeStruct((n_row_tiles * tm, d), F32)]
        if cast_outputs:
            out_specs += [pl.BlockSpec((d, tf), lambda i, j: (0, j)), pl.BlockSpec((d, tf), lambda i, j: (0, j)),
                          pl.BlockSpec((tf, d), lambda i, j: (j, 0))]
            out_shape += [jax.ShapeDtypeStruct((d, f), BF16), jax.ShapeDtypeStruct((d, f), BF16),
                          jax.ShapeDtypeStruct((f, d), BF16)]
        h_mode = dict(pipeline_mode=pl.Buffered(1)) if n_row_tiles == 1 else {}
        return pl.pallas_call(
            _ffn_kernel,
            grid=(n_row_tiles, f // tf),
            in_specs=[pl.BlockSpec((tm, d), lambda i, j: (i + row0_tile, 0), **h_mode),
                      pl.BlockSpec((1, d), lambda i, j: (0, 0))] + weight_specs(tf),
            out_specs=out_specs,
            out_shape=out_shape,
            scratch_shapes=[pltpu.VMEM((tm, d), BF16)],
            compiler_params=_params("arbitrary", "arbitrary"),
            name=name,
        )(h, g, *weights)

    f32_specs = lambda tf: [_layer_spec((d, tf), layer, lambda i, j: (0, j)),
                            _layer_spec((d, tf), layer, lambda i, j: (0, j)),
                            _layer_spec((tf, d), layer, lambda i, j: (j, 0))]
    bf16_specs = lambda tf: [pl.BlockSpec((d, tf), lambda i, j: (0, j)), pl.BlockSpec((d, tf), lambda i, j: (0, j)),
                             pl.BlockSpec((tf, d), lambda i, j: (j, 0))]
    sample, wg_b, wu_b, wd_b = call(npt, 1, NARROW_COL_TILE, (w_gate, w_up, w_down), f32_specs, True,
                                    "swiglu_ffn_sample")
    prompt, = call(0, npt, COL_TILE, (wg_b, wu_b, wd_b), bf16_specs, False, "swiglu_ffn_prompt")
    return prompt, sample


def _ple_kernel(h_ref, p_ref, g_ref, wg_ref, wp_ref, *rest, cast, final):
    if cast:
        *rest, wgb_ref, wpb_ref = rest

        @pl.when(pl.program_id(0) == 0)
        def _():
            wgb_ref[...] = wg_ref[...].astype(BF16)
            wpb_ref[...] = wp_ref[...].astype(BF16)
    else:
        wgb_ref, wpb_ref = wg_ref, wp_ref
    h = h_ref[...]
    gate = jax.nn.sigmoid(_dot(_rmsnorm(h, g_ref[...]).astype(BF16), wgb_ref[...]))
    out = h + _dot(p_ref[...].astype(BF16), wpb_ref[...]) * gate
    if final:
        gf_ref, out_ref = rest
        out_ref[...] = _rmsnorm(out, gf_ref[...])
    else:
        rest[0][...] = out


def _ple(h, pp, ps, g, w_gate, w_proj, layer, g_final=None):
    hp, hs = h
    d = hp.shape[1]
    pdim = pp.shape[2]
    final = g_final is not None
    const = lambda i: (0, 0)

    def call(rows, p, tm, weights, weight_specs, cast, name):
        m = rows.shape[0]
        assert m % tm == 0
        in_specs = [pl.BlockSpec((tm, d), lambda i: (i, 0)),
                    _layer_spec((tm, pdim), layer, lambda i: (i, 0)),
                    pl.BlockSpec((1, d), const)] + weight_specs
        args = [rows, p, g, *weights]
        if final:
            in_specs.append(pl.BlockSpec((1, d), const))
            args.append(g_final)
        out_specs = [pl.BlockSpec((tm, d), lambda i: (i, 0))]
        out_shape = [jax.ShapeDtypeStruct((m, d), F32)]
        if cast:
            out_specs += [pl.BlockSpec((d, d), const), pl.BlockSpec((pdim, d), const)]
            out_shape += [jax.ShapeDtypeStruct((d, d), BF16), jax.ShapeDtypeStruct((pdim, d), BF16)]
        return pl.pallas_call(
            functools.partial(_ple_kernel, cast=cast, final=final),
            grid=(m // tm,),
            in_specs=in_specs,
            out_specs=out_specs,
            out_shape=out_shape,
            compiler_params=_params("arbitrary"),
            name=name,
        )(*args)

    resident = dict(pipeline_mode=pl.Buffered(1))
    f32_specs = [pl.BlockSpec((None, d, d), lambda i: (layer, 0, 0), **resident),
                 pl.BlockSpec((None, pdim, d), lambda i: (layer, 0, 0), **resident)]
    bf16_specs = [pl.BlockSpec((d, d), const, **resident), pl.BlockSpec((pdim, d), const, **resident)]
    tag = "ple_final" if final else "ple"
    out_s, wg_b, wp_b = call(hs, ps, PLE_ROW_TILE, (w_gate, w_proj), f32_specs, True, tag + "_sample")
    out_p, = call(hp, pp, HALF_ROW_TILE, (wg_b, wp_b), bf16_specs, False, tag + "_prompt")
    return out_p, out_s


def _ssd_in_kernel(h_ref, g_ref, wt_ref, wdtT_ref, dtb_ref, taps_ref, cb_ref, *rest,
                   sample_len, tiles_per_seq, n_z_tiles):
    if sample_len is not None:
        (buf_ref, zs_ref, xbc_ref, nstate_ref, dt_ref, dtT_ref, wtb_ref,
         hn_sc, hist_sc, stage_sc) = rest
        i, j = 0, pl.program_id(0)
    else:
        zs_ref, xbc_ref, tail_ref, dt_ref, dtT_ref, hn_sc, carry_sc, hist_sc = rest
        i, j = pl.program_id(0), pl.program_id(1)
    jc = j - n_z_tiles
    tm = hn_sc.shape[0]

    @pl.when(j == 0)
    def _():
        hn = _rmsnorm(h_ref[...], g_ref[...]).astype(BF16)
        hn_sc[...] = hn
        dt = _softplus(_dot_nt(hn, wdtT_ref[...]) + dtb_ref[...])
        dt_ref[...] = dt
        dtT_ref[...] = dt.T

    def weight_tile():
        if sample_len is None:
            return wt_ref[...]
        wb = wt_ref[...].astype(BF16)
        wtb_ref[...] = wb
        return wb

    def finish(conv):
        return _silu(conv + cb_ref[...]).astype(BF16)

    @pl.when(j < n_z_tiles)
    def _():
        zs_ref[...] = _silu(_dot_nt(hn_sc[...], weight_tile())).astype(BF16)

    @pl.when(j >= n_z_tiles)
    def _():
        hist_sc[HISTORY_ROWS:, :] = _dot_nt(hn_sc[...], weight_tile())
        taps = taps_ref[...]
        if sample_len is None:
            hist_sc[:HISTORY_ROWS, :] = jnp.where(i % tiles_per_seq == 0, 0.0, carry_sc[jc])
            xbc_ref[...] = finish(_conv(hist_sc, taps))
            carry_sc[jc] = hist_sc[tm:, :]
            tail_ref[...] = hist_sc[HISTORY_ROWS + tm - V7X_SUBLANES:, :]
        else:
            _state_correction(buf_ref, taps, stage_sc, sample_len)
            hist_sc[:HISTORY_ROWS, :] = jnp.zeros((HISTORY_ROWS, hist_sc.shape[1]), F32)
            xbc_ref[...] = finish(_conv(hist_sc, taps, sample_len) + _staged(stage_sc))
            _emit_sample_state(hist_sc[HISTORY_ROWS:, :], stage_sc, nstate_ref, sample_len)


def _ssd_in(h_p, h_s, g, w_in_t, layer, w_dt_t, dt_b, taps, conv_b, buf, *, seq_len_p, seq_len_s, d_inner):
    n_prompt_rows, d = h_p.shape
    ms = h_s.shape[0]
    _, k, conv_dim = taps.shape
    hp = w_dt_t.shape[0]
    tm, tn = ROW_TILE, COL_TILE
    assert n_prompt_rows % tm == 0 and ms == tm and seq_len_p % tm == 0 and tm % seq_len_s == 0
    assert d_inner % tn == 0 and conv_dim % tn == 0 and k - 1 <= min(seq_len_s, V7X_SUBLANES)
    npt = n_prompt_rows // tm
    n_seq_s = ms // seq_len_s
    nz, nc = d_inner // tn, conv_dim // tn
    zc = lambda j: jnp.minimum(j, nz - 1)
    cc = lambda j: jnp.maximum(j - nz, 0)

    zs_s, xbc_s, nstate, dt_s, dtT_s, w_b = pl.pallas_call(
        functools.partial(_ssd_in_kernel, sample_len=seq_len_s, tiles_per_seq=None, n_z_tiles=nz),
        grid=(nz + nc,),
        in_specs=[
            pl.BlockSpec((tm, d), lambda j: (0, 0)),
            pl.BlockSpec((1, d), lambda j: (0, 0)),
            _layer_spec((tn, d), layer, lambda j: (j, 0)),
            pl.BlockSpec((hp, d), lambda j: (0, 0)),
            pl.BlockSpec((1, hp), lambda j: (0, 0)),
            _layer_spec((k, tn), layer, lambda j: (0, cc(j))),
            _layer_spec((1, tn), layer, lambda j: (0, cc(j))),
            _layer_spec((k - 1, n_seq_s, tn), layer, lambda j: (0, 0, cc(j))),
        ],
        out_specs=[
            pl.BlockSpec((tm, tn), lambda j: (0, zc(j))),
            pl.BlockSpec((tm, tn), lambda j: (0, cc(j))),
            pl.BlockSpec((k - 1, n_seq_s, tn), lambda j: (0, 0, cc(j))),
            pl.BlockSpec((tm, hp), lambda j: (0, 0)),
            pl.BlockSpec((hp, tm), lambda j: (0, 0)),
            pl.BlockSpec((tn, d), lambda j: (j, 0)),
        ],
        out_shape=[
            jax.ShapeDtypeStruct((ms, d_inner), BF16),
            jax.ShapeDtypeStruct((ms, conv_dim), BF16),
            jax.ShapeDtypeStruct((k - 1, n_seq_s, conv_dim), F32),
            jax.ShapeDtypeStruct((ms, hp), F32),
            jax.ShapeDtypeStruct((hp, ms), F32),
            jax.ShapeDtypeStruct(((nz + nc) * tn, d), BF16),
        ],
        scratch_shapes=[pltpu.VMEM((tm, d), BF16), pltpu.VMEM((HISTORY_ROWS + tm, tn), F32),
                        pltpu.VMEM((tn // V7X_LANES, tm, V7X_LANES), F32)],
        compiler_params=_params("arbitrary"),
        name="ssd_in_sample",
    )(h_s, g, w_in_t, w_dt_t, dt_b, taps, conv_b, buf)

    tn = WIDE_COL_TILE
    assert d_inner % tn == 0 and conv_dim % tn == 0
    nz, nc = d_inner // tn, conv_dim // tn
    const = lambda i, j: (0, 0)
    zs_p, xbc_p, tail, dt_p, dtT_p = pl.pallas_call(
        functools.partial(_ssd_in_kernel, sample_len=None, tiles_per_seq=seq_len_p // tm, n_z_tiles=nz),
        grid=(npt, nz + nc),
        in_specs=[
            pl.BlockSpec((tm, d), lambda i, j: (i, 0)),
            pl.BlockSpec((1, d), const),
            pl.BlockSpec((tn, d), lambda i, j: (j, 0)),
            pl.BlockSpec((hp, d), const),
            pl.BlockSpec((1, hp), const),
            _layer_spec((k, tn), layer, lambda i, j: (0, cc(j))),
            _layer_spec((1, tn), layer, lambda i, j: (0, cc(j))),
        ],
        out_specs=[
            pl.BlockSpec((tm, tn), lambda i, j: (i, zc(j))),
            pl.BlockSpec((tm, tn), lambda i, j: (i, cc(j))),
            pl.BlockSpec((V7X_SUBLANES, tn), lambda i, j: (i, cc(j))),
            pl.BlockSpec((tm, hp), lambda i, j: (i, 0)),
            pl.BlockSpec((hp, tm), lambda i, j: (0, i)),
        ],
        out_shape=[
            jax.ShapeDtypeStruct((n_prompt_rows, d_inner), BF16),
            jax.ShapeDtypeStruct((n_prompt_rows, conv_dim), BF16),
            jax.ShapeDtypeStruct((npt * V7X_SUBLANES, conv_dim), F32),
            jax.ShapeDtypeStruct((n_prompt_rows, hp), F32),
            jax.ShapeDtypeStruct((hp, n_prompt_rows), F32),
        ],
        scratch_shapes=[pltpu.VMEM((tm, d), BF16), pltpu.VMEM((nc, HISTORY_ROWS, tn), F32),
                        pltpu.VMEM((HISTORY_ROWS + tm, tn), F32)],
        compiler_params=_params("arbitrary", "arbitrary"),
        name="ssd_in_prompt",
    )(h_p, g, w_b, w_dt_t, dt_b, taps, conv_b)
    return ((zs_p, zs_s), (xbc_p, xbc_s), (dt_p, dt_s), (dtT_p, dtT_s)), tail, nstate


def _ssd_group_out(x, zs, cb, acum, acumT, dtT, mask, extra, ng, head0, heads_per_group, head_dim,
                   carried=None):
    heads_per_slab = V7X_LANES // head_dim
    parts = []
    for q in range(heads_per_group // heads_per_slab):
        cols = slice(q * V7X_LANES, (q + 1) * V7X_LANES)
        rhs = x[:, cols]
        if carried is not None:
            cg, st_t = carried
            rhs = jnp.concatenate([rhs, st_t[:, cols].astype(BF16)], axis=0)
        lanes = lax.broadcasted_iota(jnp.int32, rhs.shape, 1)
        acc = None
        for r in range(heads_per_slab):
            hd = head0 + q * heads_per_slab + r
            a_t = jnp.broadcast_to(acum[:, hd:hd + 1], cb.shape)
            seg = a_t - acumT[hd:hd + 1, :]
            lhs = (cb * jnp.exp(jnp.where(mask, seg, MASKED)) * dtT[hd:hd + 1, :]).astype(BF16)
            if carried is not None:
                lhs = jnp.concatenate([lhs, (cg * jnp.exp(a_t)).astype(BF16)], axis=1)
            in_head = jnp.logical_and(lanes >= r * head_dim, lanes < (r + 1) * head_dim)
            part = _dot(lhs, jnp.where(in_head, rhs, jnp.zeros_like(rhs)))
            acc = part if acc is None else acc + part
        parts.append(acc)
    y = jnp.concatenate(parts, axis=1) + extra
    gated = y * zs
    ms = jnp.mean(gated * gated, axis=-1, keepdims=True)
    return (gated * lax.rsqrt(ms + EPS) * ng).astype(BF16)


def _ssd_prompt_body(c, n_chunks, xs_ref, b_ref, c_ref, zs_ref, dt_ref, dtT_ref, alr_ref, alc_ref,
                     e_ref, d_ref, ng_ref, y_ref, state_ref, st_sc, *, n_groups, head_dim, d_state):
    @pl.when(c == 0)
    def _():
        st_sc[...] = jnp.zeros_like(st_sc)

    q_rows, d_inner = xs_ref.shape
    gw = d_inner // n_groups
    hpg = gw // head_dim
    row = lax.broadcasted_iota(jnp.int32, (q_rows, q_rows), 0)
    col = lax.broadcasted_iota(jnp.int32, (q_rows, q_rows), 1)
    causal = col <= row
    tril = jnp.where(causal, 1.0, 0.0).astype(BF16)
    triu = jnp.where(row <= col, 1.0, 0.0).astype(BF16)
    dt = dt_ref[...]
    dtT = dtT_ref[...]
    acum = _dot01_lhs(tril, dt * -jnp.exp(alr_ref[...]))
    acumT = _dot01_rhs(dtT * -jnp.exp(alc_ref[...]), triu)
    a_end = acum[q_rows - 1:q_rows, :]
    e = e_ref[...]
    x = xs_ref[...]
    xf = x.astype(F32)
    to_end = (xf * _dot((dt * jnp.exp(a_end - acum)).astype(BF16), e)).astype(BF16)
    decay = _dot01_rhs(jnp.broadcast_to(jnp.exp(a_end), (V7X_SUBLANES, a_end.shape[1])), e)[:1, :]
    skip = xf * d_ref[...]
    for g in range(n_groups):
        sl = slice(g * gw, (g + 1) * gw)
        ns = slice(g * d_state, (g + 1) * d_state)
        bg, cg = b_ref[:, ns], c_ref[:, ns]
        st = st_sc[:, sl]
        y_ref[:, sl] = _ssd_group_out(x[:, sl], zs_ref[:, sl].astype(F32), _dot_nt(cg, bg), acum,
                                      acumT, dtT, causal, skip[:, sl], ng_ref[:, sl],
                                      g * hpg, hpg, head_dim,
                                      carried=(cg.astype(F32), st))
        st_sc[:, sl] = decay[:, sl] * st + _dot_tn(bg, to_end[:, sl])

    @pl.when(c == n_chunks - 1)
    def _():
        for g in range(n_groups):
            state_ref[g * gw:(g + 1) * gw, :] = st_sc[:, g * gw:(g + 1) * gw].T


def _ssd_sample_body(xs_ref, b_ref, c_ref, zs_ref, dt_ref, dtT_ref, alr_ref, alc_ref, e_ref,
                     d_ref, ng_ref, st_ref, y_ref, nst_ref, *, seq_len, head_dim):
    q_rows, gw = xs_ref.shape
    n_seq = q_rows // seq_len
    row = lax.broadcasted_iota(jnp.int32, (q_rows, q_rows), 0)
    col = lax.broadcasted_iota(jnp.int32, (q_rows, q_rows), 1)
    same = (row // seq_len) == (col // seq_len)
    mask = jnp.logical_and(same, col <= row)
    tril = jnp.where(mask, 1.0, 0.0).astype(BF16)
    triu = jnp.where(jnp.logical_and(same, row <= col), 1.0, 0.0).astype(BF16)
    ends = jnp.where(col == (row // seq_len) * seq_len + (seq_len - 1), 1.0, 0.0).astype(BF16)
    dt = dt_ref[...]
    dtT = dtT_ref[...]
    acum = _dot01_lhs(tril, dt * -jnp.exp(alr_ref[...]))
    acumT = _dot01_rhs(dtT * -jnp.exp(alc_ref[...]), triu)
    a_end = _dot01_lhs(ends, acum)
    e = e_ref[...]
    x = xs_ref[...].astype(F32)
    to_endT = (x * _dot((dt * jnp.exp(a_end - acum)).astype(BF16), e)).T.astype(BF16)
    decayT = _dot01_rhs(jnp.exp(a_end), e).T
    from_start = _spread(jnp.exp(acum), e)
    bg = b_ref[...].astype(F32)
    cg = c_ref[...].astype(F32)
    seq_of_row = lax.broadcasted_iota(jnp.int32, bg.shape, 0) // seq_len
    inter = jnp.zeros((q_rows, gw), F32)
    for s in range(n_seq):
        st = st_ref[s]
        mine = seq_of_row == s
        inter = inter + _dot_nt(jnp.where(mine, cg, 0.0).astype(BF16), st.astype(BF16))
        bm = jnp.where(mine, bg, 0.0).astype(BF16)
        nst_ref[s] = decayT[:, s * seq_len:s * seq_len + 1] * st + _dot(to_endT, bm)
    inter = from_start * inter + x * d_ref[...]
    y_ref[...] = _ssd_group_out(xs_ref[...], zs_ref[...].astype(F32), _dot_nt(c_ref[...], b_ref[...]),
                                acum, acumT, dtT, mask, inter, ng_ref[...], 0, gw // head_dim, head_dim)


N_PROMPT_SCAN_INPUTS = 11
N_SAMPLE_SCAN_INPUTS = 12


def _ssd_scan_kernel(*refs, n_chunks, n_groups, seq_len_s, head_dim, d_state):
    a, b = N_PROMPT_SCAN_INPUTS, N_PROMPT_SCAN_INPUTS + N_SAMPLE_SCAN_INPUTS
    prompt_in, sample_in = refs[:a], refs[a:b]
    y_p, state_p, y_s, state_s, st_sc = refs[b:]
    _ssd_prompt_body(pl.program_id(0) % n_chunks, n_chunks, *prompt_in, y_p, state_p, st_sc,
                     n_groups=n_groups, head_dim=head_dim, d_state=d_state)
    _ssd_sample_body(*sample_in, y_s, state_s, seq_len=seq_len_s, head_dim=head_dim)


def _ssd_scan(xbc_p, zs_p, dt, dtT, alog_row, alog_col, expand, d_x, ng,
              xbc_s, zs_s, dt_g, dtT_g, alog_row_g, alog_col_g, state,
              *, n_seq_p, seq_len_p, seq_len_s, d_inner, n_groups, head_dim, d_state):
    q = SSD_CHUNK
    n_rows_s = xbc_s.shape[0]
    assert seq_len_p % q == 0 and q == d_state
    assert n_rows_s % q == 0 and q % seq_len_s == 0
    nc = seq_len_p // q
    nb = n_rows_s // q
    assert n_seq_p * nc == nb * n_groups
    spb = q // seq_len_s
    gw = d_inner // n_groups
    hp = dt.shape[1]
    hpg_rows = dtT_g.shape[1]
    gn = n_groups * d_state
    assert d_inner % gn == 0
    b_col0 = d_inner // d_state
    const = lambda t: (0, 0)
    sb = lambda t: t // n_groups
    g = lambda t: t % n_groups
    prompt_specs = [
        pl.BlockSpec((q, d_inner), lambda t: (t, 0)),
        pl.BlockSpec((q, gn), lambda t: (t, d_inner // gn)),
        pl.BlockSpec((q, gn), lambda t: (t, d_inner // gn + 1)),
        pl.BlockSpec((q, d_inner), lambda t: (t, 0)),
        pl.BlockSpec((q, hp), lambda t: (t, 0)),
        pl.BlockSpec((hp, q), lambda t: (0, t)),
        pl.BlockSpec((1, hp), const),
        pl.BlockSpec((hp, 1), const),
        pl.BlockSpec((hp, d_inner), const),
        pl.BlockSpec((1, d_inner), const),
        pl.BlockSpec((1, d_inner), const),
    ]
    sample_specs = [
        pl.BlockSpec((q, gw), lambda t: (sb(t), g(t))),
        pl.BlockSpec((q, d_state), lambda t: (sb(t), b_col0 + g(t))),
        pl.BlockSpec((q, d_state), lambda t: (sb(t), b_col0 + n_groups + g(t))),
        pl.BlockSpec((q, gw), lambda t: (sb(t), g(t))),
        pl.BlockSpec((None, q, hp), lambda t: (g(t), sb(t), 0)),
        pl.BlockSpec((None, hpg_rows, q), lambda t: (g(t), 0, sb(t))),
        pl.BlockSpec((None, 1, hp), lambda t: (g(t), 0, 0)),
        pl.BlockSpec((None, hpg_rows, 1), lambda t: (g(t), 0, 0)),
        pl.BlockSpec((hp, gw), const),
        pl.BlockSpec((1, gw), lambda t: (0, g(t))),
        pl.BlockSpec((1, gw), lambda t: (0, g(t))),
        pl.BlockSpec((spb, None, gw, d_state), lambda t: (sb(t), g(t), 0, 0)),
    ]
    assert len(prompt_specs) == N_PROMPT_SCAN_INPUTS and len(sample_specs) == N_SAMPLE_SCAN_INPUTS
    kern = functools.partial(_ssd_scan_kernel, n_chunks=nc, n_groups=n_groups, seq_len_s=seq_len_s,
                             head_dim=head_dim, d_state=d_state)
    return pl.pallas_call(
        kern,
        grid=(n_seq_p * nc,),
        in_specs=prompt_specs + sample_specs,
        out_specs=[
            pl.BlockSpec((q, d_inner), lambda t: (t, 0)),
            pl.BlockSpec((d_inner, d_state), lambda t: (t // nc, 0)),
            pl.BlockSpec((q, gw), lambda t: (sb(t), g(t))),
            pl.BlockSpec((spb, None, gw, d_state), lambda t: (sb(t), g(t), 0, 0)),
        ],
        out_shape=[
            jax.ShapeDtypeStruct((n_seq_p * seq_len_p, d_inner), BF16),
            jax.ShapeDtypeStruct((n_seq_p * d_inner, d_state), F32),
            jax.ShapeDtypeStruct((n_rows_s, d_inner), BF16),
            jax.ShapeDtypeStruct(state.shape, F32),
        ],
        scratch_shapes=[pltpu.VMEM((d_state, d_inner), F32)],
        compiler_params=_params("arbitrary"),
        name="ssd_scan",
    )(xbc_p, xbc_p, xbc_p, zs_p, dt, dtT, alog_row, alog_col, expand, d_x, ng,
      xbc_s, xbc_s, xbc_s, zs_s, dt_g, dtT_g, alog_row_g, alog_col_g, expand[:, :gw], d_x, ng, state)


def _prompt_conv_state(tail, *, n_prompt_tiles, tiles_per_seq, km1):
    t = tail.reshape(-1, V7X_SUBLANES, tail.shape[1])[:n_prompt_tiles]
    return t[tiles_per_seq - 1::tiles_per_seq, V7X_SUBLANES - km1:, :]


def kernel(x_prompt, x_sample, p_prompt, p_sample, state_sc_conv, state_ssd_conv, state_ssd, g_mix, g_ffn, g_ple, g_final, sc_w_in, sc_w_conv, sc_w_out, ssd_w_in, ssd_conv_w, ssd_conv_b, ssd_dt_bias, ssd_a_log, ssd_d, ssd_norm_g, ssd_w_out, ffn_w_gate, ffn_w_up, ffn_w_down, ple_w_proj, ple_w_gate):
    bp, lp, d = x_prompt.shape
    bs, ls, _ = x_sample.shape
    depth = g_mix.shape[0]
    mp, ms = bp * lp, bs * ls
    pdim = p_prompt.shape[-1]
    n_heads, head_dim, d_state = state_ssd.shape[2:]
    d_inner = n_heads * head_dim
    conv_dim = ssd_conv_w.shape[-1]
    n_groups = (conv_dim - d_inner) // (2 * d_state)
    hpg = n_heads // n_groups
    assert n_heads <= V7X_LANES and V7X_LANES % head_dim == 0 and d_state == V7X_LANES
    npt = mp // ROW_TILE
    tps = lp // ROW_TILE
    row = lambda v: v.reshape(1, -1)
    pp = p_prompt.reshape(depth, mp, pdim)
    ps = p_sample.reshape(depth, ms, pdim)

    h = (x_prompt.reshape(mp, d), x_sample.reshape(ms, d))
    sc_p, sc_s, cv_p, cv_s, st_p, st_s = [], [], [], [], [], []
    for i in range(depth):
        j = i // 2
        if i % 2 == 0:
            km1 = sc_w_conv.shape[1] - 1
            gated, tail, nstate = _short_conv_in(
                *h, row(g_mix[i]), sc_w_in, j, sc_w_conv, jnp.swapaxes(state_sc_conv, 1, 2),
                seq_len_p=lp, seq_len_s=ls)
            sc_p.append(_prompt_conv_state(tail, n_prompt_tiles=npt, tiles_per_seq=tps, km1=km1))
            sc_s.append(jnp.swapaxes(nstate, 0, 1))
            h = _matmul_residual(h, gated, sc_w_out, j, n_prompt_rows=mp)
        else:
            km1 = ssd_conv_w.shape[1] - 1
            zx = d_inner + conv_dim
            pad_h = V7X_LANES - n_heads
            w_in_t = jnp.swapaxes(ssd_w_in, 1, 2)
            w_dt_t = jnp.pad(w_in_t[j, zx:, :], ((0, pad_h), (0, 0))).astype(BF16)
            dt_b = jnp.pad(ssd_dt_bias[j], (0, pad_h))
            alog = jnp.pad(ssd_a_log[j], (0, pad_h))
            (zs, xbc, dt, dtT), tail, nstate = _ssd_in(
                *h, row(g_mix[i]), w_in_t, j, w_dt_t, row(dt_b),
                ssd_conv_w, ssd_conv_b.reshape(ssd_conv_b.shape[0], 1, conv_dim),
                jnp.swapaxes(state_ssd_conv, 1, 2),
                seq_len_p=lp, seq_len_s=ls, d_inner=d_inner)
            cv_p.append(_prompt_conv_state(tail, n_prompt_tiles=npt, tiles_per_seq=tps, km1=km1))
            cv_s.append(jnp.swapaxes(nstate, 0, 1))
            head_of_lane = jnp.arange(d_inner, dtype=jnp.int32) // head_dim
            expand = (jnp.arange(V7X_LANES, dtype=jnp.int32)[:, None] == head_of_lane[None, :]).astype(BF16)
            d_x = row(jnp.repeat(ssd_d[j], head_dim))
            ng = row(ssd_norm_g[j])
            dt_g = jnp.stack([jnp.roll(dt[1], -g * hpg, axis=1) for g in range(n_groups)])
            alog_g = jnp.stack([jnp.roll(alog, -g * hpg) for g in range(n_groups)])
            dtT_g = dtT[1][:n_heads].reshape(n_groups, hpg, ms)
            y_p, new_p, y_s, new_s = _ssd_scan(
                xbc[0], zs[0], dt[0], dtT[0], row(alog), alog.reshape(-1, 1), expand, d_x, ng,
                xbc[1], zs[1], dt_g, dtT_g, alog_g.reshape(n_groups, 1, -1),
                ssd_a_log[j].reshape(n_groups, hpg, 1),
                state_ssd[j].reshape(bs, n_groups, hpg * head_dim, d_state),
                n_seq_p=bp, seq_len_p=lp, seq_len_s=ls,
                d_inner=d_inner, n_groups=n_groups, head_dim=head_dim, d_state=d_state)
            st_p.append(new_p.reshape(bp, n_heads, head_dim, d_state))
            st_s.append(new_s.reshape(bs, n_heads, head_dim, d_state))
            h = _matmul_residual(h, (y_p, y_s), ssd_w_out, j, n_prompt_rows=mp)
        h = _ffn(h, row(g_ffn[i]), ffn_w_gate, ffn_w_up, ffn_w_down, i, n_prompt_rows=mp)
        h = _ple(h, pp, ps, row(g_ple[i]), ple_w_gate, ple_w_proj, i,
                 g_final=row(g_final) if i == depth - 1 else None)
    y_p, y_s = h
    return (y_p.reshape(bp, lp, d), y_s.reshape(bs, ls, d), jnp.stack(sc_p), jnp.stack(sc_s),
            jnp.stack(cv_p), jnp.stack(cv_s), jnp.stack(st_p), jnp.stack(st_s))
```

```python
import functools

import jax
import jax.numpy as jnp
from jax import lax
from jax.experimental import pallas as pl
from jax.experimental.pallas import tpu as pltpu

F32 = jnp.float32
BF16 = jnp.bfloat16
EPS = 1e-6
MASKED = -1e30
V7X_LANES = 128
V7X_SUBLANES = 8
HISTORY_ROWS = 16
V7X_VMEM_LIMIT = 56 * 1024 * 1024

ROW_TILE = 1024
HALF_ROW_TILE = 512
PLE_ROW_TILE = 256
COL_TILE = 512
WIDE_COL_TILE = 2048
NARROW_COL_TILE = 256
SSD_CHUNK = 128
WEIGHT_SLAB_BYTES = 16 * 1024 * 1024


def _params(*sem):
    return pltpu.CompilerParams(dimension_semantics=sem, vmem_limit_bytes=V7X_VMEM_LIMIT)


def _dot(a, b):
    return jnp.dot(a, b, preferred_element_type=F32)


def _dot_nt(a, b):
    return lax.dot_general(a, b, (((1,), (1,)), ((), ())), preferred_element_type=F32)


def _dot_tn(a, b):
    return lax.dot_general(a, b, (((0,), (0,)), ((), ())), preferred_element_type=F32)


def _split3(a):
    a1 = a.astype(BF16)
    r1 = a - a1.astype(F32)
    a2 = r1.astype(BF16)
    a3 = (r1 - a2.astype(F32)).astype(BF16)
    return a3, a2, a1


def _dot01_rhs(a, e):
    p3, p2, p1 = _split3(a)
    return (_dot(p3, e) + _dot(p2, e)) + _dot(p1, e)


def _spread(a, e):
    hi = a.astype(BF16)
    lo = (a - hi.astype(F32)).astype(BF16)
    return _dot(lo, e) + _dot(hi, e)


def _dot01_lhs(t, a):
    p3, p2, p1 = _split3(a)
    return (_dot(t, p3) + _dot(t, p2)) + _dot(t, p1)


def _rmsnorm(x, g):
    ms = jnp.mean(x * x, axis=-1, keepdims=True)
    return x * lax.rsqrt(ms + EPS) * g


def _softplus(x):
    return jnp.maximum(x, 0.0) + jnp.log1p(jnp.exp(-jnp.abs(x)))


def _silu(x):
    return x * jax.nn.sigmoid(x)


def _layer_spec(block, layer, imap):
    return pl.BlockSpec((None,) + tuple(block), lambda *a: (layer,) + tuple(imap(*a)))


def _conv(hist_ref, taps, seq_len=None, row0=0, n_rows=None):
    k = taps.shape[0]
    if n_rows is None:
        n_rows = hist_ref.shape[0] - HISTORY_ROWS
    out = taps[k - 1:k, :] * hist_ref[pl.ds(HISTORY_ROWS + row0, n_rows), :]
    if seq_len is not None:
        assert row0 % seq_len == 0
        t = lax.broadcasted_iota(jnp.int32, out.shape, 0) % seq_len
    for d in range(1, k):
        sh = hist_ref[pl.ds(HISTORY_ROWS + row0 - d, n_rows), :]
        if seq_len is not None:
            sh = jnp.where(t >= d, sh, 0.0)
        out = out + taps[k - 1 - d:k - d, :] * sh
    return out


def _state_correction(buf_ref, taps, stage_sc, seq_len):
    k = taps.shape[0]
    km1 = k - 1
    n_seq = stage_sc.shape[1] // seq_len
    stage_sc[...] = jnp.zeros_like(stage_sc)
    rows = [buf_ref[r] for r in range(km1)]
    for t in range(km1):
        acc = None
        for d in range(t + 1, k):
            term = taps[k - 1 - d:k - d, :] * rows[km1 + t - d]
            acc = term if acc is None else acc + term
        for c in range(stage_sc.shape[0]):
            stage_sc[c, pl.ds(t, n_seq, stride=seq_len), :] = acc[:, c * V7X_LANES:(c + 1) * V7X_LANES]


def _staged(stage_sc):
    return jnp.concatenate([stage_sc[c] for c in range(stage_sc.shape[0])], axis=1)


def _emit_sample_state(u, stage_sc, nstate_ref, seq_len):
    km1 = nstate_ref.shape[0]
    n_chunks = stage_sc.shape[0]
    n_seq = stage_sc.shape[1] // seq_len
    for c in range(n_chunks):
        stage_sc[c] = u[:, c * V7X_LANES:(c + 1) * V7X_LANES]
    for r in range(km1):
        nstate_ref[r] = jnp.concatenate(
            [stage_sc[c, pl.ds(seq_len - km1 + r, n_seq, stride=seq_len), :] for c in range(n_chunks)],
            axis=1)


def _sc_in_sample_kernel(x_ref, g_ref, wb_ref, wc_ref, wv_ref, taps_ref, buf_ref,
                         gated_ref, nstate_ref, wbb_ref, wcb_ref, wvb_ref, hn_sc, hist_sc, stage_sc,
                         *, sample_len):
    @pl.when(pl.program_id(0) == 0)
    def _():
        hn_sc[...] = _rmsnorm(x_ref[...], g_ref[...]).astype(BF16)

    wb, wc, wv = wb_ref[...].astype(BF16), wc_ref[...].astype(BF16), wv_ref[...].astype(BF16)
    wbb_ref[...], wcb_ref[...], wvb_ref[...] = wb, wc, wv
    taps = taps_ref[...]
    _state_correction(buf_ref, taps, stage_sc, sample_len)
    hn = hn_sc[...]
    bg = _dot(hn, wb)
    u = _dot(hn, wc) * _dot(hn, wv)
    hist_sc[:HISTORY_ROWS, :] = jnp.zeros((HISTORY_ROWS, u.shape[1]), F32)
    hist_sc[HISTORY_ROWS:, :] = u
    gated_ref[...] = (bg * (_conv(hist_sc, taps, sample_len) + _staged(stage_sc))).astype(BF16)
    _emit_sample_state(u, stage_sc, nstate_ref, sample_len)


def _sc_in_prompt_kernel(x_ref, g_ref, wb_ref, wc_ref, wv_ref, taps_ref,
                         gated_ref, tail_ref, hn_sc, carry_sc, hist_sc, *, tiles_per_seq):
    i = pl.program_id(0)
    j = pl.program_id(1)

    @pl.when(j == 0)
    def _():
        hn_sc[...] = _rmsnorm(x_ref[...], g_ref[...]).astype(BF16)

    hn = hn_sc[...]
    bg = _dot(hn, wb_ref[...])
    u = _dot(hn, wc_ref[...]) * _dot(hn, wv_ref[...])
    hist_sc[:HISTORY_ROWS, :] = jnp.where(i % tiles_per_seq == 0, 0.0, carry_sc[j])
    hist_sc[HISTORY_ROWS:, :] = u
    gated_ref[...] = (bg * _conv(hist_sc, taps_ref[...])).astype(BF16)
    carry_sc[j] = u[u.shape[0] - HISTORY_ROWS:, :]
    tail_ref[...] = u[u.shape[0] - V7X_SUBLANES:, :]


def _short_conv_in(xp, xs, g, w_in, layer, taps, buf, *, seq_len_p, seq_len_s):
    mp, d = xp.shape
    ms = xs.shape[0]
    k = taps.shape[1]
    tm = ROW_TILE
    assert mp % tm == 0 and ms == tm and seq_len_p % tm == 0 and tm % seq_len_s == 0
    assert k - 1 <= min(seq_len_s, V7X_SUBLANES)
    n_seq_s = ms // seq_len_s

    tn = NARROW_COL_TILE
    nj = d // tn
    assert d % tn == 0 and tn % V7X_LANES == 0
    gated_s, nstate, wb, wc, wv = pl.pallas_call(
        functools.partial(_sc_in_sample_kernel, sample_len=seq_len_s),
        grid=(nj,),
        in_specs=[
            pl.BlockSpec((tm, d), lambda j: (0, 0)),
            pl.BlockSpec((1, d), lambda j: (0, 0)),
            _layer_spec((d, tn), layer, lambda j: (0, j)),
            _layer_spec((d, tn), layer, lambda j: (0, nj + j)),
            _layer_spec((d, tn), layer, lambda j: (0, 2 * nj + j)),
            _layer_spec((k, tn), layer, lambda j: (0, j)),
            _layer_spec((k - 1, n_seq_s, tn), layer, lambda j: (0, 0, j)),
        ],
        out_specs=[pl.BlockSpec((tm, tn), lambda j: (0, j)),
                   pl.BlockSpec((k - 1, n_seq_s, tn), lambda j: (0, 0, j))]
        + [pl.BlockSpec((d, tn), lambda j: (0, j))] * 3,
        out_shape=[jax.ShapeDtypeStruct((ms, d), BF16), jax.ShapeDtypeStruct((k - 1, n_seq_s, d), F32)]
        + [jax.ShapeDtypeStruct((d, d), BF16)] * 3,
        scratch_shapes=[pltpu.VMEM((tm, d), BF16), pltpu.VMEM((HISTORY_ROWS + tm, tn), F32),
                        pltpu.VMEM((tn // V7X_LANES, tm, V7X_LANES), F32)],
        compiler_params=_params("arbitrary"),
        name="short_conv_in_sample",
    )(xs, g, w_in, w_in, w_in, taps, buf)

    tn = COL_TILE
    nj = d // tn
    assert d % tn == 0
    npt = mp // tm
    gated_p, tail = pl.pallas_call(
        functools.partial(_sc_in_prompt_kernel, tiles_per_seq=seq_len_p // tm),
        grid=(npt, nj),
        in_specs=[
            pl.BlockSpec((tm, d), lambda i, j: (i, 0)),
            pl.BlockSpec((1, d), lambda i, j: (0, 0)),
            pl.BlockSpec((d, tn), lambda i, j: (0, j)),
            pl.BlockSpec((d, tn), lambda i, j: (0, j)),
            pl.BlockSpec((d, tn), lambda i, j: (0, j)),
            _layer_spec((k, tn), layer, lambda i, j: (0, j)),
        ],
        out_specs=[pl.BlockSpec((tm, tn), lambda i, j: (i, j)),
                   pl.BlockSpec((V7X_SUBLANES, tn), lambda i, j: (i, j))],
        out_shape=[jax.ShapeDtypeStruct((mp, d), BF16),
                   jax.ShapeDtypeStruct((npt * V7X_SUBLANES, d), F32)],
        scratch_shapes=[pltpu.VMEM((tm, d), BF16), pltpu.VMEM((nj, HISTORY_ROWS, tn), F32),
                        pltpu.VMEM((HISTORY_ROWS + tm, tn), F32)],
        compiler_params=_params("arbitrary", "arbitrary"),
        name="short_conv_in_prompt",
    )(xp, g, wb, wc, wv, taps)
    return (gated_p, gated_s), tail, nstate


def _res_kernel(*refs, n_prompt_tiles, res_pair, a_pair):
    refs = list(refs)
    res_refs = [refs.pop(0) for _ in range(2 if res_pair else 1)]
    a_refs = [refs.pop(0) for _ in range(2 if a_pair else 1)]
    w_ref, out_ref, wb_sc = refs
    i = pl.program_id(1)

    @pl.when(i == 0)
    def _():
        wb_sc[...] = w_ref[...].astype(BF16)

    def body(which):
        out_ref[...] = res_refs[which * res_pair][...] + _dot(a_refs[which * a_pair][...], wb_sc[...])

    if not (res_pair or a_pair):
        body(0)
        return
    pl.when(i < n_prompt_tiles)(lambda: body(0))
    pl.when(i >= n_prompt_tiles)(lambda: body(1))


def _matmul_residual(res, a, w, layer, *, n_prompt_rows):
    res_pair, a_pair = isinstance(res, tuple), isinstance(a, tuple)
    _, k, n = w.shape
    m = sum(r.shape[0] for r in res) if res_pair else res.shape[0]
    tm = HALF_ROW_TILE
    tn = min(n, WEIGHT_SLAB_BYTES // (4 * k))
    assert n_prompt_rows % tm == 0 and m % tm == 0 and n % tn == 0 and tn % V7X_LANES == 0
    npt = n_prompt_rows // tm

    def row_specs(pair, block, col_of):
        if not pair:
            return [pl.BlockSpec(block, lambda j, i: (i, col_of(j)))]
        return [pl.BlockSpec(block, lambda j, i: (jnp.minimum(i, npt - 1), col_of(j))),
                pl.BlockSpec(block, lambda j, i: (jnp.maximum(i - npt, 0), col_of(j)),
                             pipeline_mode=pl.Buffered(1))]

    return pl.pallas_call(
        functools.partial(_res_kernel, n_prompt_tiles=npt, res_pair=res_pair, a_pair=a_pair),
        grid=(n // tn, m // tm),
        in_specs=row_specs(res_pair, (tm, tn), lambda j: j) + row_specs(a_pair, (tm, k), lambda j: 0)
        + [pl.BlockSpec((None, k, tn), lambda j, i: (layer, 0, j), pipeline_mode=pl.Buffered(1))],
        out_specs=pl.BlockSpec((tm, tn), lambda j, i: (i, j)),
        out_shape=jax.ShapeDtypeStruct((m, n), F32),
        scratch_shapes=[pltpu.VMEM((k, tn), BF16)],
        compiler_params=_params("arbitrary", "arbitrary"),
        name="matmul_residual",
    )(*(res if res_pair else (res,)), *(a if a_pair else (a,)), w)


def _ffn_kernel(h_ref, g_ref, wg_ref, wu_ref, wd_ref, out_ref, *rest):
    *cast_refs, hn_sc = rest

    @pl.when(pl.program_id(1) == 0)
    def _():
        h = h_ref[...]
        hn_sc[...] = _rmsnorm(h, g_ref[...]).astype(BF16)
        out_ref[...] = h

    wg, wu, wd = wg_ref[...], wu_ref[...], wd_ref[...]
    if cast_refs:
        wg, wu, wd = wg.astype(BF16), wu.astype(BF16), wd.astype(BF16)
        for ref, w in zip(cast_refs, (wg, wu, wd)):
            ref[...] = w
    hn = hn_sc[...]
    gate = _dot(hn, wg)
    act = (_silu(gate) * _dot(hn, wu)).astype(BF16)
    out_ref[...] += _dot(act, wd)


def _ffn(h, g, w_gate, w_up, w_down, layer, *, n_prompt_rows):
    m, d = h.shape
    f = w_gate.shape[2]
    tm = ROW_TILE
    assert n_prompt_rows % tm == 0 and m - n_prompt_rows == tm
    assert f % COL_TILE == 0 and f % NARROW_COL_TILE == 0
    npt = n_prompt_rows // tm

    def call(row0_tile, n_row_tiles, tf, weights, weight_specs, cast_outputs, name):
        out_specs = [pl.BlockSpec((tm, d), lambda i, j: (i, 0))]
        out_shape = [jax.ShapeDtypeStruct((n_row_tiles * tm, d), F32)]
        if cast_outputs:
            out_specs += [pl.BlockSpec((d, tf), lambda i, j: (0, j)), pl.BlockSpec((d, tf), lambda i, j: (0, j)),
                          pl.BlockSpec((tf, d), lambda i, j: (j, 0))]
            out_shape += [jax.ShapeDtypeStruct((d, f), BF16), jax.ShapeDtypeStruct((d, f), BF16),
                          jax.ShapeDtypeStruct((f, d), BF16)]
        h_mode = dict(pipeline_mode=pl.Buffered(1)) if n_row_tiles == 1 else {}
        return pl.pallas_call(
            _ffn_kernel,
            grid=(n_row_tiles, f // tf),
            in_specs=[pl.BlockSpec((tm, d), lambda i, j: (i + row0_tile, 0), **h_mode),
                      pl.BlockSpec((1, d), lambda i, j: (0, 0))] + weight_specs(tf),
            out_specs=out_specs,
            out_shape=out_shape,
            scratch_shapes=[pltpu.VMEM((tm, d), BF16)],
            compiler_params=_params("arbitrary", "arbitrary"),
            name=name,
        )(h, g, *weights)

    f32_specs = lambda tf: [_layer_spec((d, tf), layer, lambda i, j: (0, j)),
                            _layer_spec((d, tf), layer, lambda i, j: (0, j)),
                            _layer_spec((tf, d), layer, lambda i, j: (j, 0))]
    bf16_specs = lambda tf: [pl.BlockSpec((d, tf), lambda i, j: (0, j)), pl.BlockSpec((d, tf), lambda i, j: (0, j)),
                             pl.BlockSpec((tf, d), lambda i, j: (j, 0))]
    sample, wg_b, wu_b, wd_b = call(npt, 1, NARROW_COL_TILE, (w_gate, w_up, w_down), f32_specs, True,
                                    "swiglu_ffn_sample")
    prompt, = call(0, npt, COL_TILE, (wg_b, wu_b, wd_b), bf16_specs, False, "swiglu_ffn_prompt")
    return prompt, sample


def _ple_kernel(h_ref, p_ref, g_ref, wg_ref, wp_ref, g2_ref, *rest, cast, final):
    if cast:
        *rest, wgb_ref, wpb_ref = rest

        @pl.when(pl.program_id(0) == 0)
        def _():
            wgb_ref[...] = wg_ref[...].astype(BF16)
            wpb_ref[...] = wp_ref[...].astype(BF16)
    else:
        wgb_ref, wpb_ref = wg_ref, wp_ref
    h = h_ref[...]
    gate = jax.nn.sigmoid(_dot(_rmsnorm(h, g_ref[...]).astype(BF16), wgb_ref[...]))
    out = h + _dot(p_ref[...].astype(BF16), wpb_ref[...]) * gate
    normed = _rmsnorm(out, g2_ref[...])
    if final:
        rest[0][...] = normed
    else:
        rest[0][...] = out
        rest[1][...] = normed.astype(BF16)


def _ple(h, pp, ps, g, w_gate, w_proj, layer, g_after, final):
    hp, hs = h
    d = hp.shape[1]
    pdim = pp.shape[2]
    const = lambda i: (0, 0)

    def call(rows, p, tm, weights, weight_specs, cast, name):
        m = rows.shape[0]
        assert m % tm == 0
        row_spec = pl.BlockSpec((tm, d), lambda i: (i, 0))
        out_specs = [row_spec]
        out_shape = [jax.ShapeDtypeStruct((m, d), F32)]
        if not final:
            out_specs.append(row_spec)
            out_shape.append(jax.ShapeDtypeStruct((m, d), BF16))
        if cast:
            out_specs += [pl.BlockSpec((d, d), const), pl.BlockSpec((pdim, d), const)]
            out_shape += [jax.ShapeDtypeStruct((d, d), BF16), jax.ShapeDtypeStruct((pdim, d), BF16)]
        return pl.pallas_call(
            functools.partial(_ple_kernel, cast=cast, final=final),
            grid=(m // tm,),
            in_specs=[row_spec, _layer_spec((tm, pdim), layer, lambda i: (i, 0)),
                      pl.BlockSpec((1, d), const)] + weight_specs + [pl.BlockSpec((1, d), const)],
            out_specs=out_specs,
            out_shape=out_shape,
            compiler_params=_params("arbitrary"),
            name=name,
        )(rows, p, g, *weights, g_after)

    resident = dict(pipeline_mode=pl.Buffered(1))
    f32_specs = [pl.BlockSpec((None, d, d), lambda i: (layer, 0, 0), **resident),
                 pl.BlockSpec((None, pdim, d), lambda i: (layer, 0, 0), **resident)]
    bf16_specs = [pl.BlockSpec((d, d), const, **resident), pl.BlockSpec((pdim, d), const, **resident)]
    tag = "ple_final" if final else "ple"
    *outs_s, wg_b, wp_b = call(hs, ps, PLE_ROW_TILE, (w_gate, w_proj), f32_specs, True, tag + "_sample")
    outs_p = call(hp, pp, HALF_ROW_TILE, (wg_b, wp_b), bf16_specs, False, tag + "_prompt")
    return tuple(zip(outs_p, outs_s))


def _ssd_in_kernel(hn_ref, wt_ref, wdtT_ref, dtb_ref, taps_ref, cb_ref, *rest,
                   sample_len, tiles_per_seq, n_z_tiles):
    if sample_len is not None:
        (buf_ref, zs_ref, xbc_ref, nstate_ref, dt_ref, dtT_ref, wtb_ref, hist_sc, stage_sc) = rest
        i, j = 0, pl.program_id(0)
    else:
        zs_ref, xbc_ref, tail_ref, dt_ref, dtT_ref, carry_sc, hist_sc = rest
        i, j = pl.program_id(0), pl.program_id(1)
    jc = j - n_z_tiles
    tm = hn_ref.shape[0]

    @pl.when(j == 0)
    def _():
        dt = _softplus(_dot_nt(hn_ref[...], wdtT_ref[...]) + dtb_ref[...])
        dt_ref[...] = dt
        dtT_ref[...] = dt.T

    def weight_tile():
        if sample_len is None:
            return wt_ref[...]
        wb = wt_ref[...].astype(BF16)
        wtb_ref[...] = wb
        return wb

    def finish(conv):
        return _silu(conv + cb_ref[...]).astype(BF16)

    @pl.when(j < n_z_tiles)
    def _():
        zs_ref[...] = _silu(_dot_nt(hn_ref[...], weight_tile())).astype(BF16)

    @pl.when(j >= n_z_tiles)
    def _():
        hist_sc[HISTORY_ROWS:, :] = _dot_nt(hn_ref[...], weight_tile())
        taps = taps_ref[...]
        if sample_len is None:
            hist_sc[:HISTORY_ROWS, :] = jnp.where(i % tiles_per_seq == 0, 0.0, carry_sc[jc])
            xbc_ref[...] = finish(_conv(hist_sc, taps))
            carry_sc[jc] = hist_sc[tm:, :]
            tail_ref[...] = hist_sc[HISTORY_ROWS + tm - V7X_SUBLANES:, :]
        else:
            _state_correction(buf_ref, taps, stage_sc, sample_len)
            hist_sc[:HISTORY_ROWS, :] = jnp.zeros((HISTORY_ROWS, hist_sc.shape[1]), F32)
            xbc_ref[...] = finish(_conv(hist_sc, taps, sample_len) + _staged(stage_sc))
            _emit_sample_state(hist_sc[HISTORY_ROWS:, :], stage_sc, nstate_ref, sample_len)


def _ssd_in(h_p, h_s, w_in_t, layer, w_dt_t, dt_b, taps, conv_b, buf, *, seq_len_p, seq_len_s, d_inner):
    n_prompt_rows, d = h_p.shape
    ms = h_s.shape[0]
    _, k, conv_dim = taps.shape
    hp = w_dt_t.shape[0]
    tm, tn = ROW_TILE, COL_TILE
    assert n_prompt_rows % tm == 0 and ms == tm and seq_len_p % tm == 0 and tm % seq_len_s == 0
    assert d_inner % tn == 0 and conv_dim % tn == 0 and k - 1 <= min(seq_len_s, V7X_SUBLANES)
    npt = n_prompt_rows // tm
    n_seq_s = ms // seq_len_s
    nz, nc = d_inner // tn, conv_dim // tn
    zc = lambda j: jnp.minimum(j, nz - 1)
    cc = lambda j: jnp.maximum(j - nz, 0)

    zs_s, xbc_s, nstate, dt_s, dtT_s, w_b = pl.pallas_call(
        functools.partial(_ssd_in_kernel, sample_len=seq_len_s, tiles_per_seq=None, n_z_tiles=nz),
        grid=(nz + nc,),
        in_specs=[
            pl.BlockSpec((tm, d), lambda j: (0, 0)),
            _layer_spec((tn, d), layer, lambda j: (j, 0)),
            pl.BlockSpec((hp, d), lambda j: (0, 0)),
            pl.BlockSpec((1, hp), lambda j: (0, 0)),
            _layer_spec((k, tn), layer, lambda j: (0, cc(j))),
            _layer_spec((1, tn), layer, lambda j: (0, cc(j))),
            _layer_spec((k - 1, n_seq_s, tn), layer, lambda j: (0, 0, cc(j))),
        ],
        out_specs=[
            pl.BlockSpec((tm, tn), lambda j: (0, zc(j))),
            pl.BlockSpec((tm, tn), lambda j: (0, cc(j))),
            pl.BlockSpec((k - 1, n_seq_s, tn), lambda j: (0, 0, cc(j))),
            pl.BlockSpec((tm, hp), lambda j: (0, 0)),
            pl.BlockSpec((hp, tm), lambda j: (0, 0)),
            pl.BlockSpec((tn, d), lambda j: (j, 0)),
        ],
        out_shape=[
            jax.ShapeDtypeStruct((ms, d_inner), BF16),
            jax.ShapeDtypeStruct((ms, conv_dim), BF16),
            jax.ShapeDtypeStruct((k - 1, n_seq_s, conv_dim), F32),
            jax.ShapeDtypeStruct((ms, hp), F32),
            jax.ShapeDtypeStruct((hp, ms), F32),
            jax.ShapeDtypeStruct(((nz + nc) * tn, d), BF16),
        ],
        scratch_shapes=[pltpu.VMEM((HISTORY_ROWS + tm, tn), F32),
                        pltpu.VMEM((tn // V7X_LANES, tm, V7X_LANES), F32)],
        compiler_params=_params("arbitrary"),
        name="ssd_in_sample",
    )(h_s, w_in_t, w_dt_t, dt_b, taps, conv_b, buf)

    tn = WIDE_COL_TILE
    assert d_inner % tn == 0 and conv_dim % tn == 0
    nz, nc = d_inner // tn, conv_dim // tn
    const = lambda i, j: (0, 0)
    zs_p, xbc_p, tail, dt_p, dtT_p = pl.pallas_call(
        functools.partial(_ssd_in_kernel, sample_len=None, tiles_per_seq=seq_len_p // tm, n_z_tiles=nz),
        grid=(npt, nz + nc),
        in_specs=[
            pl.BlockSpec((tm, d), lambda i, j: (i, 0)),
            pl.BlockSpec((tn, d), lambda i, j: (j, 0)),
            pl.BlockSpec((hp, d), const),
            pl.BlockSpec((1, hp), const),
            _layer_spec((k, tn), layer, lambda i, j: (0, cc(j))),
            _layer_spec((1, tn), layer, lambda i, j: (0, cc(j))),
        ],
        out_specs=[
            pl.BlockSpec((tm, tn), lambda i, j: (i, zc(j))),
            pl.BlockSpec((tm, tn), lambda i, j: (i, cc(j))),
            pl.BlockSpec((V7X_SUBLANES, tn), lambda i, j: (i, cc(j))),
            pl.BlockSpec((tm, hp), lambda i, j: (i, 0)),
            pl.BlockSpec((hp, tm), lambda i, j: (0, i)),
        ],
        out_shape=[
            jax.ShapeDtypeStruct((n_prompt_rows, d_inner), BF16),
            jax.ShapeDtypeStruct((n_prompt_rows, conv_dim), BF16),
            jax.ShapeDtypeStruct((npt * V7X_SUBLANES, conv_dim), F32),
            jax.ShapeDtypeStruct((n_prompt_rows, hp), F32),
            jax.ShapeDtypeStruct((hp, n_prompt_rows), F32),
        ],
        scratch_shapes=[pltpu.VMEM((nc, HISTORY_ROWS, tn), F32),
                        pltpu.VMEM((HISTORY_ROWS + tm, tn), F32)],
        compiler_params=_params("arbitrary", "arbitrary"),
        name="ssd_in_prompt",
    )(h_p, w_b, w_dt_t, dt_b, taps, conv_b)
    return ((zs_p, zs_s), (xbc_p, xbc_s), (dt_p, dt_s), (dtT_p, dtT_s)), tail, nstate


def _ssd_group_out(x, zs, cb, acum, acumT, dtT, mask, extra, ng, head0, heads_per_group, head_dim,
                   carried=None):
    heads_per_slab = V7X_LANES // head_dim
    parts = []
    for q in range(heads_per_group // heads_per_slab):
        cols = slice(q * V7X_LANES, (q + 1) * V7X_LANES)
        rhs = x[:, cols]
        if carried is not None:
            cg, st_t = carried
            rhs = jnp.concatenate([rhs, st_t[:, cols].astype(BF16)], axis=0)
        lanes = lax.broadcasted_iota(jnp.int32, rhs.shape, 1)
        acc = None
        for r in range(heads_per_slab):
            hd = head0 + q * heads_per_slab + r
            a_t = jnp.broadcast_to(acum[:, hd:hd + 1], cb.shape)
            seg = a_t - acumT[hd:hd + 1, :]
            lhs = (cb * jnp.exp(jnp.where(mask, seg, MASKED)) * dtT[hd:hd + 1, :]).astype(BF16)
            if carried is not None:
                lhs = jnp.concatenate([lhs, (cg * jnp.exp(a_t)).astype(BF16)], axis=1)
            in_head = jnp.logical_and(lanes >= r * head_dim, lanes < (r + 1) * head_dim)
            part = _dot(lhs, jnp.where(in_head, rhs, jnp.zeros_like(rhs)))
            acc = part if acc is None else acc + part
        parts.append(acc)
    y = jnp.concatenate(parts, axis=1) + extra
    gated = y * zs
    ms = jnp.mean(gated * gated, axis=-1, keepdims=True)
    return (gated * lax.rsqrt(ms + EPS) * ng).astype(BF16)


def _ssd_prompt_body(c, n_chunks, xs_ref, b_ref, c_ref, zs_ref, dt_ref, dtT_ref, alr_ref, alc_ref,
                     e_ref, d_ref, ng_ref, y_ref, state_ref, st_sc, *, n_groups, head_dim, d_state):
    @pl.when(c == 0)
    def _():
        st_sc[...] = jnp.zeros_like(st_sc)

    q_rows, d_inner = xs_ref.shape
    gw = d_inner // n_groups
    hpg = gw // head_dim
    row = lax.broadcasted_iota(jnp.int32, (q_rows, q_rows), 0)
    col = lax.broadcasted_iota(jnp.int32, (q_rows, q_rows), 1)
    causal = col <= row
    tril = jnp.where(causal, 1.0, 0.0).astype(BF16)
    triu = jnp.where(row <= col, 1.0, 0.0).astype(BF16)
    dt = dt_ref[...]
    dtT = dtT_ref[...]
    acum = _dot01_lhs(tril, dt * -jnp.exp(alr_ref[...]))
    acumT = _dot01_rhs(dtT * -jnp.exp(alc_ref[...]), triu)
    a_end = acum[q_rows - 1:q_rows, :]
    e = e_ref[...]
    x = xs_ref[...]
    xf = x.astype(F32)
    to_end = (xf * _dot((dt * jnp.exp(a_end - acum)).astype(BF16), e)).astype(BF16)
    decay = _dot01_rhs(jnp.broadcast_to(jnp.exp(a_end), (V7X_SUBLANES, a_end.shape[1])), e)[:1, :]
    skip = xf * d_ref[...]
    for g in range(n_groups):
        sl = slice(g * gw, (g + 1) * gw)
        ns = slice(g * d_state, (g + 1) * d_state)
        bg, cg = b_ref[:, ns], c_ref[:, ns]
        st = st_sc[:, sl]
        y_ref[:, sl] = _ssd_group_out(x[:, sl], zs_ref[:, sl].astype(F32), _dot_nt(cg, bg), acum,
                                      acumT, dtT, causal, skip[:, sl], ng_ref[:, sl],
                                      g * hpg, hpg, head_dim,
                                      carried=(cg.astype(F32), st))
        st_sc[:, sl] = decay[:, sl] * st + _dot_tn(bg, to_end[:, sl])

    @pl.when(c == n_chunks - 1)
    def _():
        for g in range(n_groups):
            state_ref[g * gw:(g + 1) * gw, :] = st_sc[:, g * gw:(g + 1) * gw].T


def _ssd_sample_body(xs_ref, b_ref, c_ref, zs_ref, dt_ref, dtT_ref, alr_ref, alc_ref, e_ref,
                     d_ref, ng_ref, st_ref, y_ref, nst_ref, *, seq_len, head_dim):
    q_rows, gw = xs_ref.shape
    n_seq = q_rows // seq_len
    row = lax.broadcasted_iota(jnp.int32, (q_rows, q_rows), 0)
    col = lax.broadcasted_iota(jnp.int32, (q_rows, q_rows), 1)
    same = (row // seq_len) == (col // seq_len)
    mask = jnp.logical_and(same, col <= row)
    tril = jnp.where(mask, 1.0, 0.0).astype(BF16)
    triu = jnp.where(jnp.logical_and(same, row <= col), 1.0, 0.0).astype(BF16)
    ends = jnp.where(col == (row // seq_len) * seq_len + (seq_len - 1), 1.0, 0.0).astype(BF16)
    dt = dt_ref[...]
    dtT = dtT_ref[...]
    acum = _dot01_lhs(tril, dt * -jnp.exp(alr_ref[...]))
    acumT = _dot01_rhs(dtT * -jnp.exp(alc_ref[...]), triu)
    a_end = _dot01_lhs(ends, acum)
    e = e_ref[...]
    x = xs_ref[...].astype(F32)
    to_endT = (x * _dot((dt * jnp.exp(a_end - acum)).astype(BF16), e)).T.astype(BF16)
    decayT = _dot01_rhs(jnp.exp(a_end), e).T
    from_start = _spread(jnp.exp(acum), e)
    bg = b_ref[...].astype(F32)
    cg = c_ref[...].astype(F32)
    seq_of_row = lax.broadcasted_iota(jnp.int32, bg.shape, 0) // seq_len
    inter = jnp.zeros((q_rows, gw), F32)
    for s in range(n_seq):
        st = st_ref[s]
        mine = seq_of_row == s
        inter = inter + _dot_nt(jnp.where(mine, cg, 0.0).astype(BF16), st.astype(BF16))
        bm = jnp.where(mine, bg, 0.0).astype(BF16)
        nst_ref[s] = decayT[:, s * seq_len:s * seq_len + 1] * st + _dot(to_endT, bm)
    inter = from_start * inter + x * d_ref[...]
    y_ref[...] = _ssd_group_out(xs_ref[...], zs_ref[...].astype(F32), _dot_nt(c_ref[...], b_ref[...]),
                                acum, acumT, dtT, mask, inter, ng_ref[...], 0, gw // head_dim, head_dim)


N_PROMPT_SCAN_INPUTS = 11
N_SAMPLE_SCAN_INPUTS = 12


def _ssd_scan_kernel(*refs, n_chunks, n_groups, seq_len_s, head_dim, d_state):
    a, b = N_PROMPT_SCAN_INPUTS, N_PROMPT_SCAN_INPUTS + N_SAMPLE_SCAN_INPUTS
    prompt_in, sample_in = refs[:a], refs[a:b]
    y_p, state_p, y_s, state_s, st_sc = refs[b:]
    _ssd_prompt_body(pl.program_id(0) % n_chunks, n_chunks, *prompt_in, y_p, state_p, st_sc,
                     n_groups=n_groups, head_dim=head_dim, d_state=d_state)
    _ssd_sample_body(*sample_in, y_s, state_s, seq_len=seq_len_s, head_dim=head_dim)


def _ssd_scan(xbc_p, zs_p, dt, dtT, alog_row, alog_col, expand, d_x, ng,
              xbc_s, zs_s, dt_g, dtT_g, alog_row_g, alog_col_g, state,
              *, n_seq_p, seq_len_p, seq_len_s, d_inner, n_groups, head_dim, d_state):
    q = SSD_CHUNK
    n_rows_s = xbc_s.shape[0]
    assert seq_len_p % q == 0 and q == d_state
    assert n_rows_s % q == 0 and q % seq_len_s == 0
    nc = seq_len_p // q
    nb = n_rows_s // q
    assert n_seq_p * nc == nb * n_groups
    spb = q // seq_len_s
    gw = d_inner // n_groups
    hp = dt.shape[1]
    hpg_rows = dtT_g.shape[1]
    gn = n_groups * d_state
    assert d_inner % gn == 0
    b_col0 = d_inner // d_state
    const = lambda t: (0, 0)
    sb = lambda t: t // n_groups
    g = lambda t: t % n_groups
    prompt_specs = [
        pl.BlockSpec((q, d_inner), lambda t: (t, 0)),
        pl.BlockSpec((q, gn), lambda t: (t, d_inner // gn)),
        pl.BlockSpec((q, gn), lambda t: (t, d_inner // gn + 1)),
        pl.BlockSpec((q, d_inner), lambda t: (t, 0)),
        pl.BlockSpec((q, hp), lambda t: (t, 0)),
        pl.BlockSpec((hp, q), lambda t: (0, t)),
        pl.BlockSpec((1, hp), const),
        pl.BlockSpec((hp, 1), const),
        pl.BlockSpec((hp, d_inner), const),
        pl.BlockSpec((1, d_inner), const),
        pl.BlockSpec((1, d_inner), const),
    ]
    sample_specs = [
        pl.BlockSpec((q, gw), lambda t: (sb(t), g(t))),
        pl.BlockSpec((q, d_state), lambda t: (sb(t), b_col0 + g(t))),
        pl.BlockSpec((q, d_state), lambda t: (sb(t), b_col0 + n_groups + g(t))),
        pl.BlockSpec((q, gw), lambda t: (sb(t), g(t))),
        pl.BlockSpec((None, q, hp), lambda t: (g(t), sb(t), 0)),
        pl.BlockSpec((None, hpg_rows, q), lambda t: (g(t), 0, sb(t))),
        pl.BlockSpec((None, 1, hp), lambda t: (g(t), 0, 0)),
        pl.BlockSpec((None, hpg_rows, 1), lambda t: (g(t), 0, 0)),
        pl.BlockSpec((hp, gw), const),
        pl.BlockSpec((1, gw), lambda t: (0, g(t))),
        pl.BlockSpec((1, gw), lambda t: (0, g(t))),
        pl.BlockSpec((spb, None, gw, d_state), lambda t: (sb(t), g(t), 0, 0)),
    ]
    assert len(prompt_specs) == N_PROMPT_SCAN_INPUTS and len(sample_specs) == N_SAMPLE_SCAN_INPUTS
    kern = functools.partial(_ssd_scan_kernel, n_chunks=nc, n_groups=n_groups, seq_len_s=seq_len_s,
                             head_dim=head_dim, d_state=d_state)
    return pl.pallas_call(
        kern,
        grid=(n_seq_p * nc,),
        in_specs=prompt_specs + sample_specs,
        out_specs=[
            pl.BlockSpec((q, d_inner), lambda t: (t, 0)),
            pl.BlockSpec((d_inner, d_state), lambda t: (t // nc, 0)),
            pl.BlockSpec((q, gw), lambda t: (sb(t), g(t))),
            pl.BlockSpec((spb, None, gw, d_state), lambda t: (sb(t), g(t), 0, 0)),
        ],
        out_shape=[
            jax.ShapeDtypeStruct((n_seq_p * seq_len_p, d_inner), BF16),
            jax.ShapeDtypeStruct((n_seq_p * d_inner, d_state), F32),
            jax.ShapeDtypeStruct((n_rows_s, d_inner), BF16),
            jax.ShapeDtypeStruct(state.shape, F32),
        ],
        scratch_shapes=[pltpu.VMEM((d_state, d_inner), F32)],
        compiler_params=_params("arbitrary"),
        name="ssd_scan",
    )(xbc_p, xbc_p, xbc_p, zs_p, dt, dtT, alog_row, alog_col, expand, d_x, ng,
      xbc_s, xbc_s, xbc_s, zs_s, dt_g, dtT_g, alog_row_g, alog_col_g, expand[:, :gw], d_x, ng, state)


def _prompt_conv_state(tail, *, n_prompt_tiles, tiles_per_seq, km1):
    t = tail.reshape(-1, V7X_SUBLANES, tail.shape[1])[:n_prompt_tiles]
    return t[tiles_per_seq - 1::tiles_per_seq, V7X_SUBLANES - km1:, :]


def kernel(x_prompt, x_sample, p_prompt, p_sample, state_sc_conv, state_ssd_conv, state_ssd, g_mix, g_ffn, g_ple, g_final, sc_w_in, sc_w_conv, sc_w_out, ssd_w_in, ssd_conv_w, ssd_conv_b, ssd_dt_bias, ssd_a_log, ssd_d, ssd_norm_g, ssd_w_out, ffn_w_gate, ffn_w_up, ffn_w_down, ple_w_proj, ple_w_gate):
    bp, lp, d = x_prompt.shape
    bs, ls, _ = x_sample.shape
    depth = g_mix.shape[0]
    mp, ms = bp * lp, bs * ls
    pdim = p_prompt.shape[-1]
    n_heads, head_dim, d_state = state_ssd.shape[2:]
    d_inner = n_heads * head_dim
    conv_dim = ssd_conv_w.shape[-1]
    n_groups = (conv_dim - d_inner) // (2 * d_state)
    hpg = n_heads // n_groups
    assert n_heads <= V7X_LANES and V7X_LANES % head_dim == 0 and d_state == V7X_LANES
    npt = mp // ROW_TILE
    tps = lp // ROW_TILE
    row = lambda v: v.reshape(1, -1)
    pp = p_prompt.reshape(depth, mp, pdim)
    ps = p_sample.reshape(depth, ms, pdim)

    h = (x_prompt.reshape(mp, d), x_sample.reshape(ms, d))
    hn = None
    sc_p, sc_s, cv_p, cv_s, st_p, st_s = [], [], [], [], [], []
    for i in range(depth):
        j = i // 2
        if i % 2 == 0:
            km1 = sc_w_conv.shape[1] - 1
            gated, tail, nstate = _short_conv_in(
                *h, row(g_mix[i]), sc_w_in, j, sc_w_conv, jnp.swapaxes(state_sc_conv, 1, 2),
                seq_len_p=lp, seq_len_s=ls)
            sc_p.append(_prompt_conv_state(tail, n_prompt_tiles=npt, tiles_per_seq=tps, km1=km1))
            sc_s.append(jnp.swapaxes(nstate, 0, 1))
            h = _matmul_residual(h, gated, sc_w_out, j, n_prompt_rows=mp)
        else:
            km1 = ssd_conv_w.shape[1] - 1
            zx = d_inner + conv_dim
            pad_h = V7X_LANES - n_heads
            w_in_t = jnp.swapaxes(ssd_w_in, 1, 2)
            w_dt_t = jnp.pad(w_in_t[j, zx:, :], ((0, pad_h), (0, 0))).astype(BF16)
            dt_b = jnp.pad(ssd_dt_bias[j], (0, pad_h))
            alog = jnp.pad(ssd_a_log[j], (0, pad_h))
            assert hn is not None
            (zs, xbc, dt, dtT), tail, nstate = _ssd_in(
                *hn, w_in_t, j, w_dt_t, row(dt_b),
                ssd_conv_w, ssd_conv_b.reshape(ssd_conv_b.shape[0], 1, conv_dim),
                jnp.swapaxes(state_ssd_conv, 1, 2),
                seq_len_p=lp, seq_len_s=ls, d_inner=d_inner)
            cv_p.append(_prompt_conv_state(tail, n_prompt_tiles=npt, tiles_per_seq=tps, km1=km1))
            cv_s.append(jnp.swapaxes(nstate, 0, 1))
            head_of_lane = jnp.arange(d_inner, dtype=jnp.int32) // head_dim
            expand = (jnp.arange(V7X_LANES, dtype=jnp.int32)[:, None] == head_of_lane[None, :]).astype(BF16)
            d_x = row(jnp.repeat(ssd_d[j], head_dim))
            ng = row(ssd_norm_g[j])
            dt_g = jnp.stack([jnp.roll(dt[1], -g * hpg, axis=1) for g in range(n_groups)])
            alog_g = jnp.stack([jnp.roll(alog, -g * hpg) for g in range(n_groups)])
            dtT_g = dtT[1][:n_heads].reshape(n_groups, hpg, ms)
            y_p, new_p, y_s, new_s = _ssd_scan(
                xbc[0], zs[0], dt[0], dtT[0], row(alog), alog.reshape(-1, 1), expand, d_x, ng,
                xbc[1], zs[1], dt_g, dtT_g, alog_g.reshape(n_groups, 1, -1),
                ssd_a_log[j].reshape(n_groups, hpg, 1),
                state_ssd[j].reshape(bs, n_groups, hpg * head_dim, d_state),
                n_seq_p=bp, seq_len_p=lp, seq_len_s=ls,
                d_inner=d_inner, n_groups=n_groups, head_dim=head_dim, d_state=d_state)
            st_p.append(new_p.reshape(bp, n_heads, head_dim, d_state))
            st_s.append(new_s.reshape(bs, n_heads, head_dim, d_state))
            h = _matmul_residual(h, (y_p, y_s), ssd_w_out, j, n_prompt_rows=mp)
        h = _ffn(h, row(g_ffn[i]), ffn_w_gate, ffn_w_up, ffn_w_down, i, n_prompt_rows=mp)
        if i == depth - 1:
            (h,) = _ple(h, pp, ps, row(g_ple[i]), ple_w_gate, ple_w_proj, i, row(g_final), True)
        else:
            h, hn = _ple(h, pp, ps, row(g_ple[i]), ple_w_gate, ple_w_proj, i, row(g_mix[i + 1]), False)
    y_p, y_s = h
    return (y_p.reshape(bp, lp, d), y_s.reshape(bs, ls, d), jnp.stack(sc_p), jnp.stack(sc_s),
            jnp.stack(cv_p), jnp.stack(cv_s), jnp.stack(st_p), jnp.stack(st_s))
```

```python
import functools

import jax
import jax.numpy as jnp
from jax import lax
from jax.experimental import pallas as pl
from jax.experimental.pallas import tpu as pltpu

F32 = jnp.float32
BF16 = jnp.bfloat16
EPS = 1e-6
MASKED = -1e30
V7X_LANES = 128
V7X_SUBLANES = 8
HISTORY_ROWS = 16
V7X_VMEM_LIMIT = 56 * 1024 * 1024

ROW_TILE = 1024
HALF_ROW_TILE = 512
PLE_ROW_TILE = 256
COL_TILE = 512
WIDE_COL_TILE = 2048
NARROW_COL_TILE = 256
SSD_CHUNK = 128
WEIGHT_SLAB_BYTES = 16 * 1024 * 1024


def _params(*sem):
    return pltpu.CompilerParams(dimension_semantics=sem, vmem_limit_bytes=V7X_VMEM_LIMIT)


def _dot(a, b):
    return jnp.dot(a, b, preferred_element_type=F32)


def _dot_nt(a, b):
    return lax.dot_general(a, b, (((1,), (1,)), ((), ())), preferred_element_type=F32)


def _dot_tn(a, b):
    return lax.dot_general(a, b, (((0,), (0,)), ((), ())), preferred_element_type=F32)


def _split3(a):
    a1 = a.astype(BF16)
    r1 = a - a1.astype(F32)
    a2 = r1.astype(BF16)
    a3 = (r1 - a2.astype(F32)).astype(BF16)
    return a3, a2, a1


def _dot01_rhs(a, e):
    p3, p2, p1 = _split3(a)
    return (_dot(p3, e) + _dot(p2, e)) + _dot(p1, e)


def _spread(a, e):
    hi = a.astype(BF16)
    lo = (a - hi.astype(F32)).astype(BF16)
    return _dot(lo, e) + _dot(hi, e)


def _dot01_lhs(t, a):
    p3, p2, p1 = _split3(a)
    return (_dot(t, p3) + _dot(t, p2)) + _dot(t, p1)


def _rmsnorm(x, g):
    ms = jnp.mean(x * x, axis=-1, keepdims=True)
    return x * lax.rsqrt(ms + EPS) * g


def _softplus(x):
    return jnp.maximum(x, 0.0) + jnp.log1p(jnp.exp(-jnp.abs(x)))


def _silu(x):
    half = 0.5 * x
    return half + half * jnp.tanh(half)


def _layer_spec(block, layer, imap):
    return pl.BlockSpec((None,) + tuple(block), lambda *a: (layer,) + tuple(imap(*a)))


def _conv(hist_ref, taps, seq_len=None, row0=0, n_rows=None):
    k = taps.shape[0]
    if n_rows is None:
        n_rows = hist_ref.shape[0] - HISTORY_ROWS
    out = taps[k - 1:k, :] * hist_ref[pl.ds(HISTORY_ROWS + row0, n_rows), :]
    if seq_len is not None:
        assert row0 % seq_len == 0
        t = lax.broadcasted_iota(jnp.int32, out.shape, 0) % seq_len
    for d in range(1, k):
        sh = hist_ref[pl.ds(HISTORY_ROWS + row0 - d, n_rows), :]
        if seq_len is not None:
            sh = jnp.where(t >= d, sh, 0.0)
        out = out + taps[k - 1 - d:k - d, :] * sh
    return out


def _state_correction(buf_ref, taps, stage_sc, seq_len):
    k = taps.shape[0]
    km1 = k - 1
    n_seq = stage_sc.shape[1] // seq_len
    stage_sc[...] = jnp.zeros_like(stage_sc)
    rows = [buf_ref[r] for r in range(km1)]
    for t in range(km1):
        acc = None
        for d in range(t + 1, k):
            term = taps[k - 1 - d:k - d, :] * rows[km1 + t - d]
            acc = term if acc is None else acc + term
        for c in range(stage_sc.shape[0]):
            stage_sc[c, pl.ds(t, n_seq, stride=seq_len), :] = acc[:, c * V7X_LANES:(c + 1) * V7X_LANES]


def _staged(stage_sc):
    return jnp.concatenate([stage_sc[c] for c in range(stage_sc.shape[0])], axis=1)


def _emit_sample_state(u, stage_sc, nstate_ref, seq_len):
    km1 = nstate_ref.shape[0]
    n_chunks = stage_sc.shape[0]
    n_seq = stage_sc.shape[1] // seq_len
    for c in range(n_chunks):
        stage_sc[c] = u[:, c * V7X_LANES:(c + 1) * V7X_LANES]
    for r in range(km1):
        nstate_ref[r] = jnp.concatenate(
            [stage_sc[c, pl.ds(seq_len - km1 + r, n_seq, stride=seq_len), :] for c in range(n_chunks)],
            axis=1)


def _sc_in_sample_kernel(x_ref, g_ref, wb_ref, wc_ref, wv_ref, taps_ref, buf_ref,
                         gated_ref, nstate_ref, wbb_ref, wcb_ref, wvb_ref, hn_sc, hist_sc, stage_sc,
                         *, sample_len):
    @pl.when(pl.program_id(0) == 0)
    def _():
        hn_sc[...] = _rmsnorm(x_ref[...], g_ref[...]).astype(BF16)

    wb, wc, wv = wb_ref[...].astype(BF16), wc_ref[...].astype(BF16), wv_ref[...].astype(BF16)
    wbb_ref[...], wcb_ref[...], wvb_ref[...] = wb, wc, wv
    taps = taps_ref[...]
    _state_correction(buf_ref, taps, stage_sc, sample_len)
    hn = hn_sc[...]
    bg = _dot(hn, wb)
    u = _dot(hn, wc) * _dot(hn, wv)
    hist_sc[:HISTORY_ROWS, :] = jnp.zeros((HISTORY_ROWS, u.shape[1]), F32)
    hist_sc[HISTORY_ROWS:, :] = u
    gated_ref[...] = (bg * (_conv(hist_sc, taps, sample_len) + _staged(stage_sc))).astype(BF16)
    _emit_sample_state(u, stage_sc, nstate_ref, sample_len)


def _sc_in_prompt_kernel(x_ref, g_ref, wb_ref, wc_ref, wv_ref, taps_ref,
                         gated_ref, tail_ref, hn_sc, carry_sc, hist_sc, *, tiles_per_seq):
    i = pl.program_id(0)
    j = pl.program_id(1)

    @pl.when(j == 0)
    def _():
        hn_sc[...] = _rmsnorm(x_ref[...], g_ref[...]).astype(BF16)

    hn = hn_sc[...]
    bg = _dot(hn, wb_ref[...])
    u = _dot(hn, wc_ref[...]) * _dot(hn, wv_ref[...])
    hist_sc[:HISTORY_ROWS, :] = jnp.where(i % tiles_per_seq == 0, 0.0, carry_sc[j])
    hist_sc[HISTORY_ROWS:, :] = u
    gated_ref[...] = (bg * _conv(hist_sc, taps_ref[...])).astype(BF16)
    carry_sc[j] = u[u.shape[0] - HISTORY_ROWS:, :]
    tail_ref[...] = u[u.shape[0] - V7X_SUBLANES:, :]


def _short_conv_in(xp, xs, g, w_in, layer, taps, buf, *, seq_len_p, seq_len_s):
    mp, d = xp.shape
    ms = xs.shape[0]
    k = taps.shape[1]
    tm = ROW_TILE
    assert mp % tm == 0 and ms == tm and seq_len_p % tm == 0 and tm % seq_len_s == 0
    assert k - 1 <= min(seq_len_s, V7X_SUBLANES)
    n_seq_s = ms // seq_len_s

    tn = NARROW_COL_TILE
    nj = d // tn
    assert d % tn == 0 and tn % V7X_LANES == 0
    gated_s, nstate, wb, wc, wv = pl.pallas_call(
        functools.partial(_sc_in_sample_kernel, sample_len=seq_len_s),
        grid=(nj,),
        in_specs=[
            pl.BlockSpec((tm, d), lambda j: (0, 0)),
            pl.BlockSpec((1, d), lambda j: (0, 0)),
            _layer_spec((d, tn), layer, lambda j: (0, j)),
            _layer_spec((d, tn), layer, lambda j: (0, nj + j)),
            _layer_spec((d, tn), layer, lambda j: (0, 2 * nj + j)),
            _layer_spec((k, tn), layer, lambda j: (0, j)),
            _layer_spec((k - 1, n_seq_s, tn), layer, lambda j: (0, 0, j)),
        ],
        out_specs=[pl.BlockSpec((tm, tn), lambda j: (0, j)),
                   pl.BlockSpec((k - 1, n_seq_s, tn), lambda j: (0, 0, j))]
        + [pl.BlockSpec((d, tn), lambda j: (0, j))] * 3,
        out_shape=[jax.ShapeDtypeStruct((ms, d), BF16), jax.ShapeDtypeStruct((k - 1, n_seq_s, d), F32)]
        + [jax.ShapeDtypeStruct((d, d), BF16)] * 3,
        scratch_shapes=[pltpu.VMEM((tm, d), BF16), pltpu.VMEM((HISTORY_ROWS + tm, tn), F32),
                        pltpu.VMEM((tn // V7X_LANES, tm, V7X_LANES), F32)],
        compiler_params=_params("arbitrary"),
        name="short_conv_in_sample",
    )(xs, g, w_in, w_in, w_in, taps, buf)

    tn = COL_TILE
    nj = d // tn
    assert d % tn == 0
    npt = mp // tm
    gated_p, tail = pl.pallas_call(
        functools.partial(_sc_in_prompt_kernel, tiles_per_seq=seq_len_p // tm),
        grid=(npt, nj),
        in_specs=[
            pl.BlockSpec((tm, d), lambda i, j: (i, 0)),
            pl.BlockSpec((1, d), lambda i, j: (0, 0)),
            pl.BlockSpec((d, tn), lambda i, j: (0, j)),
            pl.BlockSpec((d, tn), lambda i, j: (0, j)),
            pl.BlockSpec((d, tn), lambda i, j: (0, j)),
            _layer_spec((k, tn), layer, lambda i, j: (0, j)),
        ],
        out_specs=[pl.BlockSpec((tm, tn), lambda i, j: (i, j)),
                   pl.BlockSpec((V7X_SUBLANES, tn), lambda i, j: (i, j))],
        out_shape=[jax.ShapeDtypeStruct((mp, d), BF16),
                   jax.ShapeDtypeStruct((npt * V7X_SUBLANES, d), F32)],
        scratch_shapes=[pltpu.VMEM((tm, d), BF16), pltpu.VMEM((nj, HISTORY_ROWS, tn), F32),
                        pltpu.VMEM((HISTORY_ROWS + tm, tn), F32)],
        compiler_params=_params("arbitrary", "arbitrary"),
        name="short_conv_in_prompt",
    )(xp, g, wb, wc, wv, taps)
    return (gated_p, gated_s), tail, nstate


def _res_kernel(*refs, n_prompt_tiles, res_pair, a_pair):
    refs = list(refs)
    res_refs = [refs.pop(0) for _ in range(2 if res_pair else 1)]
    a_refs = [refs.pop(0) for _ in range(2 if a_pair else 1)]
    w_ref, out_ref, wb_sc = refs
    i = pl.program_id(1)

    @pl.when(i == 0)
    def _():
        wb_sc[...] = w_ref[...].astype(BF16)

    def body(which):
        out_ref[...] = res_refs[which * res_pair][...] + _dot(a_refs[which * a_pair][...], wb_sc[...])

    if not (res_pair or a_pair):
        body(0)
        return
    pl.when(i < n_prompt_tiles)(lambda: body(0))
    pl.when(i >= n_prompt_tiles)(lambda: body(1))


def _matmul_residual(res, a, w, layer, *, n_prompt_rows):
    res_pair, a_pair = isinstance(res, tuple), isinstance(a, tuple)
    _, k, n = w.shape
    m = sum(r.shape[0] for r in res) if res_pair else res.shape[0]
    tm = HALF_ROW_TILE
    tn = min(n, WEIGHT_SLAB_BYTES // (4 * k))
    assert n_prompt_rows % tm == 0 and m % tm == 0 and n % tn == 0 and tn % V7X_LANES == 0
    npt = n_prompt_rows // tm

    def row_specs(pair, block, col_of):
        if not pair:
            return [pl.BlockSpec(block, lambda j, i: (i, col_of(j)))]
        return [pl.BlockSpec(block, lambda j, i: (jnp.minimum(i, npt - 1), col_of(j))),
                pl.BlockSpec(block, lambda j, i: (jnp.maximum(i - npt, 0), col_of(j)),
                             pipeline_mode=pl.Buffered(1))]

    return pl.pallas_call(
        functools.partial(_res_kernel, n_prompt_tiles=npt, res_pair=res_pair, a_pair=a_pair),
        grid=(n // tn, m // tm),
        in_specs=row_specs(res_pair, (tm, tn), lambda j: j) + row_specs(a_pair, (tm, k), lambda j: 0)
        + [pl.BlockSpec((None, k, tn), lambda j, i: (layer, 0, j), pipeline_mode=pl.Buffered(1))],
        out_specs=pl.BlockSpec((tm, tn), lambda j, i: (i, j)),
        out_shape=jax.ShapeDtypeStruct((m, n), F32),
        scratch_shapes=[pltpu.VMEM((k, tn), BF16)],
        compiler_params=_params("arbitrary", "arbitrary"),
        name="matmul_residual",
    )(*(res if res_pair else (res,)), *(a if a_pair else (a,)), w)


def _ffn_kernel(h_ref, g_ref, wg_ref, wu_ref, wd_ref, out_ref, *rest):
    *cast_refs, hn_sc = rest

    @pl.when(pl.program_id(1) == 0)
    def _():
        h = h_ref[...]
        hn_sc[...] = _rmsnorm(h, g_ref[...]).astype(BF16)
        out_ref[...] = h

    wg, wu, wd = wg_ref[...], wu_ref[...], wd_ref[...]
    if cast_refs:
        wg, wu, wd = wg.astype(BF16), wu.astype(BF16), wd.astype(BF16)
        for ref, w in zip(cast_refs, (wg, wu, wd)):
            ref[...] = w
    hn = hn_sc[...]
    gate = _dot(hn, wg)
    act = (_silu(gate) * _dot(hn, wu)).astype(BF16)
    out_ref[...] += _dot(act, wd)


def _ffn(h, g, w_gate, w_up, w_down, layer, *, n_prompt_rows):
    m, d = h.shape
    f = w_gate.shape[2]
    tm = ROW_TILE
    assert n_prompt_rows % tm == 0 and m - n_prompt_rows == tm
    assert f % COL_TILE == 0 and f % NARROW_COL_TILE == 0
    npt = n_prompt_rows // tm

    def call(row0_tile, n_row_tiles, tf, weights, weight_specs, cast_outputs, name):
        out_specs = [pl.BlockSpec((tm, d), lambda i, j: (i, 0))]
        out_shape = [jax.ShapeDtypeStruct((n_row_tiles * tm, d), F32)]
        if cast_outputs:
            out_specs += [pl.BlockSpec((d, tf), lambda i, j: (0, j)), pl.BlockSpec((d, tf), lambda i, j: (0, j)),
                          pl.BlockSpec((tf, d), lambda i, j: (j, 0))]
            out_shape += [jax.ShapeDtypeStruct((d, f), BF16), jax.ShapeDtypeStruct((d, f), BF16),
                          jax.ShapeDtypeStruct((f, d), BF16)]
        h_mode = dict(pipeline_mode=pl.Buffered(1)) if n_row_tiles == 1 else {}
        return pl.pallas_call(
            _ffn_kernel,
            grid=(n_row_tiles, f // tf),
            in_specs=[pl.BlockSpec((tm, d), lambda i, j: (i + row0_tile, 0), **h_mode),
                      pl.BlockSpec((1, d), lambda i, j: (0, 0))] + weight_specs(tf),
            out_specs=out_specs,
            out_shape=out_shape,
            scratch_shapes=[pltpu.VMEM((tm, d), BF16)],
            compiler_params=_params("arbitrary", "arbitrary"),
            name=name,
        )(h, g, *weights)

    f32_specs = lambda tf: [_layer_spec((d, tf), layer, lambda i, j: (0, j)),
                            _layer_spec((d, tf), layer, lambda i, j: (0, j)),
                            _layer_spec((tf, d), layer, lambda i, j: (j, 0))]
    bf16_specs = lambda tf: [pl.BlockSpec((d, tf), lambda i, j: (0, j)), pl.BlockSpec((d, tf), lambda i, j: (0, j)),
                             pl.BlockSpec((tf, d), lambda i, j: (j, 0))]
    sample, wg_b, wu_b, wd_b = call(npt, 1, NARROW_COL_TILE, (w_gate, w_up, w_down), f32_specs, True,
                                    "swiglu_ffn_sample")
    prompt, = call(0, npt, COL_TILE, (wg_b, wu_b, wd_b), bf16_specs, False, "swiglu_ffn_prompt")
    return prompt, sample


def _ple_kernel(h_ref, p_ref, g_ref, wg_ref, wp_ref, g2_ref, *rest, cast, final):
    if cast:
        *rest, wgb_ref, wpb_ref = rest

        @pl.when(pl.program_id(0) == 0)
        def _():
            wgb_ref[...] = wg_ref[...].astype(BF16)
            wpb_ref[...] = wp_ref[...].astype(BF16)
    else:
        wgb_ref, wpb_ref = wg_ref, wp_ref
    h = h_ref[...]
    gate = jax.nn.sigmoid(_dot(_rmsnorm(h, g_ref[...]).astype(BF16), wgb_ref[...]))
    out = h + _dot(p_ref[...].astype(BF16), wpb_ref[...]) * gate
    normed = _rmsnorm(out, g2_ref[...])
    if final:
        rest[0][...] = normed
    else:
        rest[0][...] = out
        rest[1][...] = normed.astype(BF16)


def _ple(h, pp, ps, g, w_gate, w_proj, layer, g_after, final):
    hp, hs = h
    d = hp.shape[1]
    pdim = pp.shape[2]
    const = lambda i: (0, 0)

    def call(rows, p, tm, weights, weight_specs, cast, name):
        m = rows.shape[0]
        assert m % tm == 0
        row_spec = pl.BlockSpec((tm, d), lambda i: (i, 0))
        out_specs = [row_spec]
        out_shape = [jax.ShapeDtypeStruct((m, d), F32)]
        if not final:
            out_specs.append(row_spec)
            out_shape.append(jax.ShapeDtypeStruct((m, d), BF16))
        if cast:
            out_specs += [pl.BlockSpec((d, d), const), pl.BlockSpec((pdim, d), const)]
            out_shape += [jax.ShapeDtypeStruct((d, d), BF16), jax.ShapeDtypeStruct((pdim, d), BF16)]
        return pl.pallas_call(
            functools.partial(_ple_kernel, cast=cast, final=final),
            grid=(m // tm,),
            in_specs=[row_spec, _layer_spec((tm, pdim), layer, lambda i: (i, 0)),
                      pl.BlockSpec((1, d), const)] + weight_specs + [pl.BlockSpec((1, d), const)],
            out_specs=out_specs,
            out_shape=out_shape,
            compiler_params=_params("arbitrary"),
            name=name,
        )(rows, p, g, *weights, g_after)

    resident = dict(pipeline_mode=pl.Buffered(1))
    f32_specs = [pl.BlockSpec((None, d, d), lambda i: (layer, 0, 0), **resident),
                 pl.BlockSpec((None, pdim, d), lambda i: (layer, 0, 0), **resident)]
    bf16_specs = [pl.BlockSpec((d, d), const, **resident), pl.BlockSpec((pdim, d), const, **resident)]
    tag = "ple_final" if final else "ple"
    *outs_s, wg_b, wp_b = call(hs, ps, PLE_ROW_TILE, (w_gate, w_proj), f32_specs, True, tag + "_sample")
    outs_p = call(hp, pp, HALF_ROW_TILE, (wg_b, wp_b), bf16_specs, False, tag + "_prompt")
    return tuple(zip(outs_p, outs_s))


def _ssd_in_kernel(hn_ref, wt_ref, wdtT_ref, dtb_ref, taps_ref, cb_ref, *rest,
                   sample_len, tiles_per_seq, n_z_tiles):
    if sample_len is not None:
        (buf_ref, zs_ref, xbc_ref, nstate_ref, dt_ref, dtT_ref, wtb_ref, hist_sc, stage_sc) = rest
        i, j = 0, pl.program_id(0)
    else:
        zs_ref, xbc_ref, tail_ref, dt_ref, dtT_ref, carry_sc, hist_sc = rest
        i, j = pl.program_id(0), pl.program_id(1)
    jc = j - n_z_tiles
    tm = hn_ref.shape[0]

    @pl.when(j == 0)
    def _():
        dt = _softplus(_dot_nt(hn_ref[...], wdtT_ref[...]) + dtb_ref[...])
        dt_ref[...] = dt
        dtT_ref[...] = dt.T

    def weight_tile():
        if sample_len is None:
            return wt_ref[...]
        wb = wt_ref[...].astype(BF16)
        wtb_ref[...] = wb
        return wb

    def finish(conv):
        return _silu(conv + cb_ref[...]).astype(BF16)

    @pl.when(j < n_z_tiles)
    def _():
        zs_ref[...] = _silu(_dot_nt(hn_ref[...], weight_tile())).astype(BF16)

    @pl.when(j >= n_z_tiles)
    def _():
        hist_sc[HISTORY_ROWS:, :] = _dot_nt(hn_ref[...], weight_tile())
        taps = taps_ref[...]
        if sample_len is None:
            hist_sc[:HISTORY_ROWS, :] = jnp.where(i % tiles_per_seq == 0, 0.0, carry_sc[jc])
            xbc_ref[...] = finish(_conv(hist_sc, taps))
            carry_sc[jc] = hist_sc[tm:, :]
            tail_ref[...] = hist_sc[HISTORY_ROWS + tm - V7X_SUBLANES:, :]
        else:
            _state_correction(buf_ref, taps, stage_sc, sample_len)
            hist_sc[:HISTORY_ROWS, :] = jnp.zeros((HISTORY_ROWS, hist_sc.shape[1]), F32)
            xbc_ref[...] = finish(_conv(hist_sc, taps, sample_len) + _staged(stage_sc))
            _emit_sample_state(hist_sc[HISTORY_ROWS:, :], stage_sc, nstate_ref, sample_len)


def _ssd_in(h_p, h_s, w_in_t, layer, w_dt_t, dt_b, taps, conv_b, buf, *, seq_len_p, seq_len_s, d_inner):
    n_prompt_rows, d = h_p.shape
    ms = h_s.shape[0]
    _, k, conv_dim = taps.shape
    hp = w_dt_t.shape[0]
    tm, tn = ROW_TILE, COL_TILE
    assert n_prompt_rows % tm == 0 and ms == tm and seq_len_p % tm == 0 and tm % seq_len_s == 0
    assert d_inner % tn == 0 and conv_dim % tn == 0 and k - 1 <= min(seq_len_s, V7X_SUBLANES)
    npt = n_prompt_rows // tm
    n_seq_s = ms // seq_len_s
    nz, nc = d_inner // tn, conv_dim // tn
    zc = lambda j: jnp.minimum(j, nz - 1)
    cc = lambda j: jnp.maximum(j - nz, 0)

    zs_s, xbc_s, nstate, dt_s, dtT_s, w_b = pl.pallas_call(
        functools.partial(_ssd_in_kernel, sample_len=seq_len_s, tiles_per_seq=None, n_z_tiles=nz),
        grid=(nz + nc,),
        in_specs=[
            pl.BlockSpec((tm, d), lambda j: (0, 0)),
            _layer_spec((tn, d), layer, lambda j: (j, 0)),
            pl.BlockSpec((hp, d), lambda j: (0, 0)),
            pl.BlockSpec((1, hp), lambda j: (0, 0)),
            _layer_spec((k, tn), layer, lambda j: (0, cc(j))),
            _layer_spec((1, tn), layer, lambda j: (0, cc(j))),
            _layer_spec((k - 1, n_seq_s, tn), layer, lambda j: (0, 0, cc(j))),
        ],
        out_specs=[
            pl.BlockSpec((tm, tn), lambda j: (0, zc(j))),
            pl.BlockSpec((tm, tn), lambda j: (0, cc(j))),
            pl.BlockSpec((k - 1, n_seq_s, tn), lambda j: (0, 0, cc(j))),
            pl.BlockSpec((tm, hp), lambda j: (0, 0)),
            pl.BlockSpec((hp, tm), lambda j: (0, 0)),
            pl.BlockSpec((tn, d), lambda j: (j, 0)),
        ],
        out_shape=[
            jax.ShapeDtypeStruct((ms, d_inner), BF16),
            jax.ShapeDtypeStruct((ms, conv_dim), BF16),
            jax.ShapeDtypeStruct((k - 1, n_seq_s, conv_dim), F32),
            jax.ShapeDtypeStruct((ms, hp), F32),
            jax.ShapeDtypeStruct((hp, ms), F32),
            jax.ShapeDtypeStruct(((nz + nc) * tn, d), BF16),
        ],
        scratch_shapes=[pltpu.VMEM((HISTORY_ROWS + tm, tn), F32),
                        pltpu.VMEM((tn // V7X_LANES, tm, V7X_LANES), F32)],
        compiler_params=_params("arbitrary"),
        name="ssd_in_sample",
    )(h_s, w_in_t, w_dt_t, dt_b, taps, conv_b, buf)

    tn = WIDE_COL_TILE
    assert d_inner % tn == 0 and conv_dim % tn == 0
    nz, nc = d_inner // tn, conv_dim // tn
    const = lambda i, j: (0, 0)
    zs_p, xbc_p, tail, dt_p, dtT_p = pl.pallas_call(
        functools.partial(_ssd_in_kernel, sample_len=None, tiles_per_seq=seq_len_p // tm, n_z_tiles=nz),
        grid=(npt, nz + nc),
        in_specs=[
            pl.BlockSpec((tm, d), lambda i, j: (i, 0)),
            pl.BlockSpec((tn, d), lambda i, j: (j, 0)),
            pl.BlockSpec((hp, d), const),
            pl.BlockSpec((1, hp), const),
            _layer_spec((k, tn), layer, lambda i, j: (0, cc(j))),
            _layer_spec((1, tn), layer, lambda i, j: (0, cc(j))),
        ],
        out_specs=[
            pl.BlockSpec((tm, tn), lambda i, j: (i, zc(j))),
            pl.BlockSpec((tm, tn), lambda i, j: (i, cc(j))),
            pl.BlockSpec((V7X_SUBLANES, tn), lambda i, j: (i, cc(j))),
            pl.BlockSpec((tm, hp), lambda i, j: (i, 0)),
            pl.BlockSpec((hp, tm), lambda i, j: (0, i)),
        ],
        out_shape=[
            jax.ShapeDtypeStruct((n_prompt_rows, d_inner), BF16),
            jax.ShapeDtypeStruct((n_prompt_rows, conv_dim), BF16),
            jax.ShapeDtypeStruct((npt * V7X_SUBLANES, conv_dim), F32),
            jax.ShapeDtypeStruct((n_prompt_rows, hp), F32),
            jax.ShapeDtypeStruct((hp, n_prompt_rows), F32),
        ],
        scratch_shapes=[pltpu.VMEM((nc, HISTORY_ROWS, tn), F32),
                        pltpu.VMEM((HISTORY_ROWS + tm, tn), F32)],
        compiler_params=_params("arbitrary", "arbitrary"),
        name="ssd_in_prompt",
    )(h_p, w_b, w_dt_t, dt_b, taps, conv_b)
    return ((zs_p, zs_s), (xbc_p, xbc_s), (dt_p, dt_s), (dtT_p, dtT_s)), tail, nstate


def _ssd_group_out(x, zs, cb, acum, acumT, dtT, mask, extra, ng, head0, heads_per_group, head_dim,
                   carried=None):
    heads_per_slab = V7X_LANES // head_dim
    parts = []
    for q in range(heads_per_group // heads_per_slab):
        cols = slice(q * V7X_LANES, (q + 1) * V7X_LANES)
        rhs = x[:, cols]
        if carried is not None:
            cg, st_t = carried
            rhs = jnp.concatenate([rhs, st_t[:, cols].astype(BF16)], axis=0)
        lanes = lax.broadcasted_iota(jnp.int32, rhs.shape, 1)
        acc = None
        for r in range(heads_per_slab):
            hd = head0 + q * heads_per_slab + r
            a_t = jnp.broadcast_to(acum[:, hd:hd + 1], cb.shape)
            seg = a_t - acumT[hd:hd + 1, :]
            lhs = (cb * jnp.exp(jnp.where(mask, seg, MASKED)) * dtT[hd:hd + 1, :]).astype(BF16)
            if carried is not None:
                lhs = jnp.concatenate([lhs, (cg * jnp.exp(a_t)).astype(BF16)], axis=1)
            in_head = jnp.logical_and(lanes >= r * head_dim, lanes < (r + 1) * head_dim)
            part = _dot(lhs, jnp.where(in_head, rhs, jnp.zeros_like(rhs)))
            acc = part if acc is None else acc + part
        parts.append(acc)
    y = jnp.concatenate(parts, axis=1) + extra
    gated = y * zs
    ms = jnp.mean(gated * gated, axis=-1, keepdims=True)
    return (gated * lax.rsqrt(ms + EPS) * ng).astype(BF16)


def _ssd_prompt_body(c, n_chunks, xs_ref, b_ref, c_ref, zs_ref, dt_ref, dtT_ref, alr_ref, alc_ref,
                     e_ref, d_ref, ng_ref, y_ref, state_ref, st_sc, *, n_groups, head_dim, d_state):
    @pl.when(c == 0)
    def _():
        st_sc[...] = jnp.zeros_like(st_sc)

    q_rows, d_inner = xs_ref.shape
    gw = d_inner // n_groups
    hpg = gw // head_dim
    row = lax.broadcasted_iota(jnp.int32, (q_rows, q_rows), 0)
    col = lax.broadcasted_iota(jnp.int32, (q_rows, q_rows), 1)
    causal = col <= row
    tril = jnp.where(causal, 1.0, 0.0).astype(BF16)
    triu = jnp.where(row <= col, 1.0, 0.0).astype(BF16)
    dt = dt_ref[...]
    dtT = dtT_ref[...]
    acum = _dot01_lhs(tril, dt * -jnp.exp(alr_ref[...]))
    acumT = _dot01_rhs(dtT * -jnp.exp(alc_ref[...]), triu)
    a_end = acum[q_rows - 1:q_rows, :]
    e = e_ref[...]
    x = xs_ref[...]
    xf = x.astype(F32)
    to_end = (xf * _dot((dt * jnp.exp(a_end - acum)).astype(BF16), e)).astype(BF16)
    decay = _dot01_rhs(jnp.broadcast_to(jnp.exp(a_end), (V7X_SUBLANES, a_end.shape[1])), e)[:1, :]
    skip = xf * d_ref[...]
    for g in range(n_groups):
        sl = slice(g * gw, (g + 1) * gw)
        ns = slice(g * d_state, (g + 1) * d_state)
        bg, cg = b_ref[:, ns], c_ref[:, ns]
        st = st_sc[:, sl]
        y_ref[:, sl] = _ssd_group_out(x[:, sl], zs_ref[:, sl].astype(F32), _dot_nt(cg, bg), acum,
                                      acumT, dtT, causal, skip[:, sl], ng_ref[:, sl],
                                      g * hpg, hpg, head_dim,
                                      carried=(cg.astype(F32), st))
        st_sc[:, sl] = decay[:, sl] * st + _dot_tn(bg, to_end[:, sl])

    @pl.when(c == n_chunks - 1)
    def _():
        for g in range(n_groups):
            state_ref[g * gw:(g + 1) * gw, :] = st_sc[:, g * gw:(g + 1) * gw].T


def _ssd_sample_body(xs_ref, b_ref, c_ref, zs_ref, dt_ref, dtT_ref, alr_ref, alc_ref, e_ref,
                     d_ref, ng_ref, st_ref, y_ref, nst_ref, *, seq_len, head_dim):
    q_rows, gw = xs_ref.shape
    n_seq = q_rows // seq_len
    row = lax.broadcasted_iota(jnp.int32, (q_rows, q_rows), 0)
    col = lax.broadcasted_iota(jnp.int32, (q_rows, q_rows), 1)
    same = (row // seq_len) == (col // seq_len)
    mask = jnp.logical_and(same, col <= row)
    tril = jnp.where(mask, 1.0, 0.0).astype(BF16)
    triu = jnp.where(jnp.logical_and(same, row <= col), 1.0, 0.0).astype(BF16)
    ends = jnp.where(col == (row // seq_len) * seq_len + (seq_len - 1), 1.0, 0.0).astype(BF16)
    dt = dt_ref[...]
    dtT = dtT_ref[...]
    acum = _dot01_lhs(tril, dt * -jnp.exp(alr_ref[...]))
    acumT = _dot01_rhs(dtT * -jnp.exp(alc_ref[...]), triu)
    a_end = _dot01_lhs(ends, acum)
    e = e_ref[...]
    x = xs_ref[...].astype(F32)
    to_endT = (x * _dot((dt * jnp.exp(a_end - acum)).astype(BF16), e)).T.astype(BF16)
    decayT = _dot01_rhs(jnp.exp(a_end), e).T
    from_start = _spread(jnp.exp(acum), e)
    bg = b_ref[...].astype(F32)
    cg = c_ref[...].astype(F32)
    seq_of_row = lax.broadcasted_iota(jnp.int32, bg.shape, 0) // seq_len
    inter = jnp.zeros((q_rows, gw), F32)
    for s in range(n_seq):
        st = st_ref[s]
        mine = seq_of_row == s
        inter = inter + _dot_nt(jnp.where(mine, cg, 0.0).astype(BF16), st.astype(BF16))
        bm = jnp.where(mine, bg, 0.0).astype(BF16)
        nst_ref[s] = decayT[:, s * seq_len:s * seq_len + 1] * st + _dot(to_endT, bm)
    inter = from_start * inter + x * d_ref[...]
    y_ref[...] = _ssd_group_out(xs_ref[...], zs_ref[...].astype(F32), _dot_nt(c_ref[...], b_ref[...]),
                                acum, acumT, dtT, mask, inter, ng_ref[...], 0, gw // head_dim, head_dim)


N_PROMPT_SCAN_INPUTS = 11
N_SAMPLE_SCAN_INPUTS = 12


def _ssd_scan_kernel(*refs, n_chunks, n_groups, seq_len_s, head_dim, d_state):
    a, b = N_PROMPT_SCAN_INPUTS, N_PROMPT_SCAN_INPUTS + N_SAMPLE_SCAN_INPUTS
    prompt_in, sample_in = refs[:a], refs[a:b]
    y_p, state_p, y_s, state_s, st_sc = refs[b:]
    _ssd_prompt_body(pl.program_id(0) % n_chunks, n_chunks, *prompt_in, y_p, state_p, st_sc,
                     n_groups=n_groups, head_dim=head_dim, d_state=d_state)
    _ssd_sample_body(*sample_in, y_s, state_s, seq_len=seq_len_s, head_dim=head_dim)


def _ssd_scan(xbc_p, zs_p, dt, dtT, alog_row, alog_col, expand, d_x, ng,
              xbc_s, zs_s, dt_g, dtT_g, alog_row_g, alog_col_g, state,
              *, n_seq_p, seq_len_p, seq_len_s, d_inner, n_groups, head_dim, d_state):
    q = SSD_CHUNK
    n_rows_s = xbc_s.shape[0]
    assert seq_len_p % q == 0 and q == d_state
    assert n_rows_s % q == 0 and q % seq_len_s == 0
    nc = seq_len_p // q
    nb = n_rows_s // q
    assert n_seq_p * nc == nb * n_groups
    spb = q // seq_len_s
    gw = d_inner // n_groups
    hp = dt.shape[1]
    hpg_rows = dtT_g.shape[1]
    gn = n_groups * d_state
    assert d_inner % gn == 0
    b_col0 = d_inner // d_state
    const = lambda t: (0, 0)
    sb = lambda t: t // n_groups
    g = lambda t: t % n_groups
    prompt_specs = [
        pl.BlockSpec((q, d_inner), lambda t: (t, 0)),
        pl.BlockSpec((q, gn), lambda t: (t, d_inner // gn)),
        pl.BlockSpec((q, gn), lambda t: (t, d_inner // gn + 1)),
        pl.BlockSpec((q, d_inner), lambda t: (t, 0)),
        pl.BlockSpec((q, hp), lambda t: (t, 0)),
        pl.BlockSpec((hp, q), lambda t: (0, t)),
        pl.BlockSpec((1, hp), const),
        pl.BlockSpec((hp, 1), const),
        pl.BlockSpec((hp, d_inner), const),
        pl.BlockSpec((1, d_inner), const),
        pl.BlockSpec((1, d_inner), const),
    ]
    sample_specs = [
        pl.BlockSpec((q, gw), lambda t: (sb(t), g(t))),
        pl.BlockSpec((q, d_state), lambda t: (sb(t), b_col0 + g(t))),
        pl.BlockSpec((q, d_state), lambda t: (sb(t), b_col0 + n_groups + g(t))),
        pl.BlockSpec((q, gw), lambda t: (sb(t), g(t))),
        pl.BlockSpec((None, q, hp), lambda t: (g(t), sb(t), 0)),
        pl.BlockSpec((None, hpg_rows, q), lambda t: (g(t), 0, sb(t))),
        pl.BlockSpec((None, 1, hp), lambda t: (g(t), 0, 0)),
        pl.BlockSpec((None, hpg_rows, 1), lambda t: (g(t), 0, 0)),
        pl.BlockSpec((hp, gw), const),
        pl.BlockSpec((1, gw), lambda t: (0, g(t))),
        pl.BlockSpec((1, gw), lambda t: (0, g(t))),
        pl.BlockSpec((spb, None, gw, d_state), lambda t: (sb(t), g(t), 0, 0)),
    ]
    assert len(prompt_specs) == N_PROMPT_SCAN_INPUTS and len(sample_specs) == N_SAMPLE_SCAN_INPUTS
    kern = functools.partial(_ssd_scan_kernel, n_chunks=nc, n_groups=n_groups, seq_len_s=seq_len_s,
                             head_dim=head_dim, d_state=d_state)
    return pl.pallas_call(
        kern,
        grid=(n_seq_p * nc,),
        in_specs=prompt_specs + sample_specs,
        out_specs=[
            pl.BlockSpec((q, d_inner), lambda t: (t, 0)),
            pl.BlockSpec((d_inner, d_state), lambda t: (t // nc, 0)),
            pl.BlockSpec((q, gw), lambda t: (sb(t), g(t))),
            pl.BlockSpec((spb, None, gw, d_state), lambda t: (sb(t), g(t), 0, 0)),
        ],
        out_shape=[
            jax.ShapeDtypeStruct((n_seq_p * seq_len_p, d_inner), BF16),
            jax.ShapeDtypeStruct((n_seq_p * d_inner, d_state), F32),
            jax.ShapeDtypeStruct((n_rows_s, d_inner), BF16),
            jax.ShapeDtypeStruct(state.shape, F32),
        ],
        scratch_shapes=[pltpu.VMEM((d_state, d_inner), F32)],
        compiler_params=_params("arbitrary"),
        name="ssd_scan",
    )(xbc_p, xbc_p, xbc_p, zs_p, dt, dtT, alog_row, alog_col, expand, d_x, ng,
      xbc_s, xbc_s, xbc_s, zs_s, dt_g, dtT_g, alog_row_g, alog_col_g, expand[:, :gw], d_x, ng, state)


def _prompt_conv_state(tail, *, n_prompt_tiles, tiles_per_seq, km1):
    t = tail.reshape(-1, V7X_SUBLANES, tail.shape[1])[:n_prompt_tiles]
    return t[tiles_per_seq - 1::tiles_per_seq, V7X_SUBLANES - km1:, :]


def kernel(x_prompt, x_sample, p_prompt, p_sample, state_sc_conv, state_ssd_conv, state_ssd, g_mix, g_ffn, g_ple, g_final, sc_w_in, sc_w_conv, sc_w_out, ssd_w_in, ssd_conv_w, ssd_conv_b, ssd_dt_bias, ssd_a_log, ssd_d, ssd_norm_g, ssd_w_out, ffn_w_gate, ffn_w_up, ffn_w_down, ple_w_proj, ple_w_gate):
    bp, lp, d = x_prompt.shape
    bs, ls, _ = x_sample.shape
    depth = g_mix.shape[0]
    mp, ms = bp * lp, bs * ls
    pdim = p_prompt.shape[-1]
    n_heads, head_dim, d_state = state_ssd.shape[2:]
    d_inner = n_heads * head_dim
    conv_dim = ssd_conv_w.shape[-1]
    n_groups = (conv_dim - d_inner) // (2 * d_state)
    hpg = n_heads // n_groups
    assert n_heads <= V7X_LANES and V7X_LANES % head_dim == 0 and d_state == V7X_LANES
    npt = mp // ROW_TILE
    tps = lp // ROW_TILE
    row = lambda v: v.reshape(1, -1)
    pp = p_prompt.reshape(depth, mp, pdim)
    ps = p_sample.reshape(depth, ms, pdim)

    h = (x_prompt.reshape(mp, d), x_sample.reshape(ms, d))
    hn = None
    sc_p, sc_s, cv_p, cv_s, st_p, st_s = [], [], [], [], [], []
    for i in range(depth):
        j = i // 2
        if i % 2 == 0:
            km1 = sc_w_conv.shape[1] - 1
            gated, tail, nstate = _short_conv_in(
                *h, row(g_mix[i]), sc_w_in, j, sc_w_conv, jnp.swapaxes(state_sc_conv, 1, 2),
                seq_len_p=lp, seq_len_s=ls)
            sc_p.append(_prompt_conv_state(tail, n_prompt_tiles=npt, tiles_per_seq=tps, km1=km1))
            sc_s.append(jnp.swapaxes(nstate, 0, 1))
            h = _matmul_residual(h, gated, sc_w_out, j, n_prompt_rows=mp)
        else:
            km1 = ssd_conv_w.shape[1] - 1
            zx = d_inner + conv_dim
            pad_h = V7X_LANES - n_heads
            w_in_t = jnp.swapaxes(ssd_w_in, 1, 2)
            w_dt_t = jnp.pad(w_in_t[j, zx:, :], ((0, pad_h), (0, 0))).astype(BF16)
            dt_b = jnp.pad(ssd_dt_bias[j], (0, pad_h))
            alog = jnp.pad(ssd_a_log[j], (0, pad_h))
            assert hn is not None
            (zs, xbc, dt, dtT), tail, nstate = _ssd_in(
                *hn, w_in_t, j, w_dt_t, row(dt_b),
                ssd_conv_w, ssd_conv_b.reshape(ssd_conv_b.shape[0], 1, conv_dim),
                jnp.swapaxes(state_ssd_conv, 1, 2),
                seq_len_p=lp, seq_len_s=ls, d_inner=d_inner)
            cv_p.append(_prompt_conv_state(tail, n_prompt_tiles=npt, tiles_per_seq=tps, km1=km1))
            cv_s.append(jnp.swapaxes(nstate, 0, 1))
            head_of_lane = jnp.arange(d_inner, dtype=jnp.int32) // head_dim
            expand = (jnp.arange(V7X_LANES, dtype=jnp.int32)[:, None] == head_of_lane[None, :]).astype(BF16)
            d_x = row(jnp.repeat(ssd_d[j], head_dim))
            ng = row(ssd_norm_g[j])
            dt_g = jnp.stack([jnp.roll(dt[1], -g * hpg, axis=1) for g in range(n_groups)])
            alog_g = jnp.stack([jnp.roll(alog, -g * hpg) for g in range(n_groups)])
            dtT_g = dtT[1][:n_heads].reshape(n_groups, hpg, ms)
            y_p, new_p, y_s, new_s = _ssd_scan(
                xbc[0], zs[0], dt[0], dtT[0], row(alog), alog.reshape(-1, 1), expand, d_x, ng,
                xbc[1], zs[1], dt_g, dtT_g, alog_g.reshape(n_groups, 1, -1),
                ssd_a_log[j].reshape(n_groups, hpg, 1),
                state_ssd[j].reshape(bs, n_groups, hpg * head_dim, d_state),
                n_seq_p=bp, seq_len_p=lp, seq_len_s=ls,
                d_inner=d_inner, n_groups=n_groups, head_dim=head_dim, d_state=d_state)
            st_p.append(new_p.reshape(bp, n_heads, head_dim, d_state))
            st_s.append(new_s.reshape(bs, n_heads, head_dim, d_state))
            h = _matmul_residual(h, (y_p, y_s), ssd_w_out, j, n_prompt_rows=mp)
        h = _ffn(h, row(g_ffn[i]), ffn_w_gate, ffn_w_up, ffn_w_down, i, n_prompt_rows=mp)
        if i == depth - 1:
            (h,) = _ple(h, pp, ps, row(g_ple[i]), ple_w_gate, ple_w_proj, i, row(g_final), True)
        else:
            h, hn = _ple(h, pp, ps, row(g_ple[i]), ple_w_gate, ple_w_proj, i, row(g_mix[i + 1]), False)
    y_p, y_s = h
    return (y_p.reshape(bp, lp, d), y_s.reshape(bs, ls, d), jnp.stack(sc_p), jnp.stack(sc_s),
            jnp.stack(cv_p), jnp.stack(cv_s), jnp.stack(st_p), jnp.stack(st_s))
```

```python
import functools

import jax
import jax.numpy as jnp
from jax import lax
from jax.experimental import pallas as pl
from jax.experimental.pallas import tpu as pltpu

F32 = jnp.float32
BF16 = jnp.bfloat16
EPS = 1e-6
MASKED = -1e30
V7X_LANES = 128
V7X_SUBLANES = 8
HISTORY_ROWS = 16
V7X_VMEM_LIMIT = 56 * 1024 * 1024

ROW_TILE = 1024
HALF_ROW_TILE = 512
PLE_ROW_TILE = 256
COL_TILE = 512
WIDE_COL_TILE = 2048
OUT_PROJ_COL_TILE = 1024
NARROW_COL_TILE = 256
SSD_CHUNK = 128
WEIGHT_SLAB_BYTES = 16 * 1024 * 1024


def _params(*sem):
    return pltpu.CompilerParams(dimension_semantics=sem, vmem_limit_bytes=V7X_VMEM_LIMIT)


def _dot(a, b):
    return jnp.dot(a, b, preferred_element_type=F32)


def _dot_nt(a, b):
    return lax.dot_general(a, b, (((1,), (1,)), ((), ())), preferred_element_type=F32)


def _dot_tn(a, b):
    return lax.dot_general(a, b, (((0,), (0,)), ((), ())), preferred_element_type=F32)


def _split3(a):
    a1 = a.astype(BF16)
    r1 = a - a1.astype(F32)
    a2 = r1.astype(BF16)
    a3 = (r1 - a2.astype(F32)).astype(BF16)
    return a3, a2, a1


def _dot01_rhs(a, e):
    p3, p2, p1 = _split3(a)
    return (_dot(p3, e) + _dot(p2, e)) + _dot(p1, e)


def _spread(a, e):
    hi = a.astype(BF16)
    lo = (a - hi.astype(F32)).astype(BF16)
    return _dot(lo, e) + _dot(hi, e)


def _dot01_lhs(t, a):
    p3, p2, p1 = _split3(a)
    return (_dot(t, p3) + _dot(t, p2)) + _dot(t, p1)


def _rmsnorm(x, g):
    ms = jnp.mean(x * x, axis=-1, keepdims=True)
    return x * lax.rsqrt(ms + EPS) * g


def _softplus(x):
    return jnp.maximum(x, 0.0) + jnp.log1p(jnp.exp(-jnp.abs(x)))


def _silu(x):
    half = 0.5 * x
    return half + half * jnp.tanh(half)


def _layer_spec(block, layer, imap):
    return pl.BlockSpec((None,) + tuple(block), lambda *a: (layer,) + tuple(imap(*a)))


def _conv(hist_ref, taps, seq_len=None, row0=0, n_rows=None):
    k = taps.shape[0]
    if n_rows is None:
        n_rows = hist_ref.shape[0] - HISTORY_ROWS
    out = taps[k - 1:k, :] * hist_ref[pl.ds(HISTORY_ROWS + row0, n_rows), :]
    if seq_len is not None:
        assert row0 % seq_len == 0
        t = lax.broadcasted_iota(jnp.int32, out.shape, 0) % seq_len
    for d in range(1, k):
        sh = hist_ref[pl.ds(HISTORY_ROWS + row0 - d, n_rows), :]
        if seq_len is not None:
            sh = jnp.where(t >= d, sh, 0.0)
        out = out + taps[k - 1 - d:k - d, :] * sh
    return out


def _state_correction(buf_ref, taps, stage_sc, seq_len):
    k = taps.shape[0]
    km1 = k - 1
    n_seq = stage_sc.shape[1] // seq_len
    stage_sc[...] = jnp.zeros_like(stage_sc)
    rows = [buf_ref[r] for r in range(km1)]
    for t in range(km1):
        acc = None
        for d in range(t + 1, k):
            term = taps[k - 1 - d:k - d, :] * rows[km1 + t - d]
            acc = term if acc is None else acc + term
        for c in range(stage_sc.shape[0]):
            stage_sc[c, pl.ds(t, n_seq, stride=seq_len), :] = acc[:, c * V7X_LANES:(c + 1) * V7X_LANES]


def _staged(stage_sc):
    return jnp.concatenate([stage_sc[c] for c in range(stage_sc.shape[0])], axis=1)


def _emit_sample_state(u, stage_sc, nstate_ref, seq_len):
    km1 = nstate_ref.shape[0]
    n_chunks = stage_sc.shape[0]
    n_seq = stage_sc.shape[1] // seq_len
    for c in range(n_chunks):
        stage_sc[c] = u[:, c * V7X_LANES:(c + 1) * V7X_LANES]
    for r in range(km1):
        nstate_ref[r] = jnp.concatenate(
            [stage_sc[c, pl.ds(seq_len - km1 + r, n_seq, stride=seq_len), :] for c in range(n_chunks)],
            axis=1)


def _sc_in_sample_kernel(x_ref, g_ref, wb_ref, wc_ref, wv_ref, taps_ref, buf_ref,
                         gated_ref, nstate_ref, wbb_ref, wcb_ref, wvb_ref, hn_sc, hist_sc, stage_sc,
                         *, sample_len):
    @pl.when(pl.program_id(0) == 0)
    def _():
        hn_sc[...] = _rmsnorm(x_ref[...], g_ref[...]).astype(BF16)

    wb, wc, wv = wb_ref[...].astype(BF16), wc_ref[...].astype(BF16), wv_ref[...].astype(BF16)
    wbb_ref[...], wcb_ref[...], wvb_ref[...] = wb, wc, wv
    taps = taps_ref[...]
    _state_correction(buf_ref, taps, stage_sc, sample_len)
    hn = hn_sc[...]
    bg = _dot(hn, wb)
    u = _dot(hn, wc) * _dot(hn, wv)
    hist_sc[:HISTORY_ROWS, :] = jnp.zeros((HISTORY_ROWS, u.shape[1]), F32)
    hist_sc[HISTORY_ROWS:, :] = u
    gated_ref[...] = (bg * (_conv(hist_sc, taps, sample_len) + _staged(stage_sc))).astype(BF16)
    _emit_sample_state(u, stage_sc, nstate_ref, sample_len)


def _sc_in_prompt_kernel(x_ref, g_ref, wb_ref, wc_ref, wv_ref, taps_ref,
                         gated_ref, tail_ref, hn_sc, carry_sc, hist_sc, *, tiles_per_seq):
    i = pl.program_id(0)
    j = pl.program_id(1)

    @pl.when(j == 0)
    def _():
        hn_sc[...] = _rmsnorm(x_ref[...], g_ref[...]).astype(BF16)

    hn = hn_sc[...]
    bg = _dot(hn, wb_ref[...])
    u = _dot(hn, wc_ref[...]) * _dot(hn, wv_ref[...])
    hist_sc[:HISTORY_ROWS, :] = jnp.where(i % tiles_per_seq == 0, 0.0, carry_sc[j])
    hist_sc[HISTORY_ROWS:, :] = u
    gated_ref[...] = (bg * _conv(hist_sc, taps_ref[...])).astype(BF16)
    carry_sc[j] = u[u.shape[0] - HISTORY_ROWS:, :]
    tail_ref[...] = u[u.shape[0] - V7X_SUBLANES:, :]


def _short_conv_in(xp, xs, g, w_in, layer, taps, buf, *, seq_len_p, seq_len_s):
    mp, d = xp.shape
    ms = xs.shape[0]
    k = taps.shape[1]
    tm = ROW_TILE
    assert mp % tm == 0 and ms == tm and seq_len_p % tm == 0 and tm % seq_len_s == 0
    assert k - 1 <= min(seq_len_s, V7X_SUBLANES)
    n_seq_s = ms // seq_len_s

    tn = NARROW_COL_TILE
    nj = d // tn
    assert d % tn == 0 and tn % V7X_LANES == 0
    gated_s, nstate, wb, wc, wv = pl.pallas_call(
        functools.partial(_sc_in_sample_kernel, sample_len=seq_len_s),
        grid=(nj,),
        in_specs=[
            pl.BlockSpec((tm, d), lambda j: (0, 0)),
            pl.BlockSpec((1, d), lambda j: (0, 0)),
            _layer_spec((d, tn), layer, lambda j: (0, j)),
            _layer_spec((d, tn), layer, lambda j: (0, nj + j)),
            _layer_spec((d, tn), layer, lambda j: (0, 2 * nj + j)),
            _layer_spec((k, tn), layer, lambda j: (0, j)),
            _layer_spec((k - 1, n_seq_s, tn), layer, lambda j: (0, 0, j)),
        ],
        out_specs=[pl.BlockSpec((tm, tn), lambda j: (0, j)),
                   pl.BlockSpec((k - 1, n_seq_s, tn), lambda j: (0, 0, j))]
        + [pl.BlockSpec((d, tn), lambda j: (0, j))] * 3,
        out_shape=[jax.ShapeDtypeStruct((ms, d), BF16), jax.ShapeDtypeStruct((k - 1, n_seq_s, d), F32)]
        + [jax.ShapeDtypeStruct((d, d), BF16)] * 3,
        scratch_shapes=[pltpu.VMEM((tm, d), BF16), pltpu.VMEM((HISTORY_ROWS + tm, tn), F32),
                        pltpu.VMEM((tn // V7X_LANES, tm, V7X_LANES), F32)],
        compiler_params=_params("arbitrary"),
        name="short_conv_in_sample",
    )(xs, g, w_in, w_in, w_in, taps, buf)

    tn = COL_TILE
    nj = d // tn
    assert d % tn == 0
    npt = mp // tm
    gated_p, tail = pl.pallas_call(
        functools.partial(_sc_in_prompt_kernel, tiles_per_seq=seq_len_p // tm),
        grid=(npt, nj),
        in_specs=[
            pl.BlockSpec((tm, d), lambda i, j: (i, 0)),
            pl.BlockSpec((1, d), lambda i, j: (0, 0)),
            pl.BlockSpec((d, tn), lambda i, j: (0, j)),
            pl.BlockSpec((d, tn), lambda i, j: (0, j)),
            pl.BlockSpec((d, tn), lambda i, j: (0, j)),
            _layer_spec((k, tn), layer, lambda i, j: (0, j)),
        ],
        out_specs=[pl.BlockSpec((tm, tn), lambda i, j: (i, j)),
                   pl.BlockSpec((V7X_SUBLANES, tn), lambda i, j: (i, j))],
        out_shape=[jax.ShapeDtypeStruct((mp, d), BF16),
                   jax.ShapeDtypeStruct((npt * V7X_SUBLANES, d), F32)],
        scratch_shapes=[pltpu.VMEM((tm, d), BF16), pltpu.VMEM((nj, HISTORY_ROWS, tn), F32),
                        pltpu.VMEM((HISTORY_ROWS + tm, tn), F32)],
        compiler_params=_params("arbitrary", "arbitrary"),
        name="short_conv_in_prompt",
    )(xp, g, wb, wc, wv, taps)
    return (gated_p, gated_s), tail, nstate


def _res_kernel(res_ref, a_ref, w_ref, out_ref, *cast_ref):
    if cast_ref:
        wb_ref, = cast_ref

        @pl.when(pl.program_id(1) == 0)
        def _():
            wb_ref[...] = w_ref[...].astype(BF16)
    else:
        wb_ref = w_ref
    out_ref[...] = res_ref[...] + _dot(a_ref[...], wb_ref[...])


def _matmul_residual(res, a, w, layer):
    _, k, n = w.shape

    def call(rows_res, rows_a, tm, tn, weight, weight_spec, cast, name):
        m = rows_res.shape[0]
        assert m % tm == 0 and n % tn == 0 and tn % V7X_LANES == 0
        out_specs = [pl.BlockSpec((tm, tn), lambda j, i: (i, j))]
        out_shape = [jax.ShapeDtypeStruct((m, n), F32)]
        if cast:
            out_specs.append(pl.BlockSpec((k, tn), lambda j, i: (0, j)))
            out_shape.append(jax.ShapeDtypeStruct((k, n), BF16))
        return pl.pallas_call(
            _res_kernel,
            grid=(n // tn, m // tm),
            in_specs=[pl.BlockSpec((tm, tn), lambda j, i: (i, j)),
                      pl.BlockSpec((tm, k), lambda j, i: (i, 0)), weight_spec(tn)],
            out_specs=out_specs,
            out_shape=out_shape,
            compiler_params=_params("arbitrary", "arbitrary"),
            name=name,
        )(rows_res, rows_a, weight)

    f32_spec = lambda tn: pl.BlockSpec((None, k, tn), lambda j, i: (layer, 0, j),
                                       pipeline_mode=pl.Buffered(1))
    bf16_spec = lambda tn: pl.BlockSpec((k, tn), lambda j, i: (0, j))
    out_s, w_b = call(res[1], a[1], HALF_ROW_TILE, min(n, WEIGHT_SLAB_BYTES // (4 * k)), w, f32_spec,
                      True, "matmul_residual_sample")
    out_p, = call(res[0], a[0], ROW_TILE, min(n, OUT_PROJ_COL_TILE), w_b, bf16_spec, False,
                  "matmul_residual_prompt")
    return out_p, out_s


def _ffn_kernel(h_ref, g_ref, wg_ref, wu_ref, wd_ref, out_ref, *rest):
    *cast_refs, hn_sc = rest

    @pl.when(pl.program_id(1) == 0)
    def _():
        h = h_ref[...]
        hn_sc[...] = _rmsnorm(h, g_ref[...]).astype(BF16)
        out_ref[...] = h

    wg, wu, wd = wg_ref[...], wu_ref[...], wd_ref[...]
    if cast_refs:
        wg, wu, wd = wg.astype(BF16), wu.astype(BF16), wd.astype(BF16)
        for ref, w in zip(cast_refs, (wg, wu, wd)):
            ref[...] = w
    hn = hn_sc[...]
    gate = _dot(hn, wg)
    act = (_silu(gate) * _dot(hn, wu)).astype(BF16)
    out_ref[...] += _dot(act, wd)


def _ffn(h, g, w_gate, w_up, w_down, layer):
    h_p, h_s = h
    d = h_p.shape[1]
    f = w_gate.shape[2]
    tm = ROW_TILE
    assert h_p.shape[0] % tm == 0 and h_s.shape[0] == tm
    assert f % COL_TILE == 0 and f % NARROW_COL_TILE == 0
    npt = h_p.shape[0] // tm

    def call(rows, n_row_tiles, tf, weights, weight_specs, cast_outputs, name):
        out_specs = [pl.BlockSpec((tm, d), lambda i, j: (i, 0))]
        out_shape = [jax.ShapeDtypeStruct((n_row_tiles * tm, d), F32)]
        if cast_outputs:
            out_specs += [pl.BlockSpec((d, tf), lambda i, j: (0, j)), pl.BlockSpec((d, tf), lambda i, j: (0, j)),
                          pl.BlockSpec((tf, d), lambda i, j: (j, 0))]
            out_shape += [jax.ShapeDtypeStruct((d, f), BF16), jax.ShapeDtypeStruct((d, f), BF16),
                          jax.ShapeDtypeStruct((f, d), BF16)]
        h_mode = dict(pipeline_mode=pl.Buffered(1)) if n_row_tiles == 1 else {}
        return pl.pallas_call(
            _ffn_kernel,
            grid=(n_row_tiles, f // tf),
            in_specs=[pl.BlockSpec((tm, d), lambda i, j: (i, 0), **h_mode),
                      pl.BlockSpec((1, d), lambda i, j: (0, 0))] + weight_specs(tf),
            out_specs=out_specs,
            out_shape=out_shape,
            scratch_shapes=[pltpu.VMEM((tm, d), BF16)],
            compiler_params=_params("arbitrary", "arbitrary"),
            name=name,
        )(rows, g, *weights)

    f32_specs = lambda tf: [_layer_spec((d, tf), layer, lambda i, j: (0, j)),
                            _layer_spec((d, tf), layer, lambda i, j: (0, j)),
                            _layer_spec((tf, d), layer, lambda i, j: (j, 0))]
    bf16_specs = lambda tf: [pl.BlockSpec((d, tf), lambda i, j: (0, j)), pl.BlockSpec((d, tf), lambda i, j: (0, j)),
                             pl.BlockSpec((tf, d), lambda i, j: (j, 0))]
    sample, wg_b, wu_b, wd_b = call(h_s, 1, NARROW_COL_TILE, (w_gate, w_up, w_down), f32_specs, True,
                                    "swiglu_ffn_sample")
    prompt, = call(h_p, npt, COL_TILE, (wg_b, wu_b, wd_b), bf16_specs, False, "swiglu_ffn_prompt")
    return prompt, sample


def _ple_kernel(h_ref, p_ref, g_ref, wg_ref, wp_ref, g2_ref, *rest, cast, final):
    if cast:
        *rest, wgb_ref, wpb_ref = rest

        @pl.when(pl.program_id(0) == 0)
        def _():
            wgb_ref[...] = wg_ref[...].astype(BF16)
            wpb_ref[...] = wp_ref[...].astype(BF16)
    else:
        wgb_ref, wpb_ref = wg_ref, wp_ref
    h = h_ref[...]
    gate = jax.nn.sigmoid(_dot(_rmsnorm(h, g_ref[...]).astype(BF16), wgb_ref[...]))
    out = h + _dot(p_ref[...].astype(BF16), wpb_ref[...]) * gate
    normed = _rmsnorm(out, g2_ref[...])
    if final:
        rest[0][...] = normed
    else:
        rest[0][...] = out
        rest[1][...] = normed.astype(BF16)


def _ple(h, pp, ps, g, w_gate, w_proj, layer, g_after, final):
    hp, hs = h
    d = hp.shape[1]
    pdim = pp.shape[2]
    const = lambda i: (0, 0)

    def call(rows, p, tm, weights, weight_specs, cast, name):
        m = rows.shape[0]
        assert m % tm == 0
        row_spec = pl.BlockSpec((tm, d), lambda i: (i, 0))
        out_specs = [row_spec]
        out_shape = [jax.ShapeDtypeStruct((m, d), F32)]
        if not final:
            out_specs.append(row_spec)
            out_shape.append(jax.ShapeDtypeStruct((m, d), BF16))
        if cast:
            out_specs += [pl.BlockSpec((d, d), const), pl.BlockSpec((pdim, d), const)]
            out_shape += [jax.ShapeDtypeStruct((d, d), BF16), jax.ShapeDtypeStruct((pdim, d), BF16)]
        return pl.pallas_call(
            functools.partial(_ple_kernel, cast=cast, final=final),
            grid=(m // tm,),
            in_specs=[row_spec, _layer_spec((tm, pdim), layer, lambda i: (i, 0)),
                      pl.BlockSpec((1, d), const)] + weight_specs + [pl.BlockSpec((1, d), const)],
            out_specs=out_specs,
            out_shape=out_shape,
            compiler_params=_params("arbitrary"),
            name=name,
        )(rows, p, g, *weights, g_after)

    resident = dict(pipeline_mode=pl.Buffered(1))
    f32_specs = [pl.BlockSpec((None, d, d), lambda i: (layer, 0, 0), **resident),
                 pl.BlockSpec((None, pdim, d), lambda i: (layer, 0, 0), **resident)]
    bf16_specs = [pl.BlockSpec((d, d), const, **resident), pl.BlockSpec((pdim, d), const, **resident)]
    tag = "ple_final" if final else "ple"
    *outs_s, wg_b, wp_b = call(hs, ps, PLE_ROW_TILE, (w_gate, w_proj), f32_specs, True, tag + "_sample")
    outs_p = call(hp, pp, HALF_ROW_TILE, (wg_b, wp_b), bf16_specs, False, tag + "_prompt")
    return tuple(zip(outs_p, outs_s))


def _ssd_in_kernel(hn_ref, wt_ref, wdtT_ref, dtb_ref, taps_ref, cb_ref, *rest,
                   sample_len, tiles_per_seq, n_z_tiles):
    if sample_len is not None:
        (buf_ref, zs_ref, xbc_ref, nstate_ref, dt_ref, dtT_ref, wtb_ref, hist_sc, stage_sc) = rest
        i, j = 0, pl.program_id(0)
    else:
        zs_ref, xbc_ref, tail_ref, dt_ref, dtT_ref, carry_sc, hist_sc = rest
        i, j = pl.program_id(0), pl.program_id(1)
    jc = j - n_z_tiles
    tm = hn_ref.shape[0]

    @pl.when(j == 0)
    def _():
        dt = _softplus(_dot_nt(hn_ref[...], wdtT_ref[...]) + dtb_ref[...])
        dt_ref[...] = dt
        dtT_ref[...] = dt.T

    def weight_tile():
        if sample_len is None:
            return wt_ref[...]
        wb = wt_ref[...].astype(BF16)
        wtb_ref[...] = wb
        return wb

    def finish(conv):
        return _silu(conv + cb_ref[...]).astype(BF16)

    @pl.when(j < n_z_tiles)
    def _():
        zs_ref[...] = _silu(_dot_nt(hn_ref[...], weight_tile())).astype(BF16)

    @pl.when(j >= n_z_tiles)
    def _():
        hist_sc[HISTORY_ROWS:, :] = _dot_nt(hn_ref[...], weight_tile())
        taps = taps_ref[...]
        if sample_len is None:
            hist_sc[:HISTORY_ROWS, :] = jnp.where(i % tiles_per_seq == 0, 0.0, carry_sc[jc])
            xbc_ref[...] = finish(_conv(hist_sc, taps))
            carry_sc[jc] = hist_sc[tm:, :]
            tail_ref[...] = hist_sc[HISTORY_ROWS + tm - V7X_SUBLANES:, :]
        else:
            _state_correction(buf_ref, taps, stage_sc, sample_len)
            hist_sc[:HISTORY_ROWS, :] = jnp.zeros((HISTORY_ROWS, hist_sc.shape[1]), F32)
            xbc_ref[...] = finish(_conv(hist_sc, taps, sample_len) + _staged(stage_sc))
            _emit_sample_state(hist_sc[HISTORY_ROWS:, :], stage_sc, nstate_ref, sample_len)


def _ssd_in(h_p, h_s, w_in_t, layer, w_dt_t, dt_b, taps, conv_b, buf, *, seq_len_p, seq_len_s, d_inner):
    n_prompt_rows, d = h_p.shape
    ms = h_s.shape[0]
    _, k, conv_dim = taps.shape
    hp = w_dt_t.shape[0]
    tm, tn = ROW_TILE, COL_TILE
    assert n_prompt_rows % tm == 0 and ms == tm and seq_len_p % tm == 0 and tm % seq_len_s == 0
    assert d_inner % tn == 0 and conv_dim % tn == 0 and k - 1 <= min(seq_len_s, V7X_SUBLANES)
    npt = n_prompt_rows // tm
    n_seq_s = ms // seq_len_s
    nz, nc = d_inner // tn, conv_dim // tn
    zc = lambda j: jnp.minimum(j, nz - 1)
    cc = lambda j: jnp.maximum(j - nz, 0)

    zs_s, xbc_s, nstate, dt_s, dtT_s, w_b = pl.pallas_call(
        functools.partial(_ssd_in_kernel, sample_len=seq_len_s, tiles_per_seq=None, n_z_tiles=nz),
        grid=(nz + nc,),
        in_specs=[
            pl.BlockSpec((tm, d), lambda j: (0, 0)),
            _layer_spec((tn, d), layer, lambda j: (j, 0)),
            pl.BlockSpec((hp, d), lambda j: (0, 0)),
            pl.BlockSpec((1, hp), lambda j: (0, 0)),
            _layer_spec((k, tn), layer, lambda j: (0, cc(j))),
            _layer_spec((1, tn), layer, lambda j: (0, cc(j))),
            _layer_spec((k - 1, n_seq_s, tn), layer, lambda j: (0, 0, cc(j))),
        ],
        out_specs=[
            pl.BlockSpec((tm, tn), lambda j: (0, zc(j))),
            pl.BlockSpec((tm, tn), lambda j: (0, cc(j))),
            pl.BlockSpec((k - 1, n_seq_s, tn), lambda j: (0, 0, cc(j))),
            pl.BlockSpec((tm, hp), lambda j: (0, 0)),
            pl.BlockSpec((hp, tm), lambda j: (0, 0)),
            pl.BlockSpec((tn, d), lambda j: (j, 0)),
        ],
        out_shape=[
            jax.ShapeDtypeStruct((ms, d_inner), BF16),
            jax.ShapeDtypeStruct((ms, conv_dim), BF16),
            jax.ShapeDtypeStruct((k - 1, n_seq_s, conv_dim), F32),
            jax.ShapeDtypeStruct((ms, hp), F32),
            jax.ShapeDtypeStruct((hp, ms), F32),
            jax.ShapeDtypeStruct(((nz + nc) * tn, d), BF16),
        ],
        scratch_shapes=[pltpu.VMEM((HISTORY_ROWS + tm, tn), F32),
                        pltpu.VMEM((tn // V7X_LANES, tm, V7X_LANES), F32)],
        compiler_params=_params("arbitrary"),
        name="ssd_in_sample",
    )(h_s, w_in_t, w_dt_t, dt_b, taps, conv_b, buf)

    tn = WIDE_COL_TILE
    assert d_inner % tn == 0 and conv_dim % tn == 0
    nz, nc = d_inner // tn, conv_dim // tn
    const = lambda i, j: (0, 0)
    zs_p, xbc_p, tail, dt_p, dtT_p = pl.pallas_call(
        functools.partial(_ssd_in_kernel, sample_len=None, tiles_per_seq=seq_len_p // tm, n_z_tiles=nz),
        grid=(npt, nz + nc),
        in_specs=[
            pl.BlockSpec((tm, d), lambda i, j: (i, 0)),
            pl.BlockSpec((tn, d), lambda i, j: (j, 0)),
            pl.BlockSpec((hp, d), const),
            pl.BlockSpec((1, hp), const),
            _layer_spec((k, tn), layer, lambda i, j: (0, cc(j))),
            _layer_spec((1, tn), layer, lambda i, j: (0, cc(j))),
        ],
        out_specs=[
            pl.BlockSpec((tm, tn), lambda i, j: (i, zc(j))),
            pl.BlockSpec((tm, tn), lambda i, j: (i, cc(j))),
            pl.BlockSpec((V7X_SUBLANES, tn), lambda i, j: (i, cc(j))),
            pl.BlockSpec((tm, hp), lambda i, j: (i, 0)),
            pl.BlockSpec((hp, tm), lambda i, j: (0, i)),
        ],
        out_shape=[
            jax.ShapeDtypeStruct((n_prompt_rows, d_inner), BF16),
            jax.ShapeDtypeStruct((n_prompt_rows, conv_dim), BF16),
            jax.ShapeDtypeStruct((npt * V7X_SUBLANES, conv_dim), F32),
            jax.ShapeDtypeStruct((n_prompt_rows, hp), F32),
            jax.ShapeDtypeStruct((hp, n_prompt_rows), F32),
        ],
        scratch_shapes=[pltpu.VMEM((nc, HISTORY_ROWS, tn), F32),
                        pltpu.VMEM((HISTORY_ROWS + tm, tn), F32)],
        compiler_params=_params("arbitrary", "arbitrary"),
        name="ssd_in_prompt",
    )(h_p, w_b, w_dt_t, dt_b, taps, conv_b)
    return ((zs_p, zs_s), (xbc_p, xbc_s), (dt_p, dt_s), (dtT_p, dtT_s)), tail, nstate


def _ssd_group_out(x, zs, cb, acum, acumT, dtT, mask, extra, ng, head0, heads_per_group, head_dim,
                   carried=None):
    heads_per_slab = V7X_LANES // head_dim
    parts = []
    for q in range(heads_per_group // heads_per_slab):
        cols = slice(q * V7X_LANES, (q + 1) * V7X_LANES)
        rhs = x[:, cols]
        if carried is not None:
            cg, st_t = carried
            rhs = jnp.concatenate([rhs, st_t[:, cols].astype(BF16)], axis=0)
        lanes = lax.broadcasted_iota(jnp.int32, rhs.shape, 1)
        acc = None
        for r in range(heads_per_slab):
            hd = head0 + q * heads_per_slab + r
            a_t = jnp.broadcast_to(acum[:, hd:hd + 1], cb.shape)
            seg = a_t - acumT[hd:hd + 1, :]
            lhs = (cb * jnp.exp(jnp.where(mask, seg, MASKED)) * dtT[hd:hd + 1, :]).astype(BF16)
            if carried is not None:
                lhs = jnp.concatenate([lhs, (cg * jnp.exp(a_t)).astype(BF16)], axis=1)
            in_head = jnp.logical_and(lanes >= r * head_dim, lanes < (r + 1) * head_dim)
            part = _dot(lhs, jnp.where(in_head, rhs, jnp.zeros_like(rhs)))
            acc = part if acc is None else acc + part
        parts.append(acc)
    y = jnp.concatenate(parts, axis=1) + extra
    gated = y * zs
    ms = jnp.mean(gated * gated, axis=-1, keepdims=True)
    return (gated * lax.rsqrt(ms + EPS) * ng).astype(BF16)


def _ssd_prompt_body(c, n_chunks, xs_ref, b_ref, c_ref, zs_ref, dt_ref, dtT_ref, alr_ref, alc_ref,
                     e_ref, d_ref, ng_ref, y_ref, state_ref, st_sc, *, n_groups, head_dim, d_state):
    @pl.when(c == 0)
    def _():
        st_sc[...] = jnp.zeros_like(st_sc)

    q_rows, d_inner = xs_ref.shape
    gw = d_inner // n_groups
    hpg = gw // head_dim
    row = lax.broadcasted_iota(jnp.int32, (q_rows, q_rows), 0)
    col = lax.broadcasted_iota(jnp.int32, (q_rows, q_rows), 1)
    causal = col <= row
    tril = jnp.where(causal, 1.0, 0.0).astype(BF16)
    triu = jnp.where(row <= col, 1.0, 0.0).astype(BF16)
    dt = dt_ref[...]
    dtT = dtT_ref[...]
    acum = _dot01_lhs(tril, dt * -jnp.exp(alr_ref[...]))
    acumT = _dot01_rhs(dtT * -jnp.exp(alc_ref[...]), triu)
    a_end = acum[q_rows - 1:q_rows, :]
    e = e_ref[...]
    x = xs_ref[...]
    xf = x.astype(F32)
    to_end = (xf * _dot((dt * jnp.exp(a_end - acum)).astype(BF16), e)).astype(BF16)
    decay = _dot01_rhs(jnp.broadcast_to(jnp.exp(a_end), (V7X_SUBLANES, a_end.shape[1])), e)[:1, :]
    skip = xf * d_ref[...]
    for g in range(n_groups):
        sl = slice(g * gw, (g + 1) * gw)
        ns = slice(g * d_state, (g + 1) * d_state)
        bg, cg = b_ref[:, ns], c_ref[:, ns]
        st = st_sc[:, sl]
        y_ref[:, sl] = _ssd_group_out(x[:, sl], zs_ref[:, sl].astype(F32), _dot_nt(cg, bg), acum,
                                      acumT, dtT, causal, skip[:, sl], ng_ref[:, sl],
                                      g * hpg, hpg, head_dim,
                                      carried=(cg.astype(F32), st))
        st_sc[:, sl] = decay[:, sl] * st + _dot_tn(bg, to_end[:, sl])

    @pl.when(c == n_chunks - 1)
    def _():
        for g in range(n_groups):
            state_ref[g * gw:(g + 1) * gw, :] = st_sc[:, g * gw:(g + 1) * gw].T


def _ssd_sample_body(xs_ref, b_ref, c_ref, zs_ref, dt_ref, dtT_ref, alr_ref, alc_ref, e_ref,
                     d_ref, ng_ref, st_ref, y_ref, nst_ref, *, seq_len, head_dim):
    q_rows, gw = xs_ref.shape
    n_seq = q_rows // seq_len
    row = lax.broadcasted_iota(jnp.int32, (q_rows, q_rows), 0)
    col = lax.broadcasted_iota(jnp.int32, (q_rows, q_rows), 1)
    same = (row // seq_len) == (col // seq_len)
    mask = jnp.logical_and(same, col <= row)
    tril = jnp.where(mask, 1.0, 0.0).astype(BF16)
    triu = jnp.where(jnp.logical_and(same, row <= col), 1.0, 0.0).astype(BF16)
    ends = jnp.where(col == (row // seq_len) * seq_len + (seq_len - 1), 1.0, 0.0).astype(BF16)
    dt = dt_ref[...]
    dtT = dtT_ref[...]
    acum = _dot01_lhs(tril, dt * -jnp.exp(alr_ref[...]))
    acumT = _dot01_rhs(dtT * -jnp.exp(alc_ref[...]), triu)
    a_end = _dot01_lhs(ends, acum)
    e = e_ref[...]
    x = xs_ref[...].astype(F32)
    to_endT = (x * _dot((dt * jnp.exp(a_end - acum)).astype(BF16), e)).T.astype(BF16)
    decayT = _dot01_rhs(jnp.exp(a_end), e).T
    from_start = _spread(jnp.exp(acum), e)
    bg = b_ref[...].astype(F32)
    cg = c_ref[...].astype(F32)
    seq_of_row = lax.broadcasted_iota(jnp.int32, bg.shape, 0) // seq_len
    inter = jnp.zeros((q_rows, gw), F32)
    for s in range(n_seq):
        st = st_ref[s]
        mine = seq_of_row == s
        inter = inter + _dot_nt(jnp.where(mine, cg, 0.0).astype(BF16), st.astype(BF16))
        bm = jnp.where(mine, bg, 0.0).astype(BF16)
        nst_ref[s] = decayT[:, s * seq_len:s * seq_len + 1] * st + _dot(to_endT, bm)
    inter = from_start * inter + x * d_ref[...]
    y_ref[...] = _ssd_group_out(xs_ref[...], zs_ref[...].astype(F32), _dot_nt(c_ref[...], b_ref[...]),
                                acum, acumT, dtT, mask, inter, ng_ref[...], 0, gw // head_dim, head_dim)


N_PROMPT_SCAN_INPUTS = 11
N_SAMPLE_SCAN_INPUTS = 12


def _ssd_scan_kernel(*refs, n_chunks, n_groups, seq_len_s, head_dim, d_state):
    a, b = N_PROMPT_SCAN_INPUTS, N_PROMPT_SCAN_INPUTS + N_SAMPLE_SCAN_INPUTS
    prompt_in, sample_in = refs[:a], refs[a:b]
    y_p, state_p, y_s, state_s, st_sc = refs[b:]
    _ssd_prompt_body(pl.program_id(0) % n_chunks, n_chunks, *prompt_in, y_p, state_p, st_sc,
                     n_groups=n_groups, head_dim=head_dim, d_state=d_state)
    _ssd_sample_body(*sample_in, y_s, state_s, seq_len=seq_len_s, head_dim=head_dim)


def _ssd_scan(xbc_p, zs_p, dt, dtT, alog_row, alog_col, expand, d_x, ng,
              xbc_s, zs_s, dt_g, dtT_g, alog_row_g, alog_col_g, state,
              *, n_seq_p, seq_len_p, seq_len_s, d_inner, n_groups, head_dim, d_state):
    q = SSD_CHUNK
    n_rows_s = xbc_s.shape[0]
    assert seq_len_p % q == 0 and q == d_state
    assert n_rows_s % q == 0 and q % seq_len_s == 0
    nc = seq_len_p // q
    nb = n_rows_s // q
    assert n_seq_p * nc == nb * n_groups
    spb = q // seq_len_s
    gw = d_inner // n_groups
    hp = dt.shape[1]
    hpg_rows = dtT_g.shape[1]
    gn = n_groups * d_state
    assert d_inner % gn == 0
    b_col0 = d_inner // d_state
    const = lambda t: (0, 0)
    sb = lambda t: t // n_groups
    g = lambda t: t % n_groups
    prompt_specs = [
        pl.BlockSpec((q, d_inner), lambda t: (t, 0)),
        pl.BlockSpec((q, gn), lambda t: (t, d_inner // gn)),
        pl.BlockSpec((q, gn), lambda t: (t, d_inner // gn + 1)),
        pl.BlockSpec((q, d_inner), lambda t: (t, 0)),
        pl.BlockSpec((q, hp), lambda t: (t, 0)),
        pl.BlockSpec((hp, q), lambda t: (0, t)),
        pl.BlockSpec((1, hp), const),
        pl.BlockSpec((hp, 1), const),
        pl.BlockSpec((hp, d_inner), const),
        pl.BlockSpec((1, d_inner), const),
        pl.BlockSpec((1, d_inner), const),
    ]
    sample_specs = [
        pl.BlockSpec((q, gw), lambda t: (sb(t), g(t))),
        pl.BlockSpec((q, d_state), lambda t: (sb(t), b_col0 + g(t))),
        pl.BlockSpec((q, d_state), lambda t: (sb(t), b_col0 + n_groups + g(t))),
        pl.BlockSpec((q, gw), lambda t: (sb(t), g(t))),
        pl.BlockSpec((None, q, hp), lambda t: (g(t), sb(t), 0)),
        pl.BlockSpec((None, hpg_rows, q), lambda t: (g(t), 0, sb(t))),
        pl.BlockSpec((None, 1, hp), lambda t: (g(t), 0, 0)),
        pl.BlockSpec((None, hpg_rows, 1), lambda t: (g(t), 0, 0)),
        pl.BlockSpec((hp, gw), const),
        pl.BlockSpec((1, gw), lambda t: (0, g(t))),
        pl.BlockSpec((1, gw), lambda t: (0, g(t))),
        pl.BlockSpec((spb, None, gw, d_state), lambda t: (sb(t), g(t), 0, 0)),
    ]
    assert len(prompt_specs) == N_PROMPT_SCAN_INPUTS and len(sample_specs) == N_SAMPLE_SCAN_INPUTS
    kern = functools.partial(_ssd_scan_kernel, n_chunks=nc, n_groups=n_groups, seq_len_s=seq_len_s,
                             head_dim=head_dim, d_state=d_state)
    return pl.pallas_call(
        kern,
        grid=(n_seq_p * nc,),
        in_specs=prompt_specs + sample_specs,
        out_specs=[
            pl.BlockSpec((q, d_inner), lambda t: (t, 0)),
            pl.BlockSpec((d_inner, d_state), lambda t: (t // nc, 0)),
            pl.BlockSpec((q, gw), lambda t: (sb(t), g(t))),
            pl.BlockSpec((spb, None, gw, d_state), lambda t: (sb(t), g(t), 0, 0)),
        ],
        out_shape=[
            jax.ShapeDtypeStruct((n_seq_p * seq_len_p, d_inner), BF16),
            jax.ShapeDtypeStruct((n_seq_p * d_inner, d_state), F32),
            jax.ShapeDtypeStruct((n_rows_s, d_inner), BF16),
            jax.ShapeDtypeStruct(state.shape, F32),
        ],
        scratch_shapes=[pltpu.VMEM((d_state, d_inner), F32)],
        compiler_params=_params("arbitrary"),
        name="ssd_scan",
    )(xbc_p, xbc_p, xbc_p, zs_p, dt, dtT, alog_row, alog_col, expand, d_x, ng,
      xbc_s, xbc_s, xbc_s, zs_s, dt_g, dtT_g, alog_row_g, alog_col_g, expand[:, :gw], d_x, ng, state)


def _prompt_conv_state(tail, *, n_prompt_tiles, tiles_per_seq, km1):
    t = tail.reshape(-1, V7X_SUBLANES, tail.shape[1])[:n_prompt_tiles]
    return t[tiles_per_seq - 1::tiles_per_seq, V7X_SUBLANES - km1:, :]


def kernel(x_prompt, x_sample, p_prompt, p_sample, state_sc_conv, state_ssd_conv, state_ssd, g_mix, g_ffn, g_ple, g_final, sc_w_in, sc_w_conv, sc_w_out, ssd_w_in, ssd_conv_w, ssd_conv_b, ssd_dt_bias, ssd_a_log, ssd_d, ssd_norm_g, ssd_w_out, ffn_w_gate, ffn_w_up, ffn_w_down, ple_w_proj, ple_w_gate):
    bp, lp, d = x_prompt.shape
    bs, ls, _ = x_sample.shape
    depth = g_mix.shape[0]
    mp, ms = bp * lp, bs * ls
    pdim = p_prompt.shape[-1]
    n_heads, head_dim, d_state = state_ssd.shape[2:]
    d_inner = n_heads * head_dim
    conv_dim = ssd_conv_w.shape[-1]
    n_groups = (conv_dim - d_inner) // (2 * d_state)
    hpg = n_heads // n_groups
    assert n_heads <= V7X_LANES and V7X_LANES % head_dim == 0 and d_state == V7X_LANES
    npt = mp // ROW_TILE
    tps = lp // ROW_TILE
    row = lambda v: v.reshape(1, -1)
    pp = p_prompt.reshape(depth, mp, pdim)
    ps = p_sample.reshape(depth, ms, pdim)

    h = (x_prompt.reshape(mp, d), x_sample.reshape(ms, d))
    hn = None
    sc_p, sc_s, cv_p, cv_s, st_p, st_s = [], [], [], [], [], []
    for i in range(depth):
        j = i // 2
        if i % 2 == 0:
            km1 = sc_w_conv.shape[1] - 1
            gated, tail, nstate = _short_conv_in(
                *h, row(g_mix[i]), sc_w_in, j, sc_w_conv, jnp.swapaxes(state_sc_conv, 1, 2),
                seq_len_p=lp, seq_len_s=ls)
            sc_p.append(_prompt_conv_state(tail, n_prompt_tiles=npt, tiles_per_seq=tps, km1=km1))
            sc_s.append(jnp.swapaxes(nstate, 0, 1))
            h = _matmul_residual(h, gated, sc_w_out, j)
        else:
            km1 = ssd_conv_w.shape[1] - 1
            zx = d_inner + conv_dim
            pad_h = V7X_LANES - n_heads
            w_in_t = jnp.swapaxes(ssd_w_in, 1, 2)
            w_dt_t = jnp.pad(w_in_t[j, zx:, :], ((0, pad_h), (0, 0))).astype(BF16)
            dt_b = jnp.pad(ssd_dt_bias[j], (0, pad_h))
            alog = jnp.pad(ssd_a_log[j], (0, pad_h))
            assert hn is not None
            (zs, xbc, dt, dtT), tail, nstate = _ssd_in(
                *hn, w_in_t, j, w_dt_t, row(dt_b),
                ssd_conv_w, ssd_conv_b.reshape(ssd_conv_b.shape[0], 1, conv_dim),
                jnp.swapaxes(state_ssd_conv, 1, 2),
                seq_len_p=lp, seq_len_s=ls, d_inner=d_inner)
            cv_p.append(_prompt_conv_state(tail, n_prompt_tiles=npt, tiles_per_seq=tps, km1=km1))
            cv_s.append(jnp.swapaxes(nstate, 0, 1))
            head_of_lane = jnp.arange(d_inner, dtype=jnp.int32) // head_dim
            expand = (jnp.arange(V7X_LANES, dtype=jnp.int32)[:, None] == head_of_lane[None, :]).astype(BF16)
            d_x = row(jnp.repeat(ssd_d[j], head_dim))
            ng = row(ssd_norm_g[j])
            dt_g = jnp.stack([jnp.roll(dt[1], -g * hpg, axis=1) for g in range(n_groups)])
            alog_g = jnp.stack([jnp.roll(alog, -g * hpg) for g in range(n_groups)])
            dtT_g = dtT[1][:n_heads].reshape(n_groups, hpg, ms)
            y_p, new_p, y_s, new_s = _ssd_scan(
                xbc[0], zs[0], dt[0], dtT[0], row(alog), alog.reshape(-1, 1), expand, d_x, ng,
                xbc[1], zs[1], dt_g, dtT_g, alog_g.reshape(n_groups, 1, -1),
                ssd_a_log[j].reshape(n_groups, hpg, 1),
                state_ssd[j].reshape(bs, n_groups, hpg * head_dim, d_state),
                n_seq_p=bp, seq_len_p=lp, seq_len_s=ls,
                d_inner=d_inner, n_groups=n_groups, head_dim=head_dim, d_state=d_state)
            st_p.append(new_p.reshape(bp, n_heads, head_dim, d_state))
            st_s.append(new_s.reshape(bs, n_heads, head_dim, d_state))
            h = _matmul_residual(h, (y_p, y_s), ssd_w_out, j)
        h = _ffn(h, row(g_ffn[i]), ffn_w_gate, ffn_w_up, ffn_w_down, i)
        if i == depth - 1:
            (h,) = _ple(h, pp, ps, row(g_ple[i]), ple_w_gate, ple_w_proj, i, row(g_final), True)
        else:
            h, hn = _ple(h, pp, ps, row(g_ple[i]), ple_w_gate, ple_w_proj, i, row(g_mix[i + 1]), False)
    y_p, y_s = h
    return (y_p.reshape(bp, lp, d), y_s.reshape(bs, ls, d), jnp.stack(sc_p), jnp.stack(sc_s),
            jnp.stack(cv_p), jnp.stack(cv_s), jnp.stack(st_p), jnp.stack(st_s))
```
